```python
import jax, jax.numpy as jnp
from jax import lax
import numpy as np

D_MODEL = 1024
BATCH = 8
SEQ = 2048
DEPTH = 2

PLE_DIM = 256
D_A = D_MODEL // 2
NH_A = 8
G_A = 2
HPG_A = NH_A // G_A
DK_A = D_A // NH_A
KV_A = G_A * DK_A
L_CMP = 32
STRIDE_CMP = 16
CMP_HID = 128
L_SEL = 64
N_SEL = 8
WINDOW = 256
Q_BLOCK = 128
BIG = 1e9
D_B = D_MODEL // 2
CHUNK_B = 128
G_B = 4
CH_B = D_B // G_B
D_C = D_MODEL // 2
NH_C = 4
DH_C = D_C // NH_C
CONV_W = 4
CHUNK_C = 128
N_BRANCH = 3
ALPHA = (2.0 * DEPTH) ** 0.25
BETA = (8.0 * DEPTH) ** -0.25
LN_EPS = 1e-5
IN_SIZES = (D_A, KV_A, KV_A, KV_A, KV_A, KV_A, KV_A, 3 * NH_A, D_A,
            D_B, D_B, D_B,
            D_C, D_C, D_C, 2 * NH_C, D_C, D_C,
            N_BRANCH * D_MODEL)
N_IN = sum(IN_SIZES)

kernel_name = 'hybrid_nsa_gmlp_mlstm'


def _layernorm(x, g, b):
    xf = x.astype(jnp.float32)
    mu = jnp.mean(xf, -1, keepdims=True)
    var = jnp.mean(jnp.square(xf - mu), -1, keepdims=True)
    return ((xf - mu) * lax.rsqrt(var + LN_EPS) * g + b).astype(x.dtype)


def _masked_softmax(s, mask):
    s = jnp.where(mask, s, -jnp.inf)
    m = jnp.max(s, axis=-1, keepdims=True)
    m = jnp.where(jnp.isfinite(m), m, 0.0)
    e = jnp.where(mask, jnp.exp(s - m), 0.0)
    return e / jnp.maximum(jnp.sum(e, -1, keepdims=True), 1e-30)


def _alibi_slopes(n):
    return 2.0 ** (-8.0 * jnp.arange(1, n + 1, dtype=jnp.float32) / n)


def _nsa(q, kc, vc, ks, vs, kw, vw, g, pos_k, pos_v, wk1, wk2, wv1, wv2):
    bsz, t_len, _ = q.shape
    f32 = jnp.float32
    qh = (q * DK_A ** -0.5).reshape(bsz, t_len, G_A, HPG_A, DK_A).transpose(0, 2, 3, 1, 4)
    slopes = _alibi_slopes(NH_A).reshape(G_A, HPG_A, 1, 1)
    tpos = jnp.arange(t_len)

    def kv_heads(z):
        return z.reshape(bsz, t_len, G_A, DK_A)

    n_cmp = (t_len - L_CMP) // STRIDE_CMP + 1
    cmp_start = jnp.arange(n_cmp) * STRIDE_CMP
    blk = cmp_start[:, None] + jnp.arange(L_CMP)[None, :]

    def compress(z, pos, w1, w2):
        zb = kv_heads(z)[:, blk] + pos[:, None, :]
        zb = zb.transpose(0, 1, 3, 2, 4).reshape(bsz, n_cmp, G_A, L_CMP * DK_A)
        return (jax.nn.gelu(zb @ w1) @ w2).transpose(0, 2, 1, 3)

    k_cmp = compress(kc, pos_k, wk1, wk2)
    v_cmp = compress(vc, pos_v, wv1, wv2)
    d_cmp = tpos[:, None] - (cmp_start + L_CMP - 1)[None, :]
    s_cmp = jnp.einsum('bghtd,bgnd->bghtn', qh, k_cmp).astype(f32) - slopes * d_cmp
    p_cmp = _masked_softmax(s_cmp, d_cmp >= 0)
    o_cmp = jnp.einsum('bghtn,bgnd->bghtd', p_cmp.astype(v_cmp.dtype), v_cmp)

    n_blk = t_len // L_SEL
    sel_start = jnp.arange(n_blk) * L_SEL
    overlap = ((cmp_start[:, None] < sel_start[None, :] + L_SEL)
               & (cmp_start[:, None] + L_CMP > sel_start[None, :])).astype(f32)
    imp = jnp.einsum('bghtn,nj->bgtj', p_cmp, overlap)
    cur = (tpos // L_SEL)[:, None]
    jb = jnp.arange(n_blk)[None, :]
    forced = (jb == 0) | (jb == cur) | (jb == cur - 1)
    score = jnp.where(jb <= cur, jnp.where(forced, BIG, imp), -BIG)
    k_top = min(N_SEL, n_blk)
    top_score, top_idx = lax.top_k(score, k_top)
    top_valid = top_score > -0.5 * BIG
    n_tok = k_top * L_SEL

    ks_h = kv_heads(ks).transpose(0, 2, 1, 3)
    vs_h = kv_heads(vs).transpose(0, 2, 1, 3)
    pad = ((0, 0), (0, 0), (WINDOW, 0), (0, 0))
    kw_h = jnp.pad(kv_heads(kw).transpose(0, 2, 1, 3), pad)
    vw_h = jnp.pad(kv_heads(vw).transpose(0, 2, 1, 3), pad)

    def query_block(qb):
        t0 = qb * Q_BLOCK
        tq = t0 + jnp.arange(Q_BLOCK)
        q_blk = lax.dynamic_slice_in_dim(qh, t0, Q_BLOCK, axis=3)
        idx = lax.dynamic_slice_in_dim(top_idx, t0, Q_BLOCK, axis=2)
        val = lax.dynamic_slice_in_dim(top_valid, t0, Q_BLOCK, axis=2)
        tok = (idx[..., None] * L_SEL + jnp.arange(L_SEL)).reshape(bsz, G_A, Q_BLOCK, n_tok)
        m_sel = jnp.repeat(val, L_SEL, axis=-1) & (tok <= tq[:, None])
        flat = tok.reshape(bsz, G_A, Q_BLOCK * n_tok, 1)
        k_g = jnp.take_along_axis(ks_h, flat, axis=2).reshape(bsz, G_A, Q_BLOCK, n_tok, DK_A)
        v_g = jnp.take_along_axis(vs_h, flat, axis=2).reshape(bsz, G_A, Q_BLOCK, n_tok, DK_A)
        d_sel = (tq[:, None] - tok)[:, :, None]
        s_sel = jnp.einsum('bghqd,bgqsd->bghqs', q_blk, k_g).astype(f32) - slopes * d_sel
        p_sel = _masked_softmax(s_sel, m_sel[:, :, None])
        o_sel = jnp.einsum('bghqs,bgqsd->bghqd', p_sel.astype(v_g.dtype), v_g)
        k_w = lax.dynamic_slice_in_dim(kw_h, t0, WINDOW + Q_BLOCK, axis=2)
        v_w = lax.dynamic_slice_in_dim(vw_h, t0, WINDOW + Q_BLOCK, axis=2)
        sk = t0 - WINDOW + jnp.arange(WINDOW + Q_BLOCK)
        d_w = tq[:, None] - sk[None, :]
        m_w = (d_w >= 0) & (d_w < WINDOW) & (sk[None, :] >= 0)
        s_w = jnp.einsum('bghqd,bgkd->bghqk', q_blk, k_w).astype(f32) - slopes * d_w
        p_w = _masked_softmax(s_w, m_w)
        o_w = jnp.einsum('bghqk,bgkd->bghqd', p_w.astype(v_w.dtype), v_w)
        return o_sel, o_w

    o_sel, o_win = lax.map(query_block, jnp.arange(t_len // Q_BLOCK))
    o_sel = o_sel.transpose(1, 2, 3, 0, 4, 5).reshape(bsz, G_A, HPG_A, t_len, DK_A)
    o_win = o_win.transpose(1, 2, 3, 0, 4, 5).reshape(bsz, G_A, HPG_A, t_len, DK_A)
    gates = jax.nn.sigmoid(g).reshape(bsz, t_len, 3, G_A, HPG_A).transpose(2, 0, 3, 4, 1)[..., None]
    o = gates[0] * o_cmp + gates[1] * o_sel + gates[2] * o_win
    return o.transpose(0, 3, 1, 2, 4).reshape(bsz, t_len, D_A)


def _spatial_gating(u, v, ln_g, ln_b, w_s, b_s):
    bsz, t_len, _ = u.shape
    u = jax.nn.gelu(u)
    v = _layernorm(jax.nn.gelu(v), ln_g, ln_b)
    v = v.reshape(bsz, t_len // CHUNK_B, CHUNK_B, G_B, CH_B)
    w = w_s * jnp.tril(jnp.ones((CHUNK_B, CHUNK_B), w_s.dtype))
    mixed = jnp.einsum('gts,bcsge->bctge', w, v) + b_s.T[:, :, None]
    return u * mixed.reshape(bsz, t_len, D_B)


def _mlstm(q, k, v, gif, o, conv_w, conv_b, b_i, b_f, norm_g):
    bsz, t_len, _ = q.shape
    dt = q.dtype
    f32 = jnp.float32
    qk = jnp.concatenate([q, k], -1)
    qk = lax.conv_general_dilated(qk, conv_w[:, None, :], window_strides=(1,), padding=[(CONV_W - 1, 0)],
                                  dimension_numbers=('NWC', 'WIO', 'NWC'), feature_group_count=2 * D_C) + conv_b
    q, k = jnp.split(jax.nn.silu(qk), 2, axis=-1)
    nc = t_len // CHUNK_C

    def hc(z):
        return z.reshape(bsz, nc, CHUNK_C, NH_C, DH_C).transpose(0, 3, 1, 2, 4).astype(f32)

    def hg(z):
        return z.reshape(bsz, nc, CHUNK_C, NH_C).transpose(0, 3, 1, 2)

    qh, kh, vh = hc(q), hc(k) * DH_C ** -0.5, hc(v)
    gi, gf = jnp.split(gif.astype(f32), 2, axis=-1)
    log_i = hg(gi + b_i)
    log_f = jax.nn.log_sigmoid(hg(gf + b_f))
    bcum = jnp.cumsum(log_f, axis=-1)
    b_last = bcum[..., -1]
    causal = jnp.tril(jnp.ones((CHUNK_C, CHUNK_C), bool))
    dmat = jnp.where(causal, bcum[..., :, None] - bcum[..., None, :] + log_i[..., None, :], -jnp.inf)
    w_end = b_last[..., None] - bcum + log_i
    m_loc = jnp.max(w_end, -1)
    e_end = jnp.exp(w_end - m_loc[..., None])
    c_loc = jnp.einsum('bhcs,bhcsv,bhcsk->bhcvk', e_end, vh, kh)
    n_loc = jnp.einsum('bhcs,bhcsk->bhck', e_end, kh)

    def step(carry, xs):
        c_st, n_st, m_st = carry
        cl, nl, ml, bl = xs
        m_new = jnp.maximum(bl + m_st, ml)
        a = jnp.exp(bl + m_st - m_new)
        bb = jnp.exp(ml - m_new)
        c_new = a[..., None, None] * c_st + bb[..., None, None] * cl
        n_new = a[..., None] * n_st + bb[..., None] * nl
        return (c_new, n_new, m_new), (c_st, n_st, m_st)

    init = (jnp.zeros((bsz, NH_C, DH_C, DH_C), f32), jnp.zeros((bsz, NH_C, DH_C), f32), jnp.zeros((bsz, NH_C), f32))
    xs = (c_loc.transpose(2, 0, 1, 3, 4), n_loc.transpose(2, 0, 1, 3), m_loc.transpose(2, 0, 1), b_last.transpose(2, 0, 1))
    _, (c_prev, n_prev, m_prev) = lax.scan(step, init, xs)
    c_prev = c_prev.transpose(1, 2, 0, 3, 4)
    n_prev = n_prev.transpose(1, 2, 0, 3)
    m_prev = m_prev.transpose(1, 2, 0)
    inter = bcum + m_prev[..., None]
    m_t = jnp.maximum(inter, jnp.max(dmat, -1))
    s = jnp.einsum('bhctd,bhcsd->bhcts', qh, kh) * jnp.exp(dmat - m_t[..., None])
    e_int = jnp.exp(inter - m_t)
    num = jnp.einsum('bhcts,bhcsv->bhctv', s, vh) + e_int[..., None] * jnp.einsum('bhcvk,bhctk->bhctv', c_prev, qh)
    den = jnp.sum(s, -1) + e_int * jnp.einsum('bhck,bhctk->bhct', n_prev, qh)
    h = num / jnp.maximum(jnp.abs(den), jnp.exp(-m_t))[..., None]
    mu = jnp.mean(h, -1, keepdims=True)
    var = jnp.mean(jnp.square(h - mu), -1, keepdims=True)
    hn = ((h - mu) * lax.rsqrt(var + LN_EPS)).transpose(0, 2, 3, 1, 4).reshape(bsz, t_len, D_C) * norm_g
    return (hn * jax.nn.sigmoid(o.astype(f32))).astype(dt)


def setup_inputs(seed: int = 0) -> dict:
    key = jax.random.key(seed)
    ks = jax.random.split(key, 32)
    f32 = jnp.float32

    def nrm(k, shape, scale):
        return jax.random.normal(k, shape, f32) * scale

    return {
        'x': nrm(ks[0], (BATCH, SEQ, D_MODEL), 1.0),
        'p': nrm(ks[1], (DEPTH, BATCH, SEQ, PLE_DIM), 1.0),
        'w_in': nrm(ks[2], (DEPTH, D_MODEL, N_IN), D_MODEL ** -0.5),
        'cmp_pos_k': nrm(ks[3], (DEPTH, L_CMP, DK_A), 0.1),
        'cmp_pos_v': nrm(ks[4], (DEPTH, L_CMP, DK_A), 0.1),
        'cmp_wk1': nrm(ks[5], (DEPTH, L_CMP * DK_A, CMP_HID), (L_CMP * DK_A) ** -0.5),
        'cmp_wk2': nrm(ks[6], (DEPTH, CMP_HID, DK_A), CMP_HID ** -0.5),
        'cmp_wv1': nrm(ks[7], (DEPTH, L_CMP * DK_A, CMP_HID), (L_CMP * DK_A) ** -0.5),
        'cmp_wv2': nrm(ks[8], (DEPTH, CMP_HID, DK_A), CMP_HID ** -0.5),
        'sg_ln_g': 1.0 + nrm(ks[9], (DEPTH, D_B), 0.02),
        'sg_ln_b': nrm(ks[10], (DEPTH, D_B), 0.02),
        'sg_w': nrm(ks[11], (DEPTH, G_B, CHUNK_B, CHUNK_B), CHUNK_B ** -0.5),
        'sg_b': 1.0 + nrm(ks[12], (DEPTH, G_B, CHUNK_B), 0.02),
        'ml_conv_w': nrm(ks[13], (DEPTH, CONV_W, 2 * D_C), CONV_W ** -0.5),
        'ml_conv_b': nrm(ks[14], (DEPTH, 2 * D_C), 0.02),
        'ml_b_i': nrm(ks[15], (DEPTH, NH_C), 0.1),
        'ml_b_f': jnp.linspace(3.0, 6.0, NH_C, dtype=f32)[None, :] + nrm(ks[16], (DEPTH, NH_C), 0.1),
        'ml_norm_g': 1.0 + nrm(ks[17], (DEPTH, D_C), 0.02),
        'w_br_a': nrm(ks[18], (DEPTH, D_A, D_MODEL), BETA * D_A ** -0.5),
        'w_br_b': nrm(ks[19], (DEPTH, D_B, D_MODEL), BETA * D_B ** -0.5),
        'w_br_c': nrm(ks[20], (DEPTH, D_C, D_MODEL), BETA * D_C ** -0.5),
        'w_out': nrm(ks[21], (DEPTH, D_MODEL, D_MODEL), BETA * D_MODEL ** -0.5),
        'ple_w': nrm(ks[22], (DEPTH, PLE_DIM, D_MODEL), BETA * PLE_DIM ** -0.5),
        'ple_gate': nrm(ks[23], (DEPTH, D_MODEL, D_MODEL), D_MODEL ** -0.5),
        'ln_g': 1.0 + nrm(ks[24], (DEPTH, D_MODEL), 0.02),
        'ln_b': nrm(ks[25], (DEPTH, D_MODEL), 0.02),
    }


def reference(x, p, w_in, cmp_pos_k, cmp_pos_v, cmp_wk1, cmp_wk2, cmp_wv1, cmp_wv2,
              sg_ln_g, sg_ln_b, sg_w, sg_b, ml_conv_w, ml_conv_b, ml_b_i, ml_b_f, ml_norm_g,
              w_br_a, w_br_b, w_br_c, w_out, ple_w, ple_gate, ln_g, ln_b):
    bsz, t_len = x.shape[0], x.shape[1]
    split_at = np.cumsum(IN_SIZES)[:-1].tolist()
    for i in range(DEPTH):
        (a_q, a_kc, a_vc, a_ks, a_vs, a_kw, a_vw, a_g, a_z,
         b_u, b_v, b_z,
         c_q, c_k, c_v, c_if, c_o, c_z, m_g) = jnp.split(x @ w_in[i], split_at, axis=-1)
        y_a = _nsa(a_q, a_kc, a_vc, a_ks, a_vs, a_kw, a_vw, a_g, cmp_pos_k[i], cmp_pos_v[i],
                   cmp_wk1[i], cmp_wk2[i], cmp_wv1[i], cmp_wv2[i]) * jax.nn.silu(a_z)
        y_b = _spatial_gating(b_u, b_v, sg_ln_g[i], sg_ln_b[i], sg_w[i], sg_b[i]) * jax.nn.silu(b_z)
        y_c = _mlstm(c_q, c_k, c_v, c_if, c_o, ml_conv_w[i], ml_conv_b[i], ml_b_i[i], ml_b_f[i],
                     ml_norm_g[i]) * jax.nn.silu(c_z)
        g = jax.nn.sigmoid(m_g).reshape(bsz, t_len, N_BRANCH, D_MODEL)
        merged = g[:, :, 0] * (y_a @ w_br_a[i]) + g[:, :, 1] * (y_b @ w_br_b[i]) + g[:, :, 2] * (y_c @ w_br_c[i])
        r = ALPHA * x + merged @ w_out[i]
        r = r + jax.nn.sigmoid(r @ ple_gate[i]) * (p[i] @ ple_w[i])
        x = _layernorm(r, ln_g[i], ln_b[i])
    return x
```

```python
import functools

import jax
import jax.numpy as jnp
from jax import lax
from jax.experimental import pallas as pl
from jax.experimental.pallas import tpu as pltpu

F32 = jnp.float32
BF16 = jnp.bfloat16

D_MODEL = 1024
PLE_DIM = 256
D_A = 512
NH_A = 8
G_A = 2
HPG_A = 4
DK_A = 64
L_CMP = 32
STRIDE_CMP = 16
CMP_HID = 128
L_SEL = 64
N_SEL = 8
WINDOW = 256
QB = 128
N_CMP_PAD = 128
BIG = 1e9
D_B = 512
CHUNK_B = 128
G_B = 4
D_C = 512
NH_C = 4
DH_C = 128
CONV_W = 4
CHUNK_C = 128
LN_EPS = 1e-5
DEPTH = 2
ALPHA = (2.0 * DEPTH) ** 0.25

LANE = 128
MAIN_W = 8192
TAIL_W = 896
GI_LANE = 32
GF_LANE = 36
VMEM_LIMIT = 52 * 1024 * 1024
NEG = -1e30


def _params(*sem):
    return pltpu.CompilerParams(dimension_semantics=sem, vmem_limit_bytes=VMEM_LIMIT)


def _nt_dot(a, b):
    return lax.dot_general(a, b, (((1,), (1,)), ((), ())), preferred_element_type=F32)


def _tn_dot(a, b):
    return lax.dot_general(a, b, (((0,), (0,)), ((), ())), preferred_element_type=F32)


def _dot(a, b):
    return jnp.dot(a, b, preferred_element_type=F32)


def _split3(a):
    a1 = a.astype(BF16)
    r1 = a - a1.astype(F32)
    a2 = r1.astype(BF16)
    a3 = (r1 - a2.astype(F32)).astype(BF16)
    return a1, a2, a3


def _log_sigmoid(x):
    return jnp.minimum(x, 0.0) - jnp.log1p(jnp.exp(-jnp.abs(x)))


def _mm_kernel(x_ref, w_ref, o_ref):
    o_ref[...] = _dot(x_ref[...], w_ref[...]).astype(o_ref.dtype)


def _matmul(x, w, tm, tn, name):
    m, k = x.shape
    n = w.shape[1]
    return pl.pallas_call(
        _mm_kernel,
        grid=(m // tm, n // tn),
        in_specs=[pl.BlockSpec((tm, k), lambda i, j: (i, 0)),
                  pl.BlockSpec((k, tn), lambda i, j: (0, j))],
        out_specs=pl.BlockSpec((tm, tn), lambda i, j: (i, j)),
        out_shape=jax.ShapeDtypeStruct((m, n), F32),
        compiler_params=_params("parallel", "arbitrary"),
        name=name,
    )(x, w)


def _compress_kernel(zk_ref, zv_ref, pk_ref, pv_ref, wk1_ref, wk2_ref, wv1_ref, wv2_ref, ko_ref, vo_ref):
    def one(z_ref, p_ref, w1_ref, w2_ref, o_ref):
        z = z_ref[0]
        r0 = _dot((z + p_ref[0:1, :]).astype(BF16), w1_ref[0])
        r1 = _dot((z + p_ref[1:2, :]).astype(BF16), w1_ref[1])
        hid = r0 + pltpu.roll(r1, r1.shape[0] - 1, 0)
        o_ref[0] = _dot(jax.nn.gelu(hid).astype(BF16), w2_ref[...])

    one(zk_ref, pk_ref, wk1_ref, wk2_ref, ko_ref)
    one(zv_ref, pv_ref, wv1_ref, wv2_ref, vo_ref)


def _compress(zk, zv, pk, pv, wk1, wk2, wv1, wv2):
    b, n, w = zk.shape
    zspec = pl.BlockSpec((1, n, w), lambda i: (i, 0, 0))
    full = lambda a: pl.BlockSpec(a.shape, lambda i: (0,) * a.ndim)
    ospec = pl.BlockSpec((1, n, LANE), lambda i: (i, 0, 0))
    return pl.pallas_call(
        _compress_kernel,
        grid=(b,),
        in_specs=[zspec, zspec, full(pk), full(pv), full(wk1), full(wk2), full(wv1), full(wv2)],
        out_specs=[ospec, ospec],
        out_shape=[jax.ShapeDtypeStruct((b, n, LANE), F32)] * 2,
        compiler_params=_params("parallel"),
        name="nsa_compress",
    )(zk, zv, pk, pv, wk1, wk2, wv1, wv2)


def _compress_weights(w1, w2, pos):
    eye = jnp.eye(G_A, dtype=F32)
    w1e = jnp.einsum('ldh,Gg->lGdgh', w1.reshape(L_CMP, DK_A, CMP_HID), eye)
    w1e = w1e.reshape(2, STRIDE_CMP * G_A * DK_A, G_A * CMP_HID).astype(BF16)
    w2e = jnp.einsum('hd,gG->ghGd', w2, eye).reshape(G_A * CMP_HID, G_A * DK_A).astype(BF16)
    pe = jnp.broadcast_to(pos[:, None, :], (L_CMP, G_A, DK_A)).reshape(2, STRIDE_CMP * G_A * DK_A)
    return w1e, w2e, pe


def _nsa_kernel(q_ref, z_ref, g_ref, kc_ref, vc_ref, ks_ref, vs_ref, kw_ref, vw_ref, o_ref,
                m_sc, l_sc, acc_sc):
    qb = pl.program_id(1)
    t0 = qb * QB
    rows = HPG_A * QB
    row = lax.broadcasted_iota(jnp.int32, (rows, 1), 0)
    trow = (t0 + (row & (QB - 1))).astype(F32)
    hrow = row >> 7
    kcol_i = lax.broadcasted_iota(jnp.int32, (1, LANE), 1)
    kcol = kcol_i.astype(F32)
    gsig = jax.nn.sigmoid(g_ref[...])

    n_blk_pad = 32
    jb = lax.broadcasted_iota(jnp.int32, (n_blk_pad, QB), 0)
    tq_row = t0 + lax.broadcasted_iota(jnp.int32, (n_blk_pad, QB), 1)
    cur = tq_row >> 6
    forced = (jb == 0) | (jb == cur) | (jb == cur - 1)
    allowed = jb <= cur
    ov_j = lax.broadcasted_iota(jnp.int32, (n_blk_pad, N_CMP_PAD), 0) * L_SEL
    ov_n = lax.broadcasted_iota(jnp.int32, (n_blk_pad, N_CMP_PAD), 1) * STRIDE_CMP
    ov_t = jnp.where((ov_n < ov_j + L_SEL) & (ov_n + L_CMP > ov_j), 1.0, 0.0).astype(BF16)
    e_row = lax.broadcasted_iota(jnp.int32, (LANE, LANE), 0)
    e_blk = lax.broadcasted_iota(jnp.int32, (LANE, LANE), 1) >> 6

    pieces = []
    for g in range(G_A):
        slopes = [2.0 ** -(g * HPG_A + h + 1) for h in range(HPG_A)]
        slope = jnp.where(hrow == 0, slopes[0],
                          jnp.where(hrow == 1, slopes[1], jnp.where(hrow == 2, slopes[2], slopes[3])))
        gl = slice(g * DK_A, (g + 1) * DK_A)
        q4 = jnp.concatenate(
            [q_ref[:, (g * HPG_A + h) * DK_A:(g * HPG_A + h + 1) * DK_A] for h in range(HPG_A)], axis=0)
        q4 = (q4 * DK_A ** -0.5).astype(BF16)

        s = _nt_dot(q4, kc_ref[0][:, gl].astype(BF16))
        d = trow - (kcol * STRIDE_CMP + (L_CMP - 1))
        valid = (d >= 0.0) & (kcol_i < N_CMP_PAD - 1)
        s = jnp.where(valid, s - slope * d, -jnp.inf)
        mx = jnp.max(s, axis=1, keepdims=True)
        mx = jnp.where(mx > -jnp.inf, mx, 0.0)
        e = jnp.where(valid, jnp.exp(s - mx), 0.0)
        p = e / jnp.maximum(jnp.sum(e, axis=1, keepdims=True), 1e-30)
        o_cmp = _dot(p.astype(BF16), vc_ref[0][:, gl].astype(BF16))

        psum = p[0:QB] + p[QB:2 * QB] + p[2 * QB:3 * QB] + p[3 * QB:4 * QB]
        p1, p2, p3 = _split3(psum)
        imp = _nt_dot(ov_t, p1) + _nt_dot(ov_t, p2) + _nt_dot(ov_t, p3)
        score = jnp.where(allowed, jnp.where(forced, BIG, imp), -BIG)
        cnt = jnp.zeros((n_blk_pad, QB), F32)
        for j in range(n_blk_pad):
            sj = score[j:j + 1, :]
            beats = (sj > score) | ((sj == score) & (jb > j))
            cnt = cnt + jnp.where(beats, 1.0, 0.0)
        sel_t = jnp.where((cnt < N_SEL) & allowed, 1.0, 0.0)
        sel_q = jnp.transpose(jnp.concatenate([sel_t, jnp.zeros((LANE - n_blk_pad, QB), F32)], axis=0))
        sel_q = sel_q.astype(BF16)

        m_sc[...] = jnp.full((rows, 1), NEG, F32)
        l_sc[...] = jnp.zeros((rows, 1), F32)
        acc_sc[...] = jnp.zeros((rows, DK_A), F32)

        def sel_body(c, carry):
            k0 = pl.multiple_of(c * LANE, LANE)
            kb = ks_ref[pl.ds(k0, LANE), gl].astype(BF16)
            vb = vs_ref[pl.ds(k0, LANE), gl].astype(BF16)
            sc = _nt_dot(q4, kb)
            dist = (trow - k0.astype(F32)) - kcol
            expand = jnp.where(e_row == 2 * c + e_blk, 1.0, 0.0).astype(BF16)
            chosen = _dot(sel_q, expand)
            chosen = jnp.concatenate([chosen] * HPG_A, axis=0)
            mask = (chosen > 0.5) & (dist >= 0.0)
            sc = jnp.where(mask, sc - slope * dist, NEG)
            m_old = m_sc[...]
            m_new = jnp.maximum(m_old, jnp.max(sc, axis=1, keepdims=True))
            a = jnp.exp(m_old - m_new)
            pc = jnp.where(mask, jnp.exp(sc - m_new), 0.0)
            l_sc[...] = a * l_sc[...] + jnp.sum(pc, axis=1, keepdims=True)
            acc_sc[...] = a * acc_sc[...] + _dot(pc.astype(BF16), vb)
            m_sc[...] = m_new
            return carry

        lax.fori_loop(0, qb + 1, sel_body, 0)
        o_sel = acc_sc[...] / jnp.maximum(l_sc[...], 1e-30)

        s_w, m_w, v_w = [], [], []
        for j in range(WINDOW // LANE + 1):
            c = qb - (WINDOW // LANE) + j
            k0 = pl.multiple_of(jnp.maximum(c, 0) * LANE, LANE)
            kb = kw_ref[pl.ds(k0, LANE), gl].astype(BF16)
            v_w.append(vw_ref[pl.ds(k0, LANE), gl].astype(BF16))
            pos = (c * LANE).astype(F32) + kcol
            dist = trow - pos
            mask = (dist >= 0.0) & (dist < WINDOW) & (pos >= 0.0)
            s_w.append(jnp.where(mask, _nt_dot(q4, kb) - slope * dist, -jnp.inf))
            m_w.append(mask)
        mx = jnp.max(jnp.maximum(jnp.maximum(s_w[0], s_w[1]), s_w[2]), axis=1, keepdims=True)
        e_w = [jnp.where(m_w[j], jnp.exp(s_w[j] - mx), 0.0) for j in range(3)]
        den = jnp.sum(e_w[0] + e_w[1] + e_w[2], axis=1, keepdims=True)
        inv = 1.0 / jnp.maximum(den, 1e-30)
        o_win = (_dot((e_w[0] * inv).astype(BF16), v_w[0]) + _dot((e_w[1] * inv).astype(BF16), v_w[1])
                 + _dot((e_w[2] * inv).astype(BF16), v_w[2]))

        for h in range(HPG_A):
            hs = slice(h * QB, (h + 1) * QB)
            col = g * HPG_A + h
            pieces.append(gsig[:, col:col + 1] * o_cmp[hs]
                          + gsig[:, NH_A + col:NH_A + col + 1] * o_sel[hs]
                          + gsig[:, 2 * NH_A + col:2 * NH_A + col + 1] * o_win[hs])

    o_ref[...] = jnp.concatenate(pieces, axis=1) * jax.nn.silu(z_ref[...])


def _nsa(main, tail, kcmp, vcmp, bsz, t_len):
    nqb = t_len // QB
    rows = HPG_A * QB
    seq = lambda col: pl.BlockSpec((t_len, LANE), lambda b, i: (b, col))
    cspec = pl.BlockSpec((1, N_CMP_PAD, LANE), lambda b, i: (b, 0, 0))
    return pl.pallas_call(
        _nsa_kernel,
        grid=(bsz, nqb),
        in_specs=[pl.BlockSpec((QB, D_A), lambda b, i: (b * nqb + i, 0)),
                  pl.BlockSpec((QB, D_A), lambda b, i: (b * nqb + i, 1)),
                  pl.BlockSpec((QB, LANE), lambda b, i: (b * nqb + i, 6)),
                  cspec, cspec, seq(2), seq(3), seq(4), seq(5)],
        out_specs=pl.BlockSpec((QB, D_A), lambda b, i: (b * nqb + i, 0)),
        out_shape=jax.ShapeDtypeStruct((bsz * t_len, D_A), F32),
        scratch_shapes=[pltpu.VMEM((rows, 1), F32), pltpu.VMEM((rows, 1), F32),
                        pltpu.VMEM((rows, DK_A), F32)],
        compiler_params=_params("parallel", "arbitrary"),
        name="nsa_attention",
    )(main, main, tail, kcmp, vcmp, tail, tail, tail, tail)


def _sgu_kernel(u_ref, v_ref, z_ref, lng_ref, lnb_ref, w_ref, b_ref, o_ref):
    u = jax.nn.gelu(u_ref[...])
    v = jax.nn.gelu(v_ref[...])
    mu = jnp.mean(v, axis=-1, keepdims=True)
    var = jnp.mean(jnp.square(v - mu), axis=-1, keepdims=True)
    vn = ((v - mu) * lax.rsqrt(var + LN_EPS) * lng_ref[...] + lnb_ref[...]).astype(BF16)
    gate = u * jax.nn.silu(z_ref[...])
    ti = lax.broadcasted_iota(jnp.int32, (CHUNK_B, CHUNK_B), 0)
    si = lax.broadcasted_iota(jnp.int32, (CHUNK_B, CHUNK_B), 1)
    n_chunks = u_ref.shape[0] // CHUNK_B
    for g in range(G_B):
        w = jnp.where(si <= ti, w_ref[g], 0.0).astype(BF16)
        bias = b_ref[:, g:g + 1]
        ls = slice(g * LANE, (g + 1) * LANE)
        for c in range(n_chunks):
            rs = slice(c * CHUNK_B, (c + 1) * CHUNK_B)
            o_ref[rs, ls] = gate[rs, ls] * (_dot(w, vn[rs, ls]) + bias)


def _sgu(main, lng, lnb, w, b_t, tm):
    n = main.shape[0]
    blk = lambda col: pl.BlockSpec((tm, D_B), lambda i: (i, col))
    full = lambda a: pl.BlockSpec(a.shape, lambda i: (0,) * a.ndim)
    return pl.pallas_call(
        _sgu_kernel,
        grid=(n // tm,),
        in_specs=[blk(2), blk(3), blk(4), full(lng), full(lnb), full(w), full(b_t)],
        out_specs=pl.BlockSpec((tm, D_B), lambda i: (i, 0)),
        out_shape=jax.ShapeDtypeStruct((n, D_B), F32),
        compiler_params=_params("parallel"),
        name="spatial_gating",
    )(main, main, main, lng, lnb, w, b_t)


def _mlstm_kernel(q_ref, k_ref, v_ref, o_ref, z_ref, g_ref, cwq_ref, cwk_ref, cbq_ref, cbk_ref,
                  gb_ref, ng_ref, y_ref, qc_sc, kc_sc):
    h = pl.program_id(1)
    t_len = q_ref.shape[0]
    ridx = lax.broadcasted_iota(jnp.int32, (t_len, 1), 0)

    def conv_silu(x_ref, w_ref, b_ref):
        x = x_ref[...]
        acc = x * w_ref[CONV_W - 1:CONV_W, :] + b_ref[...]
        for j in range(1, CONV_W):
            shifted = jnp.where(ridx >= j, pltpu.roll(x, j, 0), 0.0)
            acc = acc + shifted * w_ref[CONV_W - 1 - j:CONV_W - j, :]
        return jax.nn.silu(acc)

    qc_sc[...] = conv_silu(q_ref, cwq_ref, cbq_ref)
    kc_sc[...] = conv_silu(k_ref, cwk_ref, cbk_ref) * DH_C ** -0.5

    lane = lax.broadcasted_iota(jnp.int32, (CHUNK_C, LANE), 1)
    ti = lax.broadcasted_iota(jnp.int32, (CHUNK_C, CHUNK_C), 0)
    si = lax.broadcasted_iota(jnp.int32, (CHUNK_C, CHUNK_C), 1)
    eye = jnp.where(ti == si, 1.0, 0.0)
    causal = si <= ti
    tri_l = jnp.where(causal, 1.0, 0.0)
    tri_u = jnp.where(ti <= si, 1.0, 0.0)
    ng = ng_ref[...]
    gbias = gb_ref[...]

    def chunk(c, carry):
        c_st, n_st, m_st = carry
        sl = pl.ds(pl.multiple_of(c * CHUNK_C, CHUNK_C), CHUNK_C)
        q = qc_sc[sl, :]
        k = kc_sc[sl, :]
        v = v_ref[sl, :]
        gpre = g_ref[sl, :] + gbias
        li_col = jnp.sum(jnp.where(lane == GI_LANE + h, gpre, 0.0), axis=1, keepdims=True)
        lf_col = jnp.sum(jnp.where(lane == GF_LANE + h, _log_sigmoid(gpre), 0.0), axis=1, keepdims=True)
        li_row = jnp.sum(li_col * eye, axis=0, keepdims=True)
        lf_row = jnp.sum(lf_col * eye, axis=0, keepdims=True)
        bcum_row = jnp.sum(lf_col * tri_u, axis=0, keepdims=True)
        bcum_col = jnp.sum(lf_row * tri_l, axis=1, keepdims=True)
        b_last = jnp.sum(lf_col, axis=0, keepdims=True)
        w_end_row = b_last - bcum_row + li_row
        m_loc = jnp.max(w_end_row, axis=1, keepdims=True)
        e_end_col = jnp.exp(b_last - bcum_col + li_col - m_loc)

        dmat = jnp.where(causal, bcum_col - bcum_row + li_row, -jnp.inf)
        inter = bcum_col + m_st
        m_t = jnp.maximum(inter, jnp.max(dmat, axis=1, keepdims=True))
        qb16 = q.astype(BF16)
        s = _nt_dot(qb16, k.astype(BF16)) * jnp.exp(dmat - m_t)
        e_int = jnp.exp(inter - m_t)
        num = _dot(s.astype(BF16), v.astype(BF16)) + e_int * _dot(qb16, c_st.astype(BF16))
        den = jnp.sum(s, axis=1, keepdims=True) + e_int * jnp.sum(q * n_st, axis=1, keepdims=True)
        hval = num / jnp.maximum(jnp.abs(den), jnp.exp(-m_t))
        mu = jnp.mean(hval, axis=-1, keepdims=True)
        var = jnp.mean(jnp.square(hval - mu), axis=-1, keepdims=True)
        hn = (hval - mu) * lax.rsqrt(var + LN_EPS) * ng
        y_ref[sl, :] = hn * jax.nn.sigmoid(o_ref[sl, :]) * jax.nn.silu(z_ref[sl, :])

        kw = k * e_end_col
        c_loc = _tn_dot(kw.astype(BF16), v.astype(BF16))
        n_loc = jnp.sum(kw, axis=0, keepdims=True)
        m_new = jnp.maximum(b_last + m_st, m_loc)
        a = jnp.exp(b_last + m_st - m_new)
        bb = jnp.exp(m_loc - m_new)
        return a * c_st + bb * c_loc, a * n_st + bb * n_loc, m_new

    init = (jnp.zeros((DH_C, DH_C), F32), jnp.zeros((1, DH_C), F32), jnp.zeros((1, 1), F32))
    lax.fori_loop(0, t_len // CHUNK_C, chunk, init)


def _mlstm(main, tail, conv_w, conv_b, gbias, norm_g, bsz, t_len):
    col = lambda base: pl.BlockSpec((t_len, LANE), lambda b, h: (b, base + h))
    return pl.pallas_call(
        _mlstm_kernel,
        grid=(bsz, NH_C),
        in_specs=[col(20), col(24), col(28), col(32), col(36),
                  pl.BlockSpec((t_len, LANE), lambda b, h: (b, 6)),
                  pl.BlockSpec((CONV_W, LANE), lambda b, h: (0, h)),
                  pl.BlockSpec((CONV_W, LANE), lambda b, h: (0, NH_C + h)),
                  pl.BlockSpec((1, LANE), lambda b, h: (0, h)),
                  pl.BlockSpec((1, LANE), lambda b, h: (0, NH_C + h)),
                  pl.BlockSpec((1, LANE), lambda b, h: (0, 0)),
                  pl.BlockSpec((1, LANE), lambda b, h: (0, h))],
        out_specs=pl.BlockSpec((t_len, LANE), lambda b, h: (b, h)),
        out_shape=jax.ShapeDtypeStruct((bsz * t_len, D_C), F32),
        scratch_shapes=[pltpu.VMEM((t_len, LANE), F32), pltpu.VMEM((t_len, LANE), F32)],
        compiler_params=_params("parallel", "arbitrary"),
        name="mlstm",
    )(main, main, main, main, main, tail, conv_w, conv_w, conv_b, conv_b, gbias, norm_g)


def _merge_kernel(x_ref, ya_ref, yb_ref, yc_ref, g0_ref, g1_ref, g2_ref, p_ref,
                  wa_ref, wb_ref, wc_ref, wo_ref, wp_ref, wg_ref, lng_ref, lnb_ref, o_ref, ob_ref):
    merged = (jax.nn.sigmoid(g0_ref[...]) * _dot(ya_ref[...].astype(BF16), wa_ref[...])
              + jax.nn.sigmoid(g1_ref[...]) * _dot(yb_ref[...].astype(BF16), wb_ref[...])
              + jax.nn.sigmoid(g2_ref[...]) * _dot(yc_ref[...].astype(BF16), wc_ref[...]))
    r = ALPHA * x_ref[...] + _dot(merged.astype(BF16), wo_ref[...])
    r = r + jax.nn.sigmoid(_dot(r.astype(BF16), wg_ref[...])) * _dot(p_ref[...].astype(BF16), wp_ref[...])
    mu = jnp.mean(r, axis=-1, keepdims=True)
    var = jnp.mean(jnp.square(r - mu), axis=-1, keepdims=True)
    y = (r - mu) * lax.rsqrt(var + LN_EPS) * lng_ref[...] + lnb_ref[...]
    o_ref[...] = y
    ob_ref[...] = y.astype(BF16)


def _merge(x, ya, yb, yc, main, p, wa, wb, wc, wo, wp, wg, lng, lnb, tm):
    n = x.shape[0]
    row = lambda w: pl.BlockSpec((tm, w), lambda i: (i, 0))
    gate = lambda j: pl.BlockSpec((tm, D_MODEL), lambda i: (i, 5 + j))
    full = lambda a: pl.BlockSpec(a.shape, lambda i: (0,) * a.ndim)
    return pl.pallas_call(
        _merge_kernel,
        grid=(n // tm,),
        in_specs=[row(D_MODEL), row(D_A), row(D_B), row(D_C), gate(0), gate(1), gate(2), row(PLE_DIM),
                  full(wa), full(wb), full(wc), full(wo), full(wp), full(wg), full(lng), full(lnb)],
        out_specs=[row(D_MODEL), row(D_MODEL)],
        out_shape=[jax.ShapeDtypeStruct((n, D_MODEL), F32), jax.ShapeDtypeStruct((n, D_MODEL), BF16)],
        compiler_params=_params("parallel"),
        name="merge",
    )(x, ya, yb, yc, main, main, main, p, wa, wb, wc, wo, wp, wg, lng, lnb)


def _layer(x, xb, p, w_in, cmp_pos_k, cmp_pos_v, cmp_wk1, cmp_wk2, cmp_wv1, cmp_wv2,
           sg_ln_g, sg_ln_b, sg_w, sg_b, ml_conv_w, ml_conv_b, ml_b_i, ml_b_f, ml_norm_g,
           w_br_a, w_br_b, w_br_c, w_out, ple_w, ple_gate, ln_g, ln_b, bsz, t_len):
    w_main = jnp.concatenate([w_in[:, 0:512], w_in[:, 1304:4888], w_in[:, 4896:8992]], axis=1).astype(BF16)
    w_tail = jnp.concatenate([w_in[:, 512:1280], w_in[:, 1280:1304], jnp.zeros((D_MODEL, 8), F32),
                              w_in[:, 4888:4896], jnp.zeros((D_MODEL, 88), F32)], axis=1).astype(BF16)
    main = _matmul(xb, w_main, 2048, 512, "proj_main")
    tail = _matmul(xb, w_tail, 2048, TAIL_W, "proj_tail")

    rows16 = t_len // STRIDE_CMP
    zk = tail[:, 0:LANE].reshape(bsz, rows16, STRIDE_CMP * LANE)
    zv = tail[:, LANE:2 * LANE].reshape(bsz, rows16, STRIDE_CMP * LANE)
    wk1e, wk2e, pke = _compress_weights(cmp_wk1, cmp_wk2, cmp_pos_k)
    wv1e, wv2e, pve = _compress_weights(cmp_wv1, cmp_wv2, cmp_pos_v)
    kcmp, vcmp = _compress(zk, zv, pke, pve, wk1e, wk2e, wv1e, wv2e)
    ya = _nsa(main, tail, kcmp, vcmp, bsz, t_len)

    yb = _sgu(main, sg_ln_g[None, :], sg_ln_b[None, :], sg_w, sg_b.T, 512)

    gbias = jnp.zeros((1, LANE), F32).at[0, GI_LANE:GI_LANE + NH_C].set(ml_b_i)
    gbias = gbias.at[0, GF_LANE:GF_LANE + NH_C].set(ml_b_f)
    yc = _mlstm(main, tail, ml_conv_w, ml_conv_b[None, :], gbias, ml_norm_g[None, :], bsz, t_len)

    return _merge(x, ya, yb, yc, main, p, w_br_a.astype(BF16), w_br_b.astype(BF16), w_br_c.astype(BF16),
                  w_out.astype(BF16), ple_w.astype(BF16), ple_gate.astype(BF16),
                  ln_g[None, :], ln_b[None, :], 256)


def kernel(x, p, w_in, cmp_pos_k, cmp_pos_v, cmp_wk1, cmp_wk2, cmp_wv1, cmp_wv2, sg_ln_g, sg_ln_b, sg_w, sg_b,
           ml_conv_w, ml_conv_b, ml_b_i, ml_b_f, ml_norm_g, w_br_a, w_br_b, w_br_c, w_out, ple_w, ple_gate,
           ln_g, ln_b):
    bsz, t_len, d = x.shape
    assert d == D_MODEL and t_len == N_CMP_PAD * STRIDE_CMP and t_len // L_SEL == 32
    xf = x.reshape(bsz * t_len, d)
    xb = xf.astype(BF16)
    for i in range(w_in.shape[0]):
        xf, xb = _layer(xf, xb, p[i].reshape(bsz * t_len, PLE_DIM), w_in[i], cmp_pos_k[i], cmp_pos_v[i],
                        cmp_wk1[i], cmp_wk2[i], cmp_wv1[i], cmp_wv2[i], sg_ln_g[i], sg_ln_b[i], sg_w[i],
                        sg_b[i], ml_conv_w[i], ml_conv_b[i], ml_b_i[i], ml_b_f[i], ml_norm_g[i],
                        w_br_a[i], w_br_b[i], w_br_c[i], w_out[i], ple_w[i], ple_gate[i], ln_g[i], ln_b[i],
                        bsz, t_len)
    return xf.reshape(bsz, t_len, d)
```

```python
import jax
import jax.numpy as jnp
from jax import lax
from jax.experimental import pallas as pl
from jax.experimental.pallas import tpu as pltpu

F32 = jnp.float32
BF16 = jnp.bfloat16

D_MODEL = 1024
PLE_DIM = 256
D_A = 512
NH_A = 8
G_A = 2
HPG_A = 4
DK_A = 64
L_CMP = 32
STRIDE_CMP = 16
CMP_HID = 128
L_SEL = 64
N_SEL = 8
N_BLK = 32
WINDOW = 256
QB = 128
CK = 256
N_CMP_PAD = 128
BIG = 1e9
D_B = 512
CHUNK_B = 128
G_B = 4
D_C = 512
NH_C = 4
DH_C = 128
CONV_W = 4
CHUNK_C = 128
SEQ_BLK_C = 512
LN_EPS = 1e-5
DEPTH = 2
ALPHA = (2.0 * DEPTH) ** 0.25

LANE = 128
SUBLANE = 8
MAIN_W = 8192
K_W = 256
VT_W = 512
AUX_W = 384
GI_LANE = 32
GF_LANE = 36
VMEM_LIMIT = 52 * 1024 * 1024
NEG = -1e30
LOG2E = 1.4426950408889634


def _params(*sem):
    return pltpu.CompilerParams(dimension_semantics=sem, vmem_limit_bytes=VMEM_LIMIT)


def _nt_dot(a, b):
    return lax.dot_general(a, b, (((1,), (1,)), ((), ())), preferred_element_type=F32)


def _tn_dot(a, b):
    return lax.dot_general(a, b, (((0,), (0,)), ((), ())), preferred_element_type=F32)


def _dot(a, b):
    return jnp.dot(a, b, preferred_element_type=F32)


def _split3(a):
    a1 = a.astype(BF16)
    r1 = a - a1.astype(F32)
    a2 = r1.astype(BF16)
    a3 = (r1 - a2.astype(F32)).astype(BF16)
    return a1, a2, a3


def _log_sigmoid(x):
    return jnp.minimum(x, 0.0) - jnp.log1p(jnp.exp(-jnp.abs(x)))


def _mm_kernel(x_ref, w_ref, o_ref):
    o_ref[...] = _dot(x_ref[...], w_ref[...]).astype(o_ref.dtype)


def _matmul(x, w, tm, tn, name):
    m, k = x.shape
    n = w.shape[1]
    return pl.pallas_call(
        _mm_kernel,
        grid=(m // tm, n // tn),
        in_specs=[pl.BlockSpec((tm, k), lambda i, j: (i, 0)),
                  pl.BlockSpec((k, tn), lambda i, j: (0, j))],
        out_specs=pl.BlockSpec((tm, tn), lambda i, j: (i, j)),
        out_shape=jax.ShapeDtypeStruct((m, n), F32),
        compiler_params=_params("parallel", "arbitrary"),
        name=name,
    )(x, w)


def _tail_kernel(x_ref, wk_ref, wvt_ref, waux_ref, ok_ref, ovt_ref, oaux_ref):
    x = x_ref[...]
    ok_ref[...] = _dot(x, wk_ref[...]).astype(BF16)
    ovt_ref[...] = _nt_dot(wvt_ref[...], x).astype(BF16)
    oaux_ref[...] = _dot(x, waux_ref[...])


def _tail_matmul(x, wk, wvt, waux, tm):
    m, k = x.shape
    full = lambda a: pl.BlockSpec(a.shape, lambda i: (0,) * a.ndim)
    return pl.pallas_call(
        _tail_kernel,
        grid=(m // tm,),
        in_specs=[pl.BlockSpec((tm, k), lambda i: (i, 0)), full(wk), full(wvt), full(waux)],
        out_specs=[pl.BlockSpec((tm, K_W), lambda i: (i, 0)), pl.BlockSpec((VT_W, tm), lambda i: (0, i)),
                   pl.BlockSpec((tm, AUX_W), lambda i: (i, 0))],
        out_shape=[jax.ShapeDtypeStruct((m, K_W), BF16), jax.ShapeDtypeStruct((VT_W, m), BF16),
                   jax.ShapeDtypeStruct((m, AUX_W), F32)],
        compiler_params=_params("parallel"),
        name="proj_tail",
    )(x, wk, wvt, waux)


def _compress_kernel(zk_ref, zv_ref, pk_ref, pv_ref, wk1_ref, wk2_ref, wv1_ref, wv2t_ref, ko_ref, vot_ref):
    def hidden(z_ref, p_ref, w1_ref):
        z = z_ref[0]
        r0 = _dot((z + p_ref[0:1, :]).astype(BF16), w1_ref[0])
        r1 = _dot((z + p_ref[1:2, :]).astype(BF16), w1_ref[1])
        hid = r0 + pltpu.roll(r1, r1.shape[0] - 1, 0)
        return jax.nn.gelu(hid).astype(BF16)

    ko_ref[0] = _dot(hidden(zk_ref, pk_ref, wk1_ref), wk2_ref[...]).astype(BF16)
    vot_ref[0] = _nt_dot(wv2t_ref[...], hidden(zv_ref, pv_ref, wv1_ref)).astype(BF16)


def _compress(zk, zv, pk, pv, wk1, wk2, wv1, wv2t):
    b, n, w = zk.shape
    zspec = pl.BlockSpec((1, n, w), lambda i: (i, 0, 0))
    full = lambda a: pl.BlockSpec(a.shape, lambda i: (0,) * a.ndim)
    ospec = pl.BlockSpec((1, n, LANE), lambda i: (i, 0, 0))
    return pl.pallas_call(
        _compress_kernel,
        grid=(b,),
        in_specs=[zspec, zspec, full(pk), full(pv), full(wk1), full(wk2), full(wv1), full(wv2t)],
        out_specs=[ospec, ospec],
        out_shape=[jax.ShapeDtypeStruct((b, n, LANE), BF16)] * 2,
        compiler_params=_params("parallel"),
        name="nsa_compress",
    )(zk, zv, pk, pv, wk1, wk2, wv1, wv2t)


def _compress_weights(w1, w2, pos):
    eye = jnp.eye(G_A, dtype=F32)
    w1e = jnp.einsum('ldh,Gg->lGdgh', w1.reshape(L_CMP, DK_A, CMP_HID), eye)
    w1e = w1e.reshape(2, STRIDE_CMP * G_A * DK_A, G_A * CMP_HID).astype(BF16)
    w2e = jnp.einsum('hd,gG->ghGd', w2, eye).reshape(G_A * CMP_HID, G_A * DK_A).astype(BF16)
    pe = jnp.broadcast_to(pos[:, None, :], (L_CMP, G_A, DK_A)).reshape(2, STRIDE_CMP * G_A * DK_A)
    return w1e, w2e, pe


def _nsa_kernel(q_ref, z_ref, g_ref, kc_ref, vct_ref, k_ref, vt_ref, o_ref, s_sc, acc_sc, bw_sc, sel_sc):
    qb = pl.program_id(1)
    t0 = qb * QB
    cols = HPG_A * QB
    n_wchunk = WINDOW // LANE + 1
    n_wslot = n_wchunk + 1
    k_i = lax.broadcasted_iota(jnp.int32, (LANE, 1), 0)
    t_i = lax.broadcasted_iota(jnp.int32, (1, QB), 1)
    tk = (t_i - k_i).astype(F32)
    ones_row = jnp.where(lax.broadcasted_iota(jnp.int32, (LANE, CK), 0) == DK_A, 1.0, 0.0).astype(BF16)
    slopes = [[LOG2E * 2.0 ** -(g * HPG_A + h + 1) for h in range(HPG_A)] for g in range(G_A)]
    hs = [slice(h * QB, (h + 1) * QB) for h in range(HPG_A)]

    def biased(s, bias_of_head):
        return jnp.concatenate([s[:, hs[h]] + bias_of_head(h) for h in range(HPG_A)], axis=1)

    @pl.when(qb == 0)
    def _():
        for g in range(G_A):
            for h in range(HPG_A):
                for j in range(n_wchunk):
                    dist = tk + float((n_wchunk - 1 - j) * LANE)
                    ok = (dist >= 0.0) & (dist < WINDOW)
                    bw_sc[g, h * n_wslot + j] = jnp.where(ok, -slopes[g][h] * dist, NEG)
                bw_sc[g, h * n_wslot + n_wchunk] = jnp.full((LANE, QB), NEG, F32)

    gates_t = jnp.transpose(jax.nn.sigmoid(g_ref[...]))

    jb = lax.broadcasted_iota(jnp.int32, (N_BLK, QB), 0)
    cur = (t0 + lax.broadcasted_iota(jnp.int32, (N_BLK, QB), 1)) >> 6
    forced = (jb == 0) | (jb == cur) | (jb == cur - 1)
    allowed = jb <= cur
    ov_j = lax.broadcasted_iota(jnp.int32, (N_BLK, N_CMP_PAD), 0) * L_SEL
    ov_n = lax.broadcasted_iota(jnp.int32, (N_BLK, N_CMP_PAD), 1) * STRIDE_CMP
    ov_t = jnp.where((ov_n < ov_j + L_SEL) & (ov_n + L_CMP > ov_j), 1.0, 0.0).astype(BF16)

    d_cmp = (t0 - (L_CMP - 1)).astype(F32) + (t_i - STRIDE_CMP * k_i).astype(F32)
    valid_cmp = (d_cmp >= 0.0) & (k_i < N_CMP_PAD - 1)

    gls = [slice(g * DK_A, (g + 1) * DK_A) for g in range(G_A)]
    q4s, o_cmps = [], []
    for g in range(G_A):
        q4 = jnp.concatenate(
            [q_ref[:, (g * HPG_A + h) * DK_A:(g * HPG_A + h + 1) * DK_A] for h in range(HPG_A)], axis=0)
        q4 = (q4 * (LOG2E * DK_A ** -0.5)).astype(BF16)
        q4s.append(q4)

        s = biased(_nt_dot(kc_ref[0][:, gls[g]], q4),
                   lambda h: jnp.where(valid_cmp, -slopes[g][h] * d_cmp, -jnp.inf))
        mx = jnp.max(s, axis=0, keepdims=True)
        mx = jnp.where(mx > -jnp.inf, mx, 0.0)
        e = jnp.exp2(s - mx)
        p = e * (1.0 / jnp.maximum(jnp.sum(e, axis=0, keepdims=True), 1e-30))
        o_cmps.append(_dot(vct_ref[0][gls[g], :], p.astype(BF16)))

        psum = p[:, hs[0]] + p[:, hs[1]] + p[:, hs[2]] + p[:, hs[3]]
        p1, p2, p3 = _split3(psum)
        imp = _dot(ov_t, p1) + _dot(ov_t, p2) + _dot(ov_t, p3)
        score = jnp.where(allowed, jnp.where(forced, BIG, imp), -BIG)
        cnt = jnp.zeros((N_BLK, QB), F32)
        for j in range(N_BLK):
            sj = score[j:j + 1, :]
            beats = (sj > score) | ((sj == score) & (jb > j))
            cnt = cnt + jnp.where(beats, 1.0, 0.0)
        sel_sc[g] = jnp.where((cnt < N_SEL) & allowed, 1.0, 0.0)

    def score_chunk(c, m_runs):
        k0 = pl.multiple_of(c * CK, CK)
        out = []
        for g in range(G_A):
            sc = _nt_dot(k_ref[pl.ds(k0, CK), gls[g]], q4s[g])
            halves = []
            for u in range(CK // LANE):
                dist = tk + (t0 - k0 - u * LANE).astype(F32)
                blk0 = (CK // L_SEL) * c + (LANE // L_SEL) * u
                chosen = jnp.concatenate(
                    [jnp.broadcast_to(sel_sc[g, pl.ds(blk0 + i, 1), :], (L_SEL, QB))
                     for i in range(LANE // L_SEL)], axis=0)
                ok = (chosen > 0.5) & (dist >= 0.0)
                halves.append(biased(sc[u * LANE:(u + 1) * LANE],
                                     lambda h: jnp.where(ok, -slopes[g][h] * dist, NEG)))
            sc = jnp.concatenate(halves, axis=0)
            s_sc[g, c] = sc
            out.append(jnp.maximum(m_runs[g], jnp.max(sc.reshape(CK // SUBLANE, SUBLANE, cols), axis=0)))
        return tuple(out)

    n_chunk = qb // (CK // QB) + 1
    m_runs = lax.fori_loop(0, n_chunk, score_chunk, (jnp.full((SUBLANE, cols), NEG, F32),) * G_A)
    m_rows = [jnp.max(m_runs[g], axis=0, keepdims=True) for g in range(G_A)]
    acc_sc[...] = jnp.zeros(acc_sc.shape, F32)

    def value_chunk(c, carry):
        k0 = pl.multiple_of(c * CK, CK)
        for g in range(G_A):
            pc = jnp.exp2(s_sc[g, c] - m_rows[g]).astype(BF16)
            acc_sc[g] += _dot(vt_ref[g * LANE:(g + 1) * LANE, pl.ds(k0, CK)] + ones_row, pc)
        return carry

    lax.fori_loop(0, n_chunk, value_chunk, 0)

    pieces = []
    for g in range(G_A):
        o_sel = acc_sc[g]

        s_w, v_w = [], []
        for j in range(n_wchunk):
            c = qb - (n_wchunk - 1) + j
            k0 = pl.multiple_of(jnp.maximum(c, 0) * LANE, LANE)
            slot = jnp.where(c >= 0, j, n_wchunk)
            sj = _nt_dot(k_ref[pl.ds(k0, LANE), LANE + g * DK_A:LANE + (g + 1) * DK_A], q4s[g])
            s_w.append(biased(sj, lambda h: bw_sc[g, h * n_wslot + slot]))
            v_w.append(vt_ref[G_A * LANE + g * LANE:G_A * LANE + (g + 1) * LANE, pl.ds(k0, LANE)]
                       + ones_row[:, :LANE])
        mx = jnp.max(jnp.maximum(jnp.maximum(s_w[0], s_w[1]), s_w[2]), axis=0, keepdims=True)
        o_win = (_dot(v_w[0], jnp.exp2(s_w[0] - mx).astype(BF16)) + _dot(v_w[1], jnp.exp2(s_w[1] - mx).astype(BF16))
                 + _dot(v_w[2], jnp.exp2(s_w[2] - mx).astype(BF16)))

        for h in range(HPG_A):
            col = g * HPG_A + h
            os_h, ow_h = o_sel[:, hs[h]], o_win[:, hs[h]]
            w_sel = gates_t[NH_A + col:NH_A + col + 1, :] / jnp.maximum(os_h[DK_A:DK_A + 1, :], 1e-30)
            w_win = gates_t[2 * NH_A + col:2 * NH_A + col + 1, :] / jnp.maximum(ow_h[DK_A:DK_A + 1, :], 1e-30)
            pieces.append(gates_t[col:col + 1, :] * o_cmps[g][:, hs[h]] + w_sel * os_h[:DK_A] + w_win * ow_h[:DK_A])

    out_t = jnp.concatenate(pieces, axis=0)
    out = jnp.concatenate([jnp.transpose(out_t[i * LANE:(i + 1) * LANE]) for i in range(D_A // LANE)], axis=1)
    o_ref[...] = out * jax.nn.silu(z_ref[...])


def _nsa(main, kk, vt, aux, kcmp, vcmp_t, bsz, t_len):
    nqb = t_len // QB
    cols = HPG_A * QB
    n_wslot = WINDOW // LANE + 2
    cspec = pl.BlockSpec((1, N_CMP_PAD, LANE), lambda b, i: (b, 0, 0))
    return pl.pallas_call(
        _nsa_kernel,
        grid=(bsz, nqb),
        in_specs=[pl.BlockSpec((QB, D_A), lambda b, i: (b * nqb + i, 0)),
                  pl.BlockSpec((QB, D_A), lambda b, i: (b * nqb + i, 1)),
                  pl.BlockSpec((QB, LANE), lambda b, i: (b * nqb + i, 2)),
                  cspec, cspec,
                  pl.BlockSpec((t_len, K_W), lambda b, i: (b, 0)),
                  pl.BlockSpec((VT_W, t_len), lambda b, i: (0, b))],
        out_specs=pl.BlockSpec((QB, D_A), lambda b, i: (b * nqb + i, 0)),
        out_shape=jax.ShapeDtypeStruct((bsz * t_len, D_A), F32),
        scratch_shapes=[pltpu.VMEM((G_A, t_len // CK, CK, cols), F32), pltpu.VMEM((G_A, LANE, cols), F32),
                        pltpu.VMEM((G_A, HPG_A * n_wslot, LANE, QB), F32), pltpu.VMEM((G_A, N_BLK, QB), F32)],
        compiler_params=_params("parallel", "arbitrary"),
        name="nsa_attention",
    )(main, main, aux, kcmp, vcmp_t, kk, vt)


def _sgu_kernel(u_ref, v_ref, z_ref, lng_ref, lnb_ref, w_ref, b_ref, o_ref):
    u = jax.nn.gelu(u_ref[...])
    v = jax.nn.gelu(v_ref[...])
    mu = jnp.mean(v, axis=-1, keepdims=True)
    var = jnp.mean(jnp.square(v - mu), axis=-1, keepdims=True)
    vn = ((v - mu) * lax.rsqrt(var + LN_EPS) * lng_ref[...] + lnb_ref[...]).astype(BF16)
    gate = u * jax.nn.silu(z_ref[...])
    ti = lax.broadcasted_iota(jnp.int32, (CHUNK_B, CHUNK_B), 0)
    si = lax.broadcasted_iota(jnp.int32, (CHUNK_B, CHUNK_B), 1)
    n_chunks = u_ref.shape[0] // CHUNK_B
    for g in range(G_B):
        w = jnp.where(si <= ti, w_ref[g], 0.0).astype(BF16)
        bias = b_ref[:, g:g + 1]
        ls = slice(g * LANE, (g + 1) * LANE)
        for c in range(n_chunks):
            rs = slice(c * CHUNK_B, (c + 1) * CHUNK_B)
            o_ref[rs, ls] = gate[rs, ls] * (_dot(w, vn[rs, ls]) + bias)


def _sgu(main, lng, lnb, w, b_t, tm):
    n = main.shape[0]
    blk = lambda col: pl.BlockSpec((tm, D_B), lambda i: (i, col))
    full = lambda a: pl.BlockSpec(a.shape, lambda i: (0,) * a.ndim)
    return pl.pallas_call(
        _sgu_kernel,
        grid=(n // tm,),
        in_specs=[blk(2), blk(3), blk(4), full(lng), full(lnb), full(w), full(b_t)],
        out_specs=pl.BlockSpec((tm, D_B), lambda i: (i, 0)),
        out_shape=jax.ShapeDtypeStruct((n, D_B), F32),
        compiler_params=_params("parallel"),
        name="spatial_gating",
    )(main, main, main, lng, lnb, w, b_t)


def _mlstm_kernel(q_ref, k_ref, v_ref, o_ref, z_ref, g_ref, cw_ref, cb_ref, gb_ref, ng_ref, y_ref,
                  qc_sc, kc_sc, halo_sc, st_sc, m_sc):
    sb = pl.program_id(1)
    blk = q_ref.shape[0]

    @pl.when(sb == 0)
    def _():
        halo_sc[...] = jnp.zeros(halo_sc.shape, F32)
        st_sc[...] = jnp.zeros(st_sc.shape, F32)
        m_sc[...] = jnp.zeros(m_sc.shape, F32)

    r8 = lax.broadcasted_iota(jnp.int32, (SUBLANE, 1), 0)

    def conv_silu(x_ref, which, dst_ref):
        x = x_ref[...]
        prev = halo_sc[which]
        w = cw_ref[:, which * D_C:(which + 1) * D_C]
        acc = x * w[CONV_W - 1:CONV_W, :] + cb_ref[:, which * D_C:(which + 1) * D_C]
        for j in range(1, CONV_W):
            rolled = pltpu.roll(x, j, 0)
            head = jnp.where(r8 < j, pltpu.roll(prev, j, 0), rolled[0:SUBLANE])
            shifted = jnp.concatenate([head, rolled[SUBLANE:]], axis=0)
            acc = acc + shifted * w[CONV_W - 1 - j:CONV_W - j, :]
        halo_sc[which] = x[blk - SUBLANE:blk]
        dst_ref[...] = jax.nn.silu(acc)

    conv_silu(q_ref, 0, qc_sc)
    conv_silu(k_ref, 1, kc_sc)

    ti = lax.broadcasted_iota(jnp.int32, (CHUNK_C, CHUNK_C), 0)
    si = lax.broadcasted_iota(jnp.int32, (CHUNK_C, CHUNK_C), 1)
    causal = si <= ti
    tri_l = jnp.where(causal, 1.0, 0.0).astype(BF16)
    ones_blk = jnp.where(si == 0, 1.0, 0.0).astype(BF16)
    ng = ng_ref[...]
    gbias = gb_ref[...]

    state = [st_sc[h] for h in range(NH_C)]
    m_state = [m_sc[h][:, 0:1] for h in range(NH_C)]

    for c in range(blk // CHUNK_C):
        rs = slice(c * CHUNK_C, (c + 1) * CHUNK_C)
        li = g_ref[rs, :] + gbias
        lf = _log_sigmoid(li)
        f1, f2, f3 = _split3(lf)
        bc = _dot(tri_l, f1) + _dot(tri_l, f2) + _dot(tri_l, f3)
        li_t = jnp.transpose(li)
        bc_t = jnp.transpose(bc)
        for h in range(NH_C):
            ls = slice(h * DH_C, (h + 1) * DH_C)
            q = qc_sc[rs, ls]
            k = kc_sc[rs, ls] * DH_C ** -0.5
            vext = jnp.concatenate([v_ref[rs, ls].astype(BF16), ones_blk], axis=1)
            li_col = li[:, GI_LANE + h:GI_LANE + h + 1]
            bc_col = bc[:, GF_LANE + h:GF_LANE + h + 1]
            b_last = bc[CHUNK_C - 1:CHUNK_C, GF_LANE + h:GF_LANE + h + 1]
            g_row = li_t[GI_LANE + h:GI_LANE + h + 1, :] - bc_t[GF_LANE + h:GF_LANE + h + 1, :]
            m_prev = m_state[h]
            m_loc = b_last + jnp.max(g_row, axis=1, keepdims=True)
            e_end = jnp.exp(b_last + (li_col - bc_col) - m_loc)

            mg = jnp.maximum(m_prev, jnp.max(jnp.where(causal, g_row, -jnp.inf), axis=1, keepdims=True))
            dec = jnp.exp(jnp.where(causal, g_row - mg, -jnp.inf))
            s = _nt_dot(q.astype(BF16), k.astype(BF16)) * dec
            e_int = jnp.exp(m_prev - mg)
            lhs = jnp.concatenate([s, e_int * q], axis=1).astype(BF16)
            rhs = jnp.concatenate([vext, state[h].astype(BF16)], axis=0)
            both = _dot(lhs, rhs)
            num = both[:, :DH_C]
            den = both[:, DH_C:DH_C + 1]
            hval = num / jnp.maximum(jnp.abs(den), jnp.exp(-(bc_col + mg)))
            mu = jnp.mean(hval, axis=-1, keepdims=True)
            var = jnp.mean(jnp.square(hval - mu), axis=-1, keepdims=True)
            hn = (hval - mu) * lax.rsqrt(var + LN_EPS) * ng[:, ls]
            y_ref[rs, ls] = hn * jax.nn.sigmoid(o_ref[rs, ls]) * jax.nn.silu(z_ref[rs, ls])

            loc = _tn_dot((k * e_end).astype(BF16), vext)
            m_new = jnp.maximum(b_last + m_prev, m_loc)
            state[h] = jnp.exp(b_last + m_prev - m_new) * state[h] + jnp.exp(m_loc - m_new) * loc
            m_state[h] = m_new

    for h in range(NH_C):
        st_sc[h] = state[h]
        m_sc[h] = jnp.broadcast_to(m_state[h], (1, LANE))


def _mlstm(main, aux, conv_w, conv_b, gbias, norm_g, bsz, t_len):
    nsb = t_len // SEQ_BLK_C
    blk = lambda col: pl.BlockSpec((SEQ_BLK_C, D_C), lambda b, s: (b * nsb + s, col))
    full = lambda a: pl.BlockSpec(a.shape, lambda b, s: (0,) * a.ndim)
    return pl.pallas_call(
        _mlstm_kernel,
        grid=(bsz, nsb),
        in_specs=[blk(5), blk(6), blk(7), blk(8), blk(9),
                  pl.BlockSpec((SEQ_BLK_C, LANE), lambda b, s: (b * nsb + s, 2)),
                  full(conv_w), full(conv_b), full(gbias), full(norm_g)],
        out_specs=pl.BlockSpec((SEQ_BLK_C, D_C), lambda b, s: (b * nsb + s, 0)),
        out_shape=jax.ShapeDtypeStruct((bsz * t_len, D_C), F32),
        scratch_shapes=[pltpu.VMEM((SEQ_BLK_C, D_C), F32), pltpu.VMEM((SEQ_BLK_C, D_C), F32),
                        pltpu.VMEM((2, SUBLANE, D_C), F32), pltpu.VMEM((NH_C, DH_C, 2 * DH_C), F32),
                        pltpu.VMEM((NH_C, 1, LANE), F32)],
        compiler_params=_params("parallel", "arbitrary"),
        name="mlstm",
    )(main, main, main, main, main, aux, conv_w, conv_b, gbias, norm_g)


def _merge_kernel(x_ref, ya_ref, yb_ref, yc_ref, g0_ref, g1_ref, g2_ref, p_ref,
                  wa_ref, wb_ref, wc_ref, wo_ref, wp_ref, wg_ref, lng_ref, lnb_ref, o_ref, ob_ref):
    merged = (jax.nn.sigmoid(g0_ref[...]) * _dot(ya_ref[...].astype(BF16), wa_ref[...])
              + jax.nn.sigmoid(g1_ref[...]) * _dot(yb_ref[...].astype(BF16), wb_ref[...])
              + jax.nn.sigmoid(g2_ref[...]) * _dot(yc_ref[...].astype(BF16), wc_ref[...]))
    r = ALPHA * x_ref[...] + _dot(merged.astype(BF16), wo_ref[...])
    r = r + jax.nn.sigmoid(_dot(r.astype(BF16), wg_ref[...])) * _dot(p_ref[...].astype(BF16), wp_ref[...])
    mu = jnp.mean(r, axis=-1, keepdims=True)
    var = jnp.mean(jnp.square(r - mu), axis=-1, keepdims=True)
    y = (r - mu) * lax.rsqrt(var + LN_EPS) * lng_ref[...] + lnb_ref[...]
    o_ref[...] = y
    ob_ref[...] = y.astype(BF16)


def _merge(x, ya, yb, yc, main, p, wa, wb, wc, wo, wp, wg, lng, lnb, tm):
    n = x.shape[0]
    row = lambda w: pl.BlockSpec((tm, w), lambda i: (i, 0))
    gate = lambda j: pl.BlockSpec((tm, D_MODEL), lambda i: (i, 5 + j))
    full = lambda a: pl.BlockSpec(a.shape, lambda i: (0,) * a.ndim)
    return pl.pallas_call(
        _merge_kernel,
        grid=(n // tm,),
        in_specs=[row(D_MODEL), row(D_A), row(D_B), row(D_C), gate(0), gate(1), gate(2), row(PLE_DIM),
                  full(wa), full(wb), full(wc), full(wo), full(wp), full(wg), full(lng), full(lnb)],
        out_specs=[row(D_MODEL), row(D_MODEL)],
        out_shape=[jax.ShapeDtypeStruct((n, D_MODEL), F32), jax.ShapeDtypeStruct((n, D_MODEL), BF16)],
        compiler_params=_params("parallel"),
        name="merge",
    )(x, ya, yb, yc, main, main, main, p, wa, wb, wc, wo, wp, wg, lng, lnb)


def _pad_groups(w):
    d = w.shape[0]
    return jnp.pad(w.reshape(d, G_A, DK_A), ((0, 0), (0, 0), (0, LANE - DK_A))).reshape(d, G_A * LANE)


def _layer(x, xb, p, w_in, cmp_pos_k, cmp_pos_v, cmp_wk1, cmp_wk2, cmp_wv1, cmp_wv2,
           sg_ln_g, sg_ln_b, sg_w, sg_b, ml_conv_w, ml_conv_b, ml_b_i, ml_b_f, ml_norm_g,
           w_br_a, w_br_b, w_br_c, w_out, ple_w, ple_gate, ln_g, ln_b, bsz, t_len):
    w_main = jnp.concatenate([w_in[:, 0:512], w_in[:, 1304:4888], w_in[:, 4896:8992]], axis=1).astype(BF16)
    w_k = jnp.concatenate([w_in[:, 768:896], w_in[:, 1024:1152]], axis=1).astype(BF16)
    w_vt = jnp.concatenate([_pad_groups(w_in[:, 896:1024]), _pad_groups(w_in[:, 1152:1280])], axis=1).T.astype(BF16)
    w_aux = jnp.concatenate([w_in[:, 512:768], w_in[:, 1280:1304], jnp.zeros((D_MODEL, 8), F32),
                             w_in[:, 4888:4896], jnp.zeros((D_MODEL, 88), F32)], axis=1).astype(BF16)
    main = _matmul(xb, w_main, 2048, 512, "proj_main")
    kk, vt, aux = _tail_matmul(xb, w_k, w_vt, w_aux, 2048)

    rows16 = t_len // STRIDE_CMP
    zk = aux[:, 0:LANE].reshape(bsz, rows16, STRIDE_CMP * LANE)
    zv = aux[:, LANE:2 * LANE].reshape(bsz, rows16, STRIDE_CMP * LANE)
    wk1e, wk2e, pke = _compress_weights(cmp_wk1, cmp_wk2, cmp_pos_k)
    wv1e, wv2e, pve = _compress_weights(cmp_wv1, cmp_wv2, cmp_pos_v)
    kcmp, vcmp_t = _compress(zk, zv, pke, pve, wk1e, wk2e, wv1e, wv2e.T)
    ya = _nsa(main, kk, vt, aux, kcmp, vcmp_t, bsz, t_len)

    yb = _sgu(main, sg_ln_g[None, :], sg_ln_b[None, :], sg_w, sg_b.T, 512)

    gbias = jnp.zeros((1, LANE), F32).at[0, GI_LANE:GI_LANE + NH_C].set(ml_b_i)
    gbias = gbias.at[0, GF_LANE:GF_LANE + NH_C].set(ml_b_f)
    yc = _mlstm(main, aux, ml_conv_w, ml_conv_b[None, :], gbias, ml_norm_g[None, :], bsz, t_len)

    return _merge(x, ya, yb, yc, main, p, w_br_a.astype(BF16), w_br_b.astype(BF16), w_br_c.astype(BF16),
                  w_out.astype(BF16), ple_w.astype(BF16), ple_gate.astype(BF16),
                  ln_g[None, :], ln_b[None, :], 256)


def kernel(x, p, w_in, cmp_pos_k, cmp_pos_v, cmp_wk1, cmp_wk2, cmp_wv1, cmp_wv2, sg_ln_g, sg_ln_b, sg_w, sg_b,
           ml_conv_w, ml_conv_b, ml_b_i, ml_b_f, ml_norm_g, w_br_a, w_br_b, w_br_c, w_out, ple_w, ple_gate,
           ln_g, ln_b):
    bsz, t_len, d = x.shape
    assert d == D_MODEL and t_len == N_CMP_PAD * STRIDE_CMP and t_len // L_SEL == N_BLK
    assert t_len % SEQ_BLK_C == 0 and t_len % CK == 0
    xf = x.reshape(bsz * t_len, d)
    xb = xf.astype(BF16)
    for i in range(w_in.shape[0]):
        xf, xb = _layer(xf, xb, p[i].reshape(bsz * t_len, PLE_DIM), w_in[i], cmp_pos_k[i], cmp_pos_v[i],
                        cmp_wk1[i], cmp_wk2[i], cmp_wv1[i], cmp_wv2[i], sg_ln_g[i], sg_ln_b[i], sg_w[i],
                        sg_b[i], ml_conv_w[i], ml_conv_b[i], ml_b_i[i], ml_b_f[i], ml_norm_g[i],
                        w_br_a[i], w_br_b[i], w_br_c[i], w_out[i], ple_w[i], ple_gate[i], ln_g[i], ln_b[i],
                        bsz, t_len)
    return xf.reshape(bsz, t_len, d)
```

```python
import jax
import jax.numpy as jnp
from jax import lax
from jax.experimental import pallas as pl
from jax.experimental.pallas import tpu as pltpu

F32 = jnp.float32
BF16 = jnp.bfloat16

D_MODEL = 1024
PLE_DIM = 256
D_A = 512
NH_A = 8
G_A = 2
HPG_A = 4
DK_A = 64
L_CMP = 32
STRIDE_CMP = 16
CMP_HID = 128
L_SEL = 64
N_SEL = 8
N_BLK = 32
WINDOW = 256
QB = 256
CK = 256
N_CMP_PAD = 128
BIG = 1e9
D_B = 512
CHUNK_B = 128
G_B = 4
D_C = 512
NH_C = 4
DH_C = 128
CONV_W = 4
CHUNK_C = 128
SEQ_BLK_C = 512
LN_EPS = 1e-5
DEPTH = 2
ALPHA = (2.0 * DEPTH) ** 0.25

LANE = 128
SUBLANE = 8
MAIN_W = 8192
K_W = 256
VT_W = 512
AUX_W = 384
GI_LANE = 32
GF_LANE = 36
VMEM_LIMIT = 52 * 1024 * 1024
NEG = -1e30
LOG2E = 1.4426950408889634


def _params(*sem):
    return pltpu.CompilerParams(dimension_semantics=sem, vmem_limit_bytes=VMEM_LIMIT)


def _nt_dot(a, b):
    return lax.dot_general(a, b, (((1,), (1,)), ((), ())), preferred_element_type=F32)


def _tn_dot(a, b):
    return lax.dot_general(a, b, (((0,), (0,)), ((), ())), preferred_element_type=F32)


def _dot(a, b):
    return jnp.dot(a, b, preferred_element_type=F32)


def _split3(a):
    a1 = a.astype(BF16)
    r1 = a - a1.astype(F32)
    a2 = r1.astype(BF16)
    a3 = (r1 - a2.astype(F32)).astype(BF16)
    return a1, a2, a3


def _log_sigmoid(x):
    return jnp.minimum(x, 0.0) - jnp.log1p(jnp.exp(-jnp.abs(x)))


def _mm_kernel(x_ref, w_ref, o_ref):
    o_ref[...] = _dot(x_ref[...], w_ref[...]).astype(o_ref.dtype)


def _matmul(x, w, tm, tn, name):
    m, k = x.shape
    n = w.shape[1]
    return pl.pallas_call(
        _mm_kernel,
        grid=(m // tm, n // tn),
        in_specs=[pl.BlockSpec((tm, k), lambda i, j: (i, 0)),
                  pl.BlockSpec((k, tn), lambda i, j: (0, j))],
        out_specs=pl.BlockSpec((tm, tn), lambda i, j: (i, j)),
        out_shape=jax.ShapeDtypeStruct((m, n), F32),
        compiler_params=_params("parallel", "arbitrary"),
        name=name,
    )(x, w)


def _tail_kernel(x_ref, wk_ref, wvt_ref, waux_ref, ok_ref, ovt_ref, oaux_ref):
    x = x_ref[...]
    ok_ref[...] = _dot(x, wk_ref[...]).astype(BF16)
    ovt_ref[...] = _nt_dot(wvt_ref[...], x).astype(BF16)
    oaux_ref[...] = _dot(x, waux_ref[...])


def _tail_matmul(x, wk, wvt, waux, tm):
    m, k = x.shape
    full = lambda a: pl.BlockSpec(a.shape, lambda i: (0,) * a.ndim)
    return pl.pallas_call(
        _tail_kernel,
        grid=(m // tm,),
        in_specs=[pl.BlockSpec((tm, k), lambda i: (i, 0)), full(wk), full(wvt), full(waux)],
        out_specs=[pl.BlockSpec((tm, K_W), lambda i: (i, 0)), pl.BlockSpec((VT_W, tm), lambda i: (0, i)),
                   pl.BlockSpec((tm, AUX_W), lambda i: (i, 0))],
        out_shape=[jax.ShapeDtypeStruct((m, K_W), BF16), jax.ShapeDtypeStruct((VT_W, m), BF16),
                   jax.ShapeDtypeStruct((m, AUX_W), F32)],
        compiler_params=_params("parallel"),
        name="proj_tail",
    )(x, wk, wvt, waux)


def _compress_kernel(zk_ref, zv_ref, pk_ref, pv_ref, wk1_ref, wk2_ref, wv1_ref, wv2t_ref, ko_ref, vot_ref):
    def hidden(z_ref, p_ref, w1_ref):
        z = z_ref[0]
        r0 = _dot((z + p_ref[0:1, :]).astype(BF16), w1_ref[0])
        r1 = _dot((z + p_ref[1:2, :]).astype(BF16), w1_ref[1])
        hid = r0 + pltpu.roll(r1, r1.shape[0] - 1, 0)
        return jax.nn.gelu(hid).astype(BF16)

    ko_ref[0] = _dot(hidden(zk_ref, pk_ref, wk1_ref), wk2_ref[...]).astype(BF16)
    vot_ref[0] = _nt_dot(wv2t_ref[...], hidden(zv_ref, pv_ref, wv1_ref)).astype(BF16)


def _compress(zk, zv, pk, pv, wk1, wk2, wv1, wv2t):
    b, n, w = zk.shape
    zspec = pl.BlockSpec((1, n, w), lambda i: (i, 0, 0))
    full = lambda a: pl.BlockSpec(a.shape, lambda i: (0,) * a.ndim)
    ospec = pl.BlockSpec((1, n, LANE), lambda i: (i, 0, 0))
    return pl.pallas_call(
        _compress_kernel,
        grid=(b,),
        in_specs=[zspec, zspec, full(pk), full(pv), full(wk1), full(wk2), full(wv1), full(wv2t)],
        out_specs=[ospec, ospec],
        out_shape=[jax.ShapeDtypeStruct((b, n, LANE), BF16)] * 2,
        compiler_params=_params("parallel"),
        name="nsa_compress",
    )(zk, zv, pk, pv, wk1, wk2, wv1, wv2t)


def _compress_weights(w1, w2, pos):
    eye = jnp.eye(G_A, dtype=F32)
    w1e = jnp.einsum('ldh,Gg->lGdgh', w1.reshape(L_CMP, DK_A, CMP_HID), eye)
    w1e = w1e.reshape(2, STRIDE_CMP * G_A * DK_A, G_A * CMP_HID).astype(BF16)
    w2e = jnp.einsum('hd,gG->ghGd', w2, eye).reshape(G_A * CMP_HID, G_A * DK_A).astype(BF16)
    pe = jnp.broadcast_to(pos[:, None, :], (L_CMP, G_A, DK_A)).reshape(2, STRIDE_CMP * G_A * DK_A)
    return w1e, w2e, pe


def _nsa_kernel(q_ref, z_ref, g_ref, kc_ref, vct_ref, k_ref, vt_ref, o_ref, s_sc, acc_sc, ow_sc, bw_sc, sel_sc):
    qb = pl.program_id(1)
    t0 = qb * QB
    cols = HPG_A * QB
    n_qtile = QB // LANE
    n_wchunk = (WINDOW + QB) // LANE
    n_wslot = n_wchunk + 1
    k_i = lax.broadcasted_iota(jnp.int32, (LANE, 1), 0)
    t_i = lax.broadcasted_iota(jnp.int32, (1, QB), 1)
    tk = (t_i - k_i).astype(F32)
    ones_row = jnp.where(lax.broadcasted_iota(jnp.int32, (LANE, CK), 0) == DK_A, 1.0, 0.0).astype(BF16)
    slopes = [[LOG2E * 2.0 ** -(g * HPG_A + h + 1) for h in range(HPG_A)] for g in range(G_A)]
    hs = [slice(h * QB, (h + 1) * QB) for h in range(HPG_A)]

    def biased(s, bias_of_head):
        return jnp.concatenate([s[:, hs[h]] + bias_of_head(h) for h in range(HPG_A)], axis=1)

    @pl.when(qb == 0)
    def _():
        for g in range(G_A):
            for h in range(HPG_A):
                for j in range(n_wchunk):
                    dist = tk + float(WINDOW - j * LANE)
                    ok = (dist >= 0.0) & (dist < WINDOW)
                    bw_sc[g, h * n_wslot + j] = jnp.where(ok, -slopes[g][h] * dist, NEG)
                bw_sc[g, h * n_wslot + n_wchunk] = jnp.full((LANE, QB), NEG, F32)

    gsig = jax.nn.sigmoid(g_ref[...])
    gates_t = jnp.concatenate([jnp.transpose(gsig[r * LANE:(r + 1) * LANE]) for r in range(n_qtile)], axis=1)

    jb = lax.broadcasted_iota(jnp.int32, (N_BLK, QB), 0)
    cur = (t0 + lax.broadcasted_iota(jnp.int32, (N_BLK, QB), 1)) >> 6
    forced = (jb == 0) | (jb == cur) | (jb == cur - 1)
    allowed = jb <= cur
    ov_j = lax.broadcasted_iota(jnp.int32, (N_BLK, N_CMP_PAD), 0) * L_SEL
    ov_n = lax.broadcasted_iota(jnp.int32, (N_BLK, N_CMP_PAD), 1) * STRIDE_CMP
    ov_t = jnp.where((ov_n < ov_j + L_SEL) & (ov_n + L_CMP > ov_j), 1.0, 0.0).astype(BF16)
    jb8 = lax.broadcasted_iota(jnp.int32, (SUBLANE, QB), 0)

    d_cmp = (t0 - (L_CMP - 1)).astype(F32) + (t_i - STRIDE_CMP * k_i).astype(F32)
    valid_cmp = (d_cmp >= 0.0) & (k_i < N_CMP_PAD - 1)

    gls = [slice(g * DK_A, (g + 1) * DK_A) for g in range(G_A)]
    q4s, o_cmps = [], []
    for g in range(G_A):
        q4 = jnp.concatenate(
            [q_ref[:, (g * HPG_A + h) * DK_A:(g * HPG_A + h + 1) * DK_A] for h in range(HPG_A)], axis=0)
        q4 = (q4 * (LOG2E * DK_A ** -0.5)).astype(BF16)
        q4s.append(q4)

        s = biased(_nt_dot(kc_ref[0][:, gls[g]], q4),
                   lambda h: jnp.where(valid_cmp, -slopes[g][h] * d_cmp, -jnp.inf))
        mx = jnp.max(s, axis=0, keepdims=True)
        mx = jnp.where(mx > -jnp.inf, mx, 0.0)
        e = jnp.exp2(s - mx)
        p = e * (1.0 / jnp.maximum(jnp.sum(e, axis=0, keepdims=True), 1e-30))
        o_cmps.append(_dot(vct_ref[0][gls[g], :], p.astype(BF16)))

        psum = p[:, hs[0]] + p[:, hs[1]] + p[:, hs[2]] + p[:, hs[3]]
        p1, p2, p3 = _split3(psum)
        imp = _dot(ov_t, p1) + _dot(ov_t, p2) + _dot(ov_t, p3)
        score = jnp.where(allowed, jnp.where(forced, BIG, imp), -BIG)
        tiles = [score[r * SUBLANE:(r + 1) * SUBLANE] for r in range(N_BLK // SUBLANE)]
        cnt = [jnp.zeros((SUBLANE, QB), F32) for _ in tiles]
        for j in range(N_BLK):
            sj = score[j:j + 1, :]
            for r, tile in enumerate(tiles):
                lo = r * SUBLANE
                if j >= lo + SUBLANE:
                    beats = sj > tile
                elif j < lo:
                    beats = sj >= tile
                else:
                    beats = (sj > tile) | ((sj >= tile) & (jb8 > j - lo))
                cnt[r] = cnt[r] + jnp.where(beats, 1.0, 0.0)
        sel_sc[g] = jnp.where((jnp.concatenate(cnt, axis=0) < N_SEL) & allowed, 1.0, 0.0)

        s_w, v_w = [], []
        for j in range(n_wchunk):
            c = qb * n_qtile - WINDOW // LANE + j
            k0 = pl.multiple_of(jnp.maximum(c, 0) * LANE, LANE)
            slot = jnp.where(c >= 0, j, n_wchunk)
            sj = _nt_dot(k_ref[pl.ds(k0, LANE), LANE + g * DK_A:LANE + (g + 1) * DK_A], q4)
            s_w.append(biased(sj, lambda h: bw_sc[g, h * n_wslot + slot]))
            v_w.append(vt_ref[G_A * LANE + g * LANE:G_A * LANE + (g + 1) * LANE, pl.ds(k0, LANE)]
                       + ones_row[:, :LANE])
        mx = s_w[0]
        for j in range(1, n_wchunk):
            mx = jnp.maximum(mx, s_w[j])
        mx = jnp.max(mx, axis=0, keepdims=True)
        o_win = _dot(v_w[0], jnp.exp2((s_w[0] - mx).astype(BF16)))
        for j in range(1, n_wchunk):
            o_win = o_win + _dot(v_w[j], jnp.exp2((s_w[j] - mx).astype(BF16)))
        ow_sc[g] = o_win

    def score_chunk(c, m_runs):
        k0 = pl.multiple_of(c * CK, CK)
        out = []
        for g in range(G_A):
            sc = _nt_dot(k_ref[pl.ds(k0, CK), gls[g]], q4s[g])
            halves = []
            for u in range(CK // LANE):
                dist = tk + (t0 - k0 - u * LANE).astype(F32)
                blk0 = (CK // L_SEL) * c + (LANE // L_SEL) * u
                chosen = jnp.concatenate(
                    [jnp.broadcast_to(sel_sc[g, pl.ds(blk0 + i, 1), :], (L_SEL, QB))
                     for i in range(LANE // L_SEL)], axis=0)
                ok = (chosen > 0.5) & (dist >= 0.0)
                halves.append(biased(sc[u * LANE:(u + 1) * LANE],
                                     lambda h: jnp.where(ok, -slopes[g][h] * dist, NEG)))
            sc = jnp.concatenate(halves, axis=0)
            s_sc[g, c] = sc
            out.append(jnp.maximum(m_runs[g], jnp.max(sc.reshape(CK // SUBLANE, SUBLANE, cols), axis=0)))
        return tuple(out)

    n_chunk = ((qb + 1) * QB - 1) // CK + 1
    m_runs = lax.fori_loop(0, n_chunk, score_chunk, (jnp.full((SUBLANE, cols), NEG, F32),) * G_A)
    m_rows = [jnp.max(m_runs[g], axis=0, keepdims=True) for g in range(G_A)]
    acc_sc[...] = jnp.zeros(acc_sc.shape, F32)

    def value_chunk(c, carry):
        k0 = pl.multiple_of(c * CK, CK)
        for g in range(G_A):
            pc = jnp.exp2((s_sc[g, c] - m_rows[g]).astype(BF16))
            acc_sc[g] += _dot(vt_ref[g * LANE:(g + 1) * LANE, pl.ds(k0, CK)] + ones_row, pc)
        return carry

    lax.fori_loop(0, n_chunk, value_chunk, 0)

    pieces = []
    for g in range(G_A):
        o_sel, o_win = acc_sc[g], ow_sc[g]
        for h in range(HPG_A):
            col = g * HPG_A + h
            os_h, ow_h = o_sel[:, hs[h]], o_win[:, hs[h]]
            w_sel = gates_t[NH_A + col:NH_A + col + 1, :] / jnp.maximum(os_h[DK_A:DK_A + 1, :], 1e-30)
            w_win = gates_t[2 * NH_A + col:2 * NH_A + col + 1, :] / jnp.maximum(ow_h[DK_A:DK_A + 1, :], 1e-30)
            pieces.append(gates_t[col:col + 1, :] * o_cmps[g][:, hs[h]] + w_sel * os_h[:DK_A] + w_win * ow_h[:DK_A])

    out_t = jnp.concatenate(pieces, axis=0)
    out = jnp.concatenate(
        [jnp.concatenate([jnp.transpose(out_t[i * LANE:(i + 1) * LANE, r * LANE:(r + 1) * LANE])
                          for i in range(D_A // LANE)], axis=1) for r in range(n_qtile)], axis=0)
    o_ref[...] = out * jax.nn.silu(z_ref[...])


def _nsa(main, kk, vt, aux, kcmp, vcmp_t, bsz, t_len):
    nqb = t_len // QB
    cols = HPG_A * QB
    n_wslot = (WINDOW + QB) // LANE + 1
    cspec = pl.BlockSpec((1, N_CMP_PAD, LANE), lambda b, i: (b, 0, 0))
    return pl.pallas_call(
        _nsa_kernel,
        grid=(bsz, nqb),
        in_specs=[pl.BlockSpec((QB, D_A), lambda b, i: (b * nqb + i, 0)),
                  pl.BlockSpec((QB, D_A), lambda b, i: (b * nqb + i, 1)),
                  pl.BlockSpec((QB, LANE), lambda b, i: (b * nqb + i, 2)),
                  cspec, cspec,
                  pl.BlockSpec((t_len, K_W), lambda b, i: (b, 0)),
                  pl.BlockSpec((VT_W, t_len), lambda b, i: (0, b))],
        out_specs=pl.BlockSpec((QB, D_A), lambda b, i: (b * nqb + i, 0)),
        out_shape=jax.ShapeDtypeStruct((bsz * t_len, D_A), F32),
        scratch_shapes=[pltpu.VMEM((G_A, t_len // CK, CK, cols), F32), pltpu.VMEM((G_A, LANE, cols), F32),
                        pltpu.VMEM((G_A, LANE, cols), F32),
                        pltpu.VMEM((G_A, HPG_A * n_wslot, LANE, QB), F32), pltpu.VMEM((G_A, N_BLK, QB), F32)],
        compiler_params=_params("parallel", "arbitrary"),
        name="nsa_attention",
    )(main, main, aux, kcmp, vcmp_t, kk, vt)


def _sgu_kernel(u_ref, v_ref, z_ref, lng_ref, lnb_ref, w_ref, b_ref, o_ref):
    u = jax.nn.gelu(u_ref[...])
    v = jax.nn.gelu(v_ref[...])
    mu = jnp.mean(v, axis=-1, keepdims=True)
    var = jnp.mean(jnp.square(v - mu), axis=-1, keepdims=True)
    vn = ((v - mu) * lax.rsqrt(var + LN_EPS) * lng_ref[...] + lnb_ref[...]).astype(BF16)
    gate = u * jax.nn.silu(z_ref[...])
    ti = lax.broadcasted_iota(jnp.int32, (CHUNK_B, CHUNK_B), 0)
    si = lax.broadcasted_iota(jnp.int32, (CHUNK_B, CHUNK_B), 1)
    n_chunks = u_ref.shape[0] // CHUNK_B
    for g in range(G_B):
        w = jnp.where(si <= ti, w_ref[g], 0.0).astype(BF16)
        bias = b_ref[:, g:g + 1]
        ls = slice(g * LANE, (g + 1) * LANE)
        for c in range(n_chunks):
            rs = slice(c * CHUNK_B, (c + 1) * CHUNK_B)
            o_ref[rs, ls] = gate[rs, ls] * (_dot(w, vn[rs, ls]) + bias)


def _sgu(main, lng, lnb, w, b_t, tm):
    n = main.shape[0]
    blk = lambda col: pl.BlockSpec((tm, D_B), lambda i: (i, col))
    full = lambda a: pl.BlockSpec(a.shape, lambda i: (0,) * a.ndim)
    return pl.pallas_call(
        _sgu_kernel,
        grid=(n // tm,),
        in_specs=[blk(2), blk(3), blk(4), full(lng), full(lnb), full(w), full(b_t)],
        out_specs=pl.BlockSpec((tm, D_B), lambda i: (i, 0)),
        out_shape=jax.ShapeDtypeStruct((n, D_B), F32),
        compiler_params=_params("parallel"),
        name="spatial_gating",
    )(main, main, main, lng, lnb, w, b_t)


def _mlstm_kernel(q_ref, k_ref, v_ref, o_ref, z_ref, g_ref, cw_ref, cb_ref, gb_ref, ng_ref, y_ref,
                  qc_sc, kc_sc, halo_sc, st_sc, m_sc):
    sb = pl.program_id(1)
    blk = q_ref.shape[0]

    @pl.when(sb == 0)
    def _():
        halo_sc[...] = jnp.zeros(halo_sc.shape, F32)
        st_sc[...] = jnp.zeros(st_sc.shape, F32)
        m_sc[...] = jnp.zeros(m_sc.shape, F32)

    r8 = lax.broadcasted_iota(jnp.int32, (SUBLANE, 1), 0)

    def conv_silu(x_ref, which, dst_ref):
        x = x_ref[...]
        prev = halo_sc[which]
        w = cw_ref[:, which * D_C:(which + 1) * D_C]
        acc = x * w[CONV_W - 1:CONV_W, :] + cb_ref[:, which * D_C:(which + 1) * D_C]
        for j in range(1, CONV_W):
            rolled = pltpu.roll(x, j, 0)
            head = jnp.where(r8 < j, pltpu.roll(prev, j, 0), rolled[0:SUBLANE])
            shifted = jnp.concatenate([head, rolled[SUBLANE:]], axis=0)
            acc = acc + shifted * w[CONV_W - 1 - j:CONV_W - j, :]
        halo_sc[which] = x[blk - SUBLANE:blk]
        dst_ref[...] = jax.nn.silu(acc)

    conv_silu(q_ref, 0, qc_sc)
    conv_silu(k_ref, 1, kc_sc)

    ti = lax.broadcasted_iota(jnp.int32, (CHUNK_C, CHUNK_C), 0)
    si = lax.broadcasted_iota(jnp.int32, (CHUNK_C, CHUNK_C), 1)
    causal = si <= ti
    tri_l = jnp.where(causal, 1.0, 0.0).astype(BF16)
    ones_blk = jnp.where(si == 0, 1.0, 0.0).astype(BF16)
    ng = ng_ref[...]
    gbias = gb_ref[...]

    state = [st_sc[h] for h in range(NH_C)]
    m_state = [m_sc[h][:, 0:1] for h in range(NH_C)]

    for c in range(blk // CHUNK_C):
        rs = slice(c * CHUNK_C, (c + 1) * CHUNK_C)
        li = g_ref[rs, :] + gbias
        lf = _log_sigmoid(li)
        f1, f2, f3 = _split3(lf)
        bc = _dot(tri_l, f1) + _dot(tri_l, f2) + _dot(tri_l, f3)
        li_t = jnp.transpose(li)
        bc_t = jnp.transpose(bc)
        for h in range(NH_C):
            ls = slice(h * DH_C, (h + 1) * DH_C)
            q = qc_sc[rs, ls]
            k = kc_sc[rs, ls] * DH_C ** -0.5
            vext = jnp.concatenate([v_ref[rs, ls].astype(BF16), ones_blk], axis=1)
            li_col = li[:, GI_LANE + h:GI_LANE + h + 1]
            bc_col = bc[:, GF_LANE + h:GF_LANE + h + 1]
            b_last = bc[CHUNK_C - 1:CHUNK_C, GF_LANE + h:GF_LANE + h + 1]
            g_row = li_t[GI_LANE + h:GI_LANE + h + 1, :] - bc_t[GF_LANE + h:GF_LANE + h + 1, :]
            m_prev = m_state[h]
            m_loc = b_last + jnp.max(g_row, axis=1, keepdims=True)
            e_end = jnp.exp(b_last + (li_col - bc_col) - m_loc)

            mg = jnp.maximum(m_prev, jnp.max(jnp.where(causal, g_row, -jnp.inf), axis=1, keepdims=True))
            dec = jnp.exp(jnp.where(causal, g_row - mg, -jnp.inf))
            s = _nt_dot(q.astype(BF16), k.astype(BF16)) * dec
            e_int = jnp.exp(m_prev - mg)
            lhs = jnp.concatenate([s, e_int * q], axis=1).astype(BF16)
            rhs = jnp.concatenate([vext, state[h].astype(BF16)], axis=0)
            both = _dot(lhs, rhs)
            num = both[:, :DH_C]
            den = both[:, DH_C:DH_C + 1]
            hval = num / jnp.maximum(jnp.abs(den), jnp.exp(-(bc_col + mg)))
            mu = jnp.mean(hval, axis=-1, keepdims=True)
            var = jnp.mean(jnp.square(hval - mu), axis=-1, keepdims=True)
            hn = (hval - mu) * lax.rsqrt(var + LN_EPS) * ng[:, ls]
            y_ref[rs, ls] = hn * jax.nn.sigmoid(o_ref[rs, ls]) * jax.nn.silu(z_ref[rs, ls])

            loc = _tn_dot((k * e_end).astype(BF16), vext)
            m_new = jnp.maximum(b_last + m_prev, m_loc)
            state[h] = jnp.exp(b_last + m_prev - m_new) * state[h] + jnp.exp(m_loc - m_new) * loc
            m_state[h] = m_new

    for h in range(NH_C):
        st_sc[h] = state[h]
        m_sc[h] = jnp.broadcast_to(m_state[h], (1, LANE))


def _mlstm(main, aux, conv_w, conv_b, gbias, norm_g, bsz, t_len):
    nsb = t_len // SEQ_BLK_C
    blk = lambda col: pl.BlockSpec((SEQ_BLK_C, D_C), lambda b, s: (b * nsb + s, col))
    full = lambda a: pl.BlockSpec(a.shape, lambda b, s: (0,) * a.ndim)
    return pl.pallas_call(
        _mlstm_kernel,
        grid=(bsz, nsb),
        in_specs=[blk(5), blk(6), blk(7), blk(8), blk(9),
                  pl.BlockSpec((SEQ_BLK_C, LANE), lambda b, s: (b * nsb + s, 2)),
                  full(conv_w), full(conv_b), full(gbias), full(norm_g)],
        out_specs=pl.BlockSpec((SEQ_BLK_C, D_C), lambda b, s: (b * nsb + s, 0)),
        out_shape=jax.ShapeDtypeStruct((bsz * t_len, D_C), F32),
        scratch_shapes=[pltpu.VMEM((SEQ_BLK_C, D_C), F32), pltpu.VMEM((SEQ_BLK_C, D_C), F32),
                        pltpu.VMEM((2, SUBLANE, D_C), F32), pltpu.VMEM((NH_C, DH_C, 2 * DH_C), F32),
                        pltpu.VMEM((NH_C, 1, LANE), F32)],
        compiler_params=_params("parallel", "arbitrary"),
        name="mlstm",
    )(main, main, main, main, main, aux, conv_w, conv_b, gbias, norm_g)


def _merge_kernel(x_ref, ya_ref, yb_ref, yc_ref, g0_ref, g1_ref, g2_ref, p_ref,
                  wa_ref, wb_ref, wc_ref, wo_ref, wp_ref, wg_ref, lng_ref, lnb_ref, o_ref, ob_ref):
    merged = (jax.nn.sigmoid(g0_ref[...]) * _dot(ya_ref[...].astype(BF16), wa_ref[...])
              + jax.nn.sigmoid(g1_ref[...]) * _dot(yb_ref[...].astype(BF16), wb_ref[...])
              + jax.nn.sigmoid(g2_ref[...]) * _dot(yc_ref[...].astype(BF16), wc_ref[...]))
    r = ALPHA * x_ref[...] + _dot(merged.astype(BF16), wo_ref[...])
    r = r + jax.nn.sigmoid(_dot(r.astype(BF16), wg_ref[...])) * _dot(p_ref[...].astype(BF16), wp_ref[...])
    mu = jnp.mean(r, axis=-1, keepdims=True)
    var = jnp.mean(jnp.square(r - mu), axis=-1, keepdims=True)
    y = (r - mu) * lax.rsqrt(var + LN_EPS) * lng_ref[...] + lnb_ref[...]
    o_ref[...] = y
    ob_ref[...] = y.astype(BF16)


def _merge(x, ya, yb, yc, main, p, wa, wb, wc, wo, wp, wg, lng, lnb, tm):
    n = x.shape[0]
    row = lambda w: pl.BlockSpec((tm, w), lambda i: (i, 0))
    gate = lambda j: pl.BlockSpec((tm, D_MODEL), lambda i: (i, 5 + j))
    full = lambda a: pl.BlockSpec(a.shape, lambda i: (0,) * a.ndim)
    return pl.pallas_call(
        _merge_kernel,
        grid=(n // tm,),
        in_specs=[row(D_MODEL), row(D_A), row(D_B), row(D_C), gate(0), gate(1), gate(2), row(PLE_DIM),
                  full(wa), full(wb), full(wc), full(wo), full(wp), full(wg), full(lng), full(lnb)],
        out_specs=[row(D_MODEL), row(D_MODEL)],
        out_shape=[jax.ShapeDtypeStruct((n, D_MODEL), F32), jax.ShapeDtypeStruct((n, D_MODEL), BF16)],
        compiler_params=_params("parallel"),
        name="merge",
    )(x, ya, yb, yc, main, main, main, p, wa, wb, wc, wo, wp, wg, lng, lnb)


def _pad_groups(w):
    d = w.shape[0]
    return jnp.pad(w.reshape(d, G_A, DK_A), ((0, 0), (0, 0), (0, LANE - DK_A))).reshape(d, G_A * LANE)


def _layer(x, xb, p, w_in, cmp_pos_k, cmp_pos_v, cmp_wk1, cmp_wk2, cmp_wv1, cmp_wv2,
           sg_ln_g, sg_ln_b, sg_w, sg_b, ml_conv_w, ml_conv_b, ml_b_i, ml_b_f, ml_norm_g,
           w_br_a, w_br_b, w_br_c, w_out, ple_w, ple_gate, ln_g, ln_b, bsz, t_len):
    w_main = jnp.concatenate([w_in[:, 0:512], w_in[:, 1304:4888], w_in[:, 4896:8992]], axis=1).astype(BF16)
    w_k = jnp.concatenate([w_in[:, 768:896], w_in[:, 1024:1152]], axis=1).astype(BF16)
    w_vt = jnp.concatenate([_pad_groups(w_in[:, 896:1024]), _pad_groups(w_in[:, 1152:1280])], axis=1).T.astype(BF16)
    w_aux = jnp.concatenate([w_in[:, 512:768], w_in[:, 1280:1304], jnp.zeros((D_MODEL, 8), F32),
                             w_in[:, 4888:4896], jnp.zeros((D_MODEL, 88), F32)], axis=1).astype(BF16)
    main = _matmul(xb, w_main, 2048, 512, "proj_main")
    kk, vt, aux = _tail_matmul(xb, w_k, w_vt, w_aux, 2048)

    rows16 = t_len // STRIDE_CMP
    zk = aux[:, 0:LANE].reshape(bsz, rows16, STRIDE_CMP * LANE)
    zv = aux[:, LANE:2 * LANE].reshape(bsz, rows16, STRIDE_CMP * LANE)
    wk1e, wk2e, pke = _compress_weights(cmp_wk1, cmp_wk2, cmp_pos_k)
    wv1e, wv2e, pve = _compress_weights(cmp_wv1, cmp_wv2, cmp_pos_v)
    kcmp, vcmp_t = _compress(zk, zv, pke, pve, wk1e, wk2e, wv1e, wv2e.T)
    ya = _nsa(main, kk, vt, aux, kcmp, vcmp_t, bsz, t_len)

    yb = _sgu(main, sg_ln_g[None, :], sg_ln_b[None, :], sg_w, sg_b.T, 512)

    gbias = jnp.zeros((1, LANE), F32).at[0, GI_LANE:GI_LANE + NH_C].set(ml_b_i)
    gbias = gbias.at[0, GF_LANE:GF_LANE + NH_C].set(ml_b_f)
    yc = _mlstm(main, aux, ml_conv_w, ml_conv_b[None, :], gbias, ml_norm_g[None, :], bsz, t_len)

    return _merge(x, ya, yb, yc, main, p, w_br_a.astype(BF16), w_br_b.astype(BF16), w_br_c.astype(BF16),
                  w_out.astype(BF16), ple_w.astype(BF16), ple_gate.astype(BF16),
                  ln_g[None, :], ln_b[None, :], 256)


def kernel(x, p, w_in, cmp_pos_k, cmp_pos_v, cmp_wk1, cmp_wk2, cmp_wv1, cmp_wv2, sg_ln_g, sg_ln_b, sg_w, sg_b,
           ml_conv_w, ml_conv_b, ml_b_i, ml_b_f, ml_norm_g, w_br_a, w_br_b, w_br_c, w_out, ple_w, ple_gate,
           ln_g, ln_b):
    bsz, t_len, d = x.shape
    assert d == D_MODEL and t_len == N_CMP_PAD * STRIDE_CMP and t_len // L_SEL == N_BLK
    assert t_len % SEQ_BLK_C == 0 and t_len % CK == 0
    xf = x.reshape(bsz * t_len, d)
    xb = xf.astype(BF16)
    for i in range(w_in.shape[0]):
        xf, xb = _layer(xf, xb, p[i].reshape(bsz * t_len, PLE_DIM), w_in[i], cmp_pos_k[i], cmp_pos_v[i],
                        cmp_wk1[i], cmp_wk2[i], cmp_wv1[i], cmp_wv2[i], sg_ln_g[i], sg_ln_b[i], sg_w[i],
                        sg_b[i], ml_conv_w[i], ml_conv_b[i], ml_b_i[i], ml_b_f[i], ml_norm_g[i],
                        w_br_a[i], w_br_b[i], w_br_c[i], w_out[i], ple_w[i], ple_gate[i], ln_g[i], ln_b[i],
                        bsz, t_len)
    return xf.reshape(bsz, t_len, d)
```

```python
import jax
import jax.numpy as jnp
from jax import lax
from jax.experimental import pallas as pl
from jax.experimental.pallas import tpu as pltpu

F32 = jnp.float32
BF16 = jnp.bfloat16

D_MODEL = 1024
PLE_DIM = 256
D_A = 512
NH_A = 8
G_A = 2
HPG_A = 4
DK_A = 64
L_CMP = 32
STRIDE_CMP = 16
CMP_HID = 128
L_SEL = 64
N_SEL = 8
N_BLK = 32
WINDOW = 256
QB = 256
CK = 256
N_CMP_PAD = 128
BIG = 1e9
D_B = 512
CHUNK_B = 128
G_B = 4
D_C = 512
NH_C = 4
DH_C = 128
CONV_W = 4
CHUNK_C = 128
SEQ_BLK_C = 512
LN_EPS = 1e-5
DEPTH = 2
ALPHA = (2.0 * DEPTH) ** 0.25

LANE = 128
SUBLANE = 8
MAIN_W = 7680
COL_AQ, COL_AZ, COL_BU, COL_BV, COL_BZ, COL_CQ, COL_CK, COL_CO, COL_CZ = 6, 7, 8, 9, 10, 11, 12, 13, 14
K_W = 256
NSA_VT_W = 512
VT_W = NSA_VT_W + 512
DVX = 128 + 16
GI_LANE = 32
GF_LANE = 36
V7X_VMEM_BYTES = 64 * 1024 * 1024
VMEM_LIMIT = V7X_VMEM_BYTES * 13 // 16
NEG = -1e30
LOG2E = 1.4426950408889634


def _params(*sem):
    return pltpu.CompilerParams(dimension_semantics=sem, vmem_limit_bytes=VMEM_LIMIT)


def _nt_dot(a, b):
    return lax.dot_general(a, b, (((1,), (1,)), ((), ())), preferred_element_type=F32)


def _dot(a, b):
    return jnp.dot(a, b, preferred_element_type=F32)


def _split3(a):
    a1 = a.astype(BF16)
    r1 = a - a1.astype(F32)
    a2 = r1.astype(BF16)
    a3 = (r1 - a2.astype(F32)).astype(BF16)
    return a1, a2, a3


def _log_sigmoid(x):
    return jnp.minimum(x, 0.0) - jnp.log1p(jnp.exp(-jnp.abs(x)))


def _layer_spec(a, layer, grid_rank):
    zeros = (0,) * (a.ndim - 1)
    if grid_rank == 1:
        return pl.BlockSpec((None,) + a.shape[1:], lambda i: (layer,) + zeros)
    return pl.BlockSpec((None,) + a.shape[1:], lambda i, j: (layer,) + zeros)


def _mm_kernel(x_ref, w_ref, o_ref):
    o_ref[...] = _dot(x_ref[...].astype(BF16), w_ref[...]).astype(o_ref.dtype)


def _matmul(x, w, layer, tm, tn, name):
    m, k = x.shape
    n = w.shape[2]
    return pl.pallas_call(
        _mm_kernel,
        grid=(m // tm, n // tn),
        in_specs=[pl.BlockSpec((tm, k), lambda i, j: (i, 0)),
                  pl.BlockSpec((None, k, tn), lambda i, j: (layer, 0, j))],
        out_specs=pl.BlockSpec((tm, tn), lambda i, j: (i, j)),
        out_shape=jax.ShapeDtypeStruct((m, n), F32),
        compiler_params=_params("parallel", "arbitrary"),
        name=name,
    )(x, w)


def _tail_kernel(x_ref, wk_ref, wvt_ref, waux_ref, ok_ref, ovt_ref, okc_ref, ovc_ref, osm_ref):
    x = x_ref[...].astype(BF16)
    ok_ref[...] = _dot(x, wk_ref[...]).astype(BF16)
    ovt_ref[...] = _nt_dot(wvt_ref[...], x).astype(BF16)
    aux = _dot(x, waux_ref[...])
    okc_ref[...] = aux[:, 0:LANE]
    ovc_ref[...] = aux[:, LANE:2 * LANE]
    osm_ref[...] = aux[:, 2 * LANE:3 * LANE]


def _tail_matmul(x, wk, wvt, waux, layer, tm):
    m, k = x.shape
    tok = lambda w: pl.BlockSpec((tm, w), lambda i: (i, 0))
    f32_out = jax.ShapeDtypeStruct((m, LANE), F32)
    return pl.pallas_call(
        _tail_kernel,
        grid=(m // tm,),
        in_specs=[tok(k), _layer_spec(wk, layer, 1), _layer_spec(wvt, layer, 1), _layer_spec(waux, layer, 1)],
        out_specs=[tok(K_W), pl.BlockSpec((VT_W, tm), lambda i: (0, i)), tok(LANE), tok(LANE), tok(LANE)],
        out_shape=[jax.ShapeDtypeStruct((m, K_W), BF16), jax.ShapeDtypeStruct((VT_W, m), BF16),
                   f32_out, f32_out, f32_out],
        compiler_params=_params("parallel"),
        name="proj_tail",
    )(x, wk, wvt, waux)


def _compress_kernel(zk_ref, zv_ref, pk_ref, pv_ref, wk1_ref, wk2_ref, wv1_ref, wv2t_ref, ko_ref, vot_ref):
    def hidden(z_ref, p_ref, w1_ref):
        z = z_ref[0]
        r0 = _dot((z + p_ref[0:1, :]).astype(BF16), w1_ref[0])
        r1 = _dot((z + p_ref[1:2, :]).astype(BF16), w1_ref[1])
        hid = r0 + pltpu.roll(r1, r1.shape[0] - 1, 0)
        return jax.nn.gelu(hid).astype(BF16)

    ko_ref[0] = _dot(hidden(zk_ref, pk_ref, wk1_ref), wk2_ref[...]).astype(BF16)
    vot_ref[0] = _nt_dot(wv2t_ref[...], hidden(zv_ref, pv_ref, wv1_ref)).astype(BF16)


def _compress(zk, zv, pk, pv, wk1, wk2, wv1, wv2t, layer):
    b, n, w = zk.shape
    zspec = pl.BlockSpec((1, n, w), lambda i: (i, 0, 0))
    full = lambda a: _layer_spec(a, layer, 1)
    ospec = pl.BlockSpec((1, n, LANE), lambda i: (i, 0, 0))
    return pl.pallas_call(
        _compress_kernel,
        grid=(b,),
        in_specs=[zspec, zspec, full(pk), full(pv), full(wk1), full(wk2), full(wv1), full(wv2t)],
        out_specs=[ospec, ospec],
        out_shape=[jax.ShapeDtypeStruct((b, n, LANE), BF16)] * 2,
        compiler_params=_params("parallel"),
        name="nsa_compress",
    )(zk, zv, pk, pv, wk1, wk2, wv1, wv2t)


def _compress_weights(w1, w2, pos, transpose_out):
    nl = w1.shape[0]
    eye = jnp.eye(G_A, dtype=F32)
    w1e = jnp.einsum('nldh,Gg->nlGdgh', w1.reshape(nl, L_CMP, DK_A, CMP_HID), eye)
    w1e = w1e.reshape(nl, 2, STRIDE_CMP * G_A * DK_A, G_A * CMP_HID).astype(BF16)
    w2e = jnp.einsum('nhd,gG->nghGd', w2, eye).reshape(nl, G_A * CMP_HID, G_A * DK_A)
    w2e = (jnp.swapaxes(w2e, 1, 2) if transpose_out else w2e).astype(BF16)
    pe = jnp.broadcast_to(pos[:, :, None, :], (nl, L_CMP, G_A, DK_A)).reshape(nl, 2, STRIDE_CMP * G_A * DK_A)
    return w1e, w2e, pe


def _nsa_kernel(q_ref, z_ref, g_ref, kc_ref, vct_ref, k_ref, vt_ref, o_ref, s_sc, acc_sc, ow_sc, bw_sc, sel_sc):
    qb = pl.program_id(1)
    t0 = qb * QB
    cols = HPG_A * QB
    n_qtile = QB // LANE
    n_wchunk = (WINDOW + QB) // LANE
    n_wslot = n_wchunk + 1
    k_i = lax.broadcasted_iota(jnp.int32, (LANE, 1), 0)
    t_i = lax.broadcasted_iota(jnp.int32, (1, QB), 1)
    tk = (t_i - k_i).astype(F32)
    ones_row = jnp.where(lax.broadcasted_iota(jnp.int32, (LANE, CK), 0) == DK_A, 1.0, 0.0).astype(BF16)
    slopes = [[LOG2E * 2.0 ** -(g * HPG_A + h + 1) for h in range(HPG_A)] for g in range(G_A)]
    hs = [slice(h * QB, (h + 1) * QB) for h in range(HPG_A)]

    def biased(s, bias_of_head):
        return jnp.concatenate([s[:, hs[h]] + bias_of_head(h) for h in range(HPG_A)], axis=1)

    @pl.when(qb == 0)
    def _():
        for g in range(G_A):
            for h in range(HPG_A):
                for j in range(n_wchunk):
                    dist = tk + float(WINDOW - j * LANE)
                    ok = (dist >= 0.0) & (dist < WINDOW)
                    bw_sc[g, h * n_wslot + j] = jnp.where(ok, -slopes[g][h] * dist, NEG)
                bw_sc[g, h * n_wslot + n_wchunk] = jnp.full((LANE, QB), NEG, F32)

    gsig = jax.nn.sigmoid(g_ref[...])
    gates_t = jnp.concatenate([jnp.transpose(gsig[r * LANE:(r + 1) * LANE]) for r in range(n_qtile)], axis=1)

    jb = lax.broadcasted_iota(jnp.int32, (N_BLK, QB), 0)
    cur = (t0 + lax.broadcasted_iota(jnp.int32, (N_BLK, QB), 1)) >> 6
    forced = (jb == 0) | (jb == cur) | (jb == cur - 1)
    allowed = jb <= cur
    ov_j = lax.broadcasted_iota(jnp.int32, (N_BLK, N_CMP_PAD), 0) * L_SEL
    ov_n = lax.broadcasted_iota(jnp.int32, (N_BLK, N_CMP_PAD), 1) * STRIDE_CMP
    ov_t = jnp.where((ov_n < ov_j + L_SEL) & (ov_n + L_CMP > ov_j), 1.0, 0.0).astype(BF16)
    jb8 = lax.broadcasted_iota(jnp.int32, (SUBLANE, QB), 0)

    d_cmp = (t0 - (L_CMP - 1)).astype(F32) + (t_i - STRIDE_CMP * k_i).astype(F32)
    valid_cmp = (d_cmp >= 0.0) & (k_i < N_CMP_PAD - 1)

    gls = [slice(g * DK_A, (g + 1) * DK_A) for g in range(G_A)]
    q4s, o_cmps = [], []
    for g in range(G_A):
        q4 = jnp.concatenate(
            [q_ref[:, (g * HPG_A + h) * DK_A:(g * HPG_A + h + 1) * DK_A] for h in range(HPG_A)], axis=0)
        q4 = (q4 * (LOG2E * DK_A ** -0.5)).astype(BF16)
        q4s.append(q4)

        s = biased(_nt_dot(kc_ref[0][:, gls[g]], q4),
                   lambda h: jnp.where(valid_cmp, -slopes[g][h] * d_cmp, -jnp.inf))
        mx = jnp.max(s, axis=0, keepdims=True)
        mx = jnp.where(mx > -jnp.inf, mx, 0.0)
        e = jnp.exp2(s - mx)
        p = e * (1.0 / jnp.maximum(jnp.sum(e, axis=0, keepdims=True), 1e-30))
        o_cmps.append(_dot(vct_ref[0][gls[g], :], p.astype(BF16)))

        psum = p[:, hs[0]] + p[:, hs[1]] + p[:, hs[2]] + p[:, hs[3]]
        p1, p2, p3 = _split3(psum)
        imp = _dot(ov_t, p1) + _dot(ov_t, p2) + _dot(ov_t, p3)
        score = jnp.where(allowed, jnp.where(forced, BIG, imp), -BIG)
        tiles = [score[r * SUBLANE:(r + 1) * SUBLANE] for r in range(N_BLK // SUBLANE)]
        cnt = [jnp.zeros((SUBLANE, QB), F32) for _ in tiles]
        for j in range(N_BLK):
            sj = score[j:j + 1, :]
            for r, tile in enumerate(tiles):
                lo = r * SUBLANE
                if j >= lo + SUBLANE:
                    beats = sj > tile
                elif j < lo:
                    beats = sj >= tile
                else:
                    beats = (sj > tile) | ((sj >= tile) & (jb8 > j - lo))
                cnt[r] = cnt[r] + jnp.where(beats, 1.0, 0.0)
        sel_sc[g] = jnp.where((jnp.concatenate(cnt, axis=0) < N_SEL) & allowed, 1.0, 0.0)

        s_w, v_w = [], []
        for j in range(n_wchunk):
            c = qb * n_qtile - WINDOW // LANE + j
            k0 = pl.multiple_of(jnp.maximum(c, 0) * LANE, LANE)
            slot = jnp.where(c >= 0, j, n_wchunk)
            sj = _nt_dot(k_ref[pl.ds(k0, LANE), LANE + g * DK_A:LANE + (g + 1) * DK_A], q4)
            s_w.append(biased(sj, lambda h: bw_sc[g, h * n_wslot + slot]))
            v_w.append(vt_ref[G_A * LANE + g * LANE:G_A * LANE + (g + 1) * LANE, pl.ds(k0, LANE)]
                       + ones_row[:, :LANE])
        mx = s_w[0]
        for j in range(1, n_wchunk):
            mx = jnp.maximum(mx, s_w[j])
        mx = jnp.max(mx, axis=0, keepdims=True)
        o_win = _dot(v_w[0], jnp.exp2((s_w[0] - mx).astype(BF16)))
        for j in range(1, n_wchunk):
            o_win = o_win + _dot(v_w[j], jnp.exp2((s_w[j] - mx).astype(BF16)))
        ow_sc[g] = o_win

    def score_chunk(c, m_runs):
        k0 = pl.multiple_of(c * CK, CK)
        out = []
        for g in range(G_A):
            sc = _nt_dot(k_ref[pl.ds(k0, CK), gls[g]], q4s[g])
            halves = []
            for u in range(CK // LANE):
                dist = tk + (t0 - k0 - u * LANE).astype(F32)
                blk0 = (CK // L_SEL) * c + (LANE // L_SEL) * u
                chosen = jnp.concatenate(
                    [jnp.broadcast_to(sel_sc[g, pl.ds(blk0 + i, 1), :], (L_SEL, QB))
                     for i in range(LANE // L_SEL)], axis=0)
                ok = (chosen > 0.5) & (dist >= 0.0)
                halves.append(biased(sc[u * LANE:(u + 1) * LANE],
                                     lambda h: jnp.where(ok, -slopes[g][h] * dist, NEG)))
            sc = jnp.concatenate(halves, axis=0)
            s_sc[g, c] = sc
            out.append(jnp.maximum(m_runs[g], jnp.max(sc.reshape(CK // SUBLANE, SUBLANE, cols), axis=0)))
        return tuple(out)

    n_chunk = ((qb + 1) * QB - 1) // CK + 1
    m_runs = lax.fori_loop(0, n_chunk, score_chunk, (jnp.full((SUBLANE, cols), NEG, F32),) * G_A)
    m_rows = [jnp.max(m_runs[g], axis=0, keepdims=True) for g in range(G_A)]
    acc_sc[...] = jnp.zeros(acc_sc.shape, F32)

    def value_chunk(c, carry):
        k0 = pl.multiple_of(c * CK, CK)
        for g in range(G_A):
            pc = jnp.exp2((s_sc[g, c] - m_rows[g]).astype(BF16))
            acc_sc[g] += _dot(vt_ref[g * LANE:(g + 1) * LANE, pl.ds(k0, CK)] + ones_row, pc)
        return carry

    lax.fori_loop(0, n_chunk, value_chunk, 0)

    pieces = []
    for g in range(G_A):
        o_sel, o_win = acc_sc[g], ow_sc[g]
        for h in range(HPG_A):
            col = g * HPG_A + h
            os_h, ow_h = o_sel[:, hs[h]], o_win[:, hs[h]]
            w_sel = gates_t[NH_A + col:NH_A + col + 1, :] / jnp.maximum(os_h[DK_A:DK_A + 1, :], 1e-30)
            w_win = gates_t[2 * NH_A + col:2 * NH_A + col + 1, :] / jnp.maximum(ow_h[DK_A:DK_A + 1, :], 1e-30)
            pieces.append(gates_t[col:col + 1, :] * o_cmps[g][:, hs[h]] + w_sel * os_h[:DK_A] + w_win * ow_h[:DK_A])

    out_t = jnp.concatenate(pieces, axis=0)
    out = jnp.concatenate(
        [jnp.concatenate([jnp.transpose(out_t[i * LANE:(i + 1) * LANE, r * LANE:(r + 1) * LANE])
                          for i in range(D_A // LANE)], axis=1) for r in range(n_qtile)], axis=0)
    o_ref[...] = out * jax.nn.silu(z_ref[...])


def _nsa(main, kk, vt, small, kcmp, vcmp_t, bsz, t_len):
    nqb = t_len // QB
    cols = HPG_A * QB
    n_wslot = (WINDOW + QB) // LANE + 1
    cspec = pl.BlockSpec((1, N_CMP_PAD, LANE), lambda b, i: (b, 0, 0))
    return pl.pallas_call(
        _nsa_kernel,
        grid=(bsz, nqb),
        in_specs=[pl.BlockSpec((QB, D_A), lambda b, i: (b * nqb + i, COL_AQ)),
                  pl.BlockSpec((QB, D_A), lambda b, i: (b * nqb + i, COL_AZ)),
                  pl.BlockSpec((QB, LANE), lambda b, i: (b * nqb + i, 0)),
                  cspec, cspec,
                  pl.BlockSpec((t_len, K_W), lambda b, i: (b, 0)),
                  pl.BlockSpec((NSA_VT_W, t_len), lambda b, i: (0, b))],
        out_specs=pl.BlockSpec((QB, D_A), lambda b, i: (b * nqb + i, 0)),
        out_shape=jax.ShapeDtypeStruct((bsz * t_len, D_A), F32),
        scratch_shapes=[pltpu.VMEM((G_A, t_len // CK, CK, cols), F32), pltpu.VMEM((G_A, LANE, cols), F32),
                        pltpu.VMEM((G_A, LANE, cols), F32),
                        pltpu.VMEM((G_A, HPG_A * n_wslot, LANE, QB), F32), pltpu.VMEM((G_A, N_BLK, QB), F32)],
        compiler_params=_params("parallel", "arbitrary"),
        name="nsa_attention",
    )(main, main, small, kcmp, vcmp_t, kk, vt)


def _sgu_kernel(u_ref, v_ref, z_ref, lng_ref, lnb_ref, w_ref, b_ref, o_ref):
    u = jax.nn.gelu(u_ref[...])
    v = jax.nn.gelu(v_ref[...])
    mu = jnp.mean(v, axis=-1, keepdims=True)
    var = jnp.mean(jnp.square(v - mu), axis=-1, keepdims=True)
    vn = ((v - mu) * lax.rsqrt(var + LN_EPS) * lng_ref[...] + lnb_ref[...]).astype(BF16)
    gate = u * jax.nn.silu(z_ref[...])
    ti = lax.broadcasted_iota(jnp.int32, (CHUNK_B, CHUNK_B), 0)
    si = lax.broadcasted_iota(jnp.int32, (CHUNK_B, CHUNK_B), 1)
    n_chunks = u_ref.shape[0] // CHUNK_B
    for g in range(G_B):
        w = jnp.where(si <= ti, w_ref[g], 0.0).astype(BF16)
        bias = b_ref[:, g:g + 1]
        ls = slice(g * LANE, (g + 1) * LANE)
        for c in range(n_chunks):
            rs = slice(c * CHUNK_B, (c + 1) * CHUNK_B)
            o_ref[rs, ls] = gate[rs, ls] * (_dot(w, vn[rs, ls]) + bias)


def _sgu(main, lng, lnb, w, b_t, layer, tm):
    n = main.shape[0]
    blk = lambda col: pl.BlockSpec((tm, D_B), lambda i: (i, col))
    full = lambda a: _layer_spec(a, layer, 1)
    return pl.pallas_call(
        _sgu_kernel,
        grid=(n // tm,),
        in_specs=[blk(COL_BU), blk(COL_BV), blk(COL_BZ), full(lng), full(lnb), full(w), full(b_t)],
        out_specs=pl.BlockSpec((tm, D_B), lambda i: (i, 0)),
        out_shape=jax.ShapeDtypeStruct((n, D_B), F32),
        compiler_params=_params("parallel"),
        name="spatial_gating",
    )(main, main, main, lng, lnb, w, b_t)


def _mlstm_kernel(q_ref, k_ref, vt_ref, o_ref, z_ref, g_ref, cw_ref, cb_ref, gb_ref, ng_ref, y_ref,
                  qc_sc, kc_sc, halo_sc, st_sc, m_sc):
    sb = pl.program_id(1)
    blk = q_ref.shape[0]

    @pl.when(sb == 0)
    def _():
        halo_sc[...] = jnp.zeros(halo_sc.shape, F32)
        st_sc[...] = jnp.zeros(st_sc.shape, F32)
        m_sc[...] = jnp.zeros(m_sc.shape, F32)

    r8 = lax.broadcasted_iota(jnp.int32, (SUBLANE, 1), 0)

    def conv_silu(x_ref, which, dst_ref, scale):
        x = x_ref[...]
        prev = halo_sc[which]
        w = cw_ref[:, which * D_C:(which + 1) * D_C]
        acc = x * w[CONV_W - 1:CONV_W, :] + cb_ref[:, which * D_C:(which + 1) * D_C]
        for j in range(1, CONV_W):
            rolled = pltpu.roll(x, j, 0)
            head = jnp.where(r8 < j, pltpu.roll(prev, j, 0), rolled[0:SUBLANE])
            shifted = jnp.concatenate([head, rolled[SUBLANE:]], axis=0)
            acc = acc + shifted * w[CONV_W - 1 - j:CONV_W - j, :]
        halo_sc[which] = x[blk - SUBLANE:blk]
        dst_ref[...] = (jax.nn.silu(acc) * scale).astype(BF16)

    conv_silu(q_ref, 0, qc_sc, 1.0)
    conv_silu(k_ref, 1, kc_sc, DH_C ** -0.5)

    si = lax.broadcasted_iota(jnp.int32, (CHUNK_C, CHUNK_C), 0)
    ti = lax.broadcasted_iota(jnp.int32, (CHUNK_C, CHUNK_C), 1)
    causal = si <= ti
    tri_u = jnp.where(causal, 1.0, 0.0).astype(BF16)
    ones_rows = jnp.where(lax.broadcasted_iota(jnp.int32, (DVX - DH_C, CHUNK_C), 0) == 0, 1.0, 0.0).astype(BF16)
    ng = ng_ref[...]
    gbias = gb_ref[...]

    for c in range(blk // CHUNK_C):
        rs = slice(c * CHUNK_C, (c + 1) * CHUNK_C)
        gate_rows = jnp.transpose(g_ref[rs, :] + gbias)[GI_LANE:GI_LANE + 2 * NH_C]
        f1, f2, f3 = _split3(_log_sigmoid(gate_rows))
        bc_rows = _dot(f1, tri_u) + _dot(f2, tri_u) + _dot(f3, tri_u)
        g_rows = gate_rows[0:NH_C] - bc_rows[NH_C:2 * NH_C]
        g_cols = jnp.transpose(jnp.concatenate([g_rows, jnp.zeros((CHUNK_C - NH_C, CHUNK_C), F32)], axis=0))
        for h in range(NH_C):
            ls = slice(h * DH_C, (h + 1) * DH_C)
            q = qc_sc[rs, ls]
            k = kc_sc[rs, ls]
            vext = jnp.concatenate([vt_ref[ls, rs], ones_rows], axis=0)
            g_col = g_cols[:, h:h + 1]
            g_row = g_rows[h:h + 1, :]
            bc_row = bc_rows[NH_C + h:NH_C + h + 1, :]
            b_last = bc_row[:, CHUNK_C - 1:CHUNK_C]
            m_prev = m_sc[h][:, 0:1]
            st = st_sc[h]
            m_loc = b_last + jnp.max(g_row, axis=1, keepdims=True)
            e_end = jnp.exp(b_last + g_row - m_loc)

            gmat = jnp.where(causal, g_col, -jnp.inf)
            mg = jnp.maximum(m_prev, jnp.max(gmat, axis=0, keepdims=True))
            s_t = _nt_dot(k, q) * jnp.exp(gmat - mg)
            e_int = jnp.exp(m_prev - mg)
            lhs = jnp.concatenate([s_t, jnp.transpose(q.astype(F32)) * e_int], axis=0).astype(BF16)
            rhs = jnp.concatenate([vext, st.astype(BF16)], axis=1)
            both = _dot(rhs, lhs)
            den = both[DH_C:DH_C + 1, :]
            hval = both[:DH_C] * (1.0 / jnp.maximum(jnp.abs(den), jnp.exp(-(bc_row + mg))))
            mu = jnp.mean(hval, axis=0, keepdims=True)
            var = jnp.mean(jnp.square(hval - mu), axis=0, keepdims=True)
            hn = jnp.transpose((hval - mu) * lax.rsqrt(var + LN_EPS))
            o_gate = o_ref[rs, ls]
            z_gate = z_ref[rs, ls]
            y_ref[rs, ls] = hn * ng[:, ls] * z_gate / ((1.0 + jnp.exp(-o_gate)) * (1.0 + jnp.exp(-z_gate)))

            loc = _dot((vext.astype(F32) * e_end).astype(BF16), k)
            m_new = jnp.maximum(b_last + m_prev, m_loc)
            st_sc[h] = jnp.exp(b_last + m_prev - m_new) * st + jnp.exp(m_loc - m_new) * loc
            m_sc[h] = jnp.broadcast_to(m_new, (1, LANE))


def _mlstm(main, vt, small, conv_w, conv_b, gbias, norm_g, layer, bsz, t_len):
    nsb = t_len // SEQ_BLK_C
    blk = lambda col: pl.BlockSpec((SEQ_BLK_C, D_C), lambda b, s: (b * nsb + s, col))
    full = lambda a: _layer_spec(a, layer, 2)
    return pl.pallas_call(
        _mlstm_kernel,
        grid=(bsz, nsb),
        in_specs=[blk(COL_CQ), blk(COL_CK),
                  pl.BlockSpec((D_C, SEQ_BLK_C), lambda b, s: (NSA_VT_W // D_C, b * nsb + s)),
                  blk(COL_CO), blk(COL_CZ),
                  pl.BlockSpec((SEQ_BLK_C, LANE), lambda b, s: (b * nsb + s, 0)),
                  full(conv_w), full(conv_b), full(gbias), full(norm_g)],
        out_specs=pl.BlockSpec((SEQ_BLK_C, D_C), lambda b, s: (b * nsb + s, 0)),
        out_shape=jax.ShapeDtypeStruct((bsz * t_len, D_C), F32),
        scratch_shapes=[pltpu.VMEM((SEQ_BLK_C, D_C), BF16), pltpu.VMEM((SEQ_BLK_C, D_C), BF16),
                        pltpu.VMEM((2, SUBLANE, D_C), F32), pltpu.VMEM((NH_C, DVX, DH_C), F32),
                        pltpu.VMEM((NH_C, 1, LANE), F32)],
        compiler_params=_params("parallel", "arbitrary"),
        name="mlstm",
    )(main, main, vt, main, main, small, conv_w, conv_b, gbias, norm_g)


def _merge_kernel(x_ref, ya_ref, yb_ref, yc_ref, g0_ref, g1_ref, g2_ref, p_ref,
                  wa_ref, wb_ref, wc_ref, wo_ref, wp_ref, wg_ref, lng_ref, lnb_ref, o_ref, ob_ref):
    merged = (jax.nn.sigmoid(g0_ref[...]) * _dot(ya_ref[...].astype(BF16), wa_ref[...])
              + jax.nn.sigmoid(g1_ref[...]) * _dot(yb_ref[...].astype(BF16), wb_ref[...])
              + jax.nn.sigmoid(g2_ref[...]) * _dot(yc_ref[...].astype(BF16), wc_ref[...]))
    r = ALPHA * x_ref[...] + _dot(merged.astype(BF16), wo_ref[...])
    r = r + jax.nn.sigmoid(_dot(r.astype(BF16), wg_ref[...])) * _dot(p_ref[...].astype(BF16), wp_ref[...])
    mu = jnp.mean(r, axis=-1, keepdims=True)
    var = jnp.mean(jnp.square(r - mu), axis=-1, keepdims=True)
    y = (r - mu) * lax.rsqrt(var + LN_EPS) * lng_ref[...] + lnb_ref[...]
    o_ref[...] = y
    ob_ref[...] = y.astype(BF16)


def _merge(x, ya, yb, yc, main, p, wa, wb, wc, wo, wp, wg, lng, lnb, layer, tm):
    n = x.shape[0]
    row = lambda w: pl.BlockSpec((tm, w), lambda i: (i, 0))
    gate = lambda j: pl.BlockSpec((tm, D_MODEL), lambda i: (i, j))
    full = lambda a: _layer_spec(a, layer, 1)
    return pl.pallas_call(
        _merge_kernel,
        grid=(n // tm,),
        in_specs=[row(D_MODEL), row(D_A), row(D_B), row(D_C), gate(0), gate(1), gate(2),
                  pl.BlockSpec((None, tm, PLE_DIM), lambda i: (layer, i, 0)),
                  full(wa), full(wb), full(wc), full(wo), full(wp), full(wg), full(lng), full(lnb)],
        out_specs=[row(D_MODEL), row(D_MODEL)],
        out_shape=[jax.ShapeDtypeStruct((n, D_MODEL), F32), jax.ShapeDtypeStruct((n, D_MODEL), BF16)],
        compiler_params=_params("parallel"),
        name="merge",
    )(x, ya, yb, yc, main, main, main, p, wa, wb, wc, wo, wp, wg, lng, lnb)


def _pad_groups(w):
    nl, d = w.shape[0], w.shape[1]
    return jnp.pad(w.reshape(nl, d, G_A, DK_A), ((0, 0), (0, 0), (0, 0), (0, LANE - DK_A))).reshape(nl, d, G_A * LANE)


def _prepare(w_in, cmp_pos_k, cmp_pos_v, cmp_wk1, cmp_wk2, cmp_wv1, cmp_wv2, sg_ln_g, sg_ln_b, sg_b,
             ml_conv_b, ml_b_i, ml_b_f, ml_norm_g, w_br_a, w_br_b, w_br_c, w_out, ple_w, ple_gate, ln_g, ln_b):
    nl = w_in.shape[0]
    w = w_in
    prm = {}
    prm['w_main'] = jnp.concatenate([w[:, :, 5920:8992], w[:, :, 0:512], w[:, :, 1304:4376], w[:, :, 4896:5920]],
                                    axis=2).astype(BF16)
    prm['w_k'] = jnp.concatenate([w[:, :, 768:896], w[:, :, 1024:1152]], axis=2).astype(BF16)
    prm['w_vt'] = jnp.swapaxes(jnp.concatenate([_pad_groups(w[:, :, 896:1024]), _pad_groups(w[:, :, 1152:1280]),
                                                w[:, :, 4376:4888]], axis=2), 1, 2).astype(BF16)
    prm['w_aux'] = jnp.concatenate([w[:, :, 512:768], w[:, :, 1280:1304], jnp.zeros((nl, D_MODEL, 8), F32),
                                    w[:, :, 4888:4896], jnp.zeros((nl, D_MODEL, 88), F32)], axis=2).astype(BF16)
    prm['wk1'], prm['wk2'], prm['pk'] = _compress_weights(cmp_wk1, cmp_wk2, cmp_pos_k, False)
    prm['wv1'], prm['wv2t'], prm['pv'] = _compress_weights(cmp_wv1, cmp_wv2, cmp_pos_v, True)
    prm['sg_ln_g'], prm['sg_ln_b'] = sg_ln_g[:, None, :], sg_ln_b[:, None, :]
    prm['sg_b_t'] = jnp.swapaxes(sg_b, 1, 2)
    gbias = jnp.zeros((nl, 1, LANE), F32).at[:, 0, GI_LANE:GI_LANE + NH_C].set(ml_b_i)
    prm['gbias'] = gbias.at[:, 0, GF_LANE:GF_LANE + NH_C].set(ml_b_f)
    prm['conv_b'], prm['norm_g'] = ml_conv_b[:, None, :], ml_norm_g[:, None, :]
    for name, a in (('wa', w_br_a), ('wb', w_br_b), ('wc', w_br_c), ('wo', w_out), ('wp', ple_w), ('wg', ple_gate)):
        prm[name] = a.astype(BF16)
    prm['ln_g'], prm['ln_b'] = ln_g[:, None, :], ln_b[:, None, :]
    return prm


def _layer(i, x, xin, p, prm, sg_w, ml_conv_w, bsz, t_len):
    main = _matmul(xin, prm['w_main'], i, 2048, 512, "proj_main")
    kk, vt, kc, vc, small = _tail_matmul(xin, prm['w_k'], prm['w_vt'], prm['w_aux'], i, 1024)

    rows16 = t_len // STRIDE_CMP
    zk = kc.reshape(bsz, rows16, STRIDE_CMP * LANE)
    zv = vc.reshape(bsz, rows16, STRIDE_CMP * LANE)
    kcmp, vcmp_t = _compress(zk, zv, prm['pk'], prm['pv'], prm['wk1'], prm['wk2'], prm['wv1'], prm['wv2t'], i)
    ya = _nsa(main, kk, vt, small, kcmp, vcmp_t, bsz, t_len)
    yb = _sgu(main, prm['sg_ln_g'], prm['sg_ln_b'], sg_w, prm['sg_b_t'], i, 512)
    yc = _mlstm(main, vt, small, ml_conv_w, prm['conv_b'], prm['gbias'], prm['norm_g'], i, bsz, t_len)
    return _merge(x, ya, yb, yc, main, p, prm['wa'], prm['wb'], prm['wc'], prm['wo'], prm['wp'], prm['wg'],
                  prm['ln_g'], prm['ln_b'], i, 256)


def kernel(x, p, w_in, cmp_pos_k, cmp_pos_v, cmp_wk1, cmp_wk2, cmp_wv1, cmp_wv2, sg_ln_g, sg_ln_b, sg_w, sg_b,
           ml_conv_w, ml_conv_b, ml_b_i, ml_b_f, ml_norm_g, w_br_a, w_br_b, w_br_c, w_out, ple_w, ple_gate,
           ln_g, ln_b):
    bsz, t_len, d = x.shape
    assert d == D_MODEL and t_len == N_CMP_PAD * STRIDE_CMP and t_len // L_SEL == N_BLK
    assert t_len % SEQ_BLK_C == 0 and t_len % CK == 0 and w_in.shape[0] == DEPTH
    prm = _prepare(w_in, cmp_pos_k, cmp_pos_v, cmp_wk1, cmp_wk2, cmp_wv1, cmp_wv2, sg_ln_g, sg_ln_b, sg_b,
                   ml_conv_b, ml_b_i, ml_b_f, ml_norm_g, w_br_a, w_br_b, w_br_c, w_out, ple_w, ple_gate, ln_g, ln_b)
    xf = x.reshape(bsz * t_len, d)
    xin = xf
    pf = p.reshape(DEPTH, bsz * t_len, PLE_DIM)
    for i in range(DEPTH):
        xf, xin = _layer(i, xf, xin, pf, prm, sg_w, ml_conv_w, bsz, t_len)
    return xf.reshape(bsz, t_len, d)
```

```python
import jax
import jax.numpy as jnp
from jax import lax
from jax.experimental import pallas as pl
from jax.experimental.pallas import tpu as pltpu

F32 = jnp.float32
BF16 = jnp.bfloat16

D_MODEL = 1024
PLE_DIM = 256
D_A = 512
NH_A = 8
G_A = 2
HPG_A = 4
DK_A = 64
L_CMP = 32
STRIDE_CMP = 16
CMP_HID = 128
L_SEL = 64
N_SEL = 8
N_BLK = 32
WINDOW = 256
QB = 256
CK = 256
N_CMP_PAD = 128
BIG = 1e9
D_B = 512
CHUNK_B = 128
G_B = 4
D_C = 512
NH_C = 4
DH_C = 128
CONV_W = 4
CHUNK_C = 128
SEQ_BLK_C = 512
LN_EPS = 1e-5
DEPTH = 2
ALPHA = (2.0 * DEPTH) ** 0.25

LANE = 128
SUBLANE = 8
MAIN_W = 7680
COL_AQ, COL_AZ, COL_BU, COL_BV, COL_BZ, COL_CQ, COL_CK, COL_CO, COL_CZ = 6, 7, 8, 9, 10, 11, 12, 13, 14
K_W = 256
NSA_VT_W = 512
VT_W = NSA_VT_W + 512
DVX = 128 + 16
GI_LANE = 32
GF_LANE = 36
V7X_VMEM_BYTES = 64 * 1024 * 1024
VMEM_LIMIT = V7X_VMEM_BYTES * 13 // 16
NEG = -1e30
LOG2E = 1.4426950408889634


def _params(*sem):
    return pltpu.CompilerParams(dimension_semantics=sem, vmem_limit_bytes=VMEM_LIMIT)


def _nt_dot(a, b):
    return lax.dot_general(a, b, (((1,), (1,)), ((), ())), preferred_element_type=F32)


def _dot(a, b):
    return jnp.dot(a, b, preferred_element_type=F32)


def _split3(a):
    a1 = a.astype(BF16)
    r1 = a - a1.astype(F32)
    a2 = r1.astype(BF16)
    a3 = (r1 - a2.astype(F32)).astype(BF16)
    return a1, a2, a3


def _log_sigmoid(x):
    return jnp.minimum(x, 0.0) - jnp.log1p(jnp.exp(-jnp.abs(x)))


def _layer_spec(a, layer, grid_rank):
    zeros = (0,) * (a.ndim - 1)
    if grid_rank == 1:
        return pl.BlockSpec((None,) + a.shape[1:], lambda i: (layer,) + zeros)
    return pl.BlockSpec((None,) + a.shape[1:], lambda i, j: (layer,) + zeros)


def _mm_kernel(x_ref, w_ref, o_ref):
    o_ref[...] = _dot(x_ref[...].astype(BF16), w_ref[...]).astype(o_ref.dtype)


def _matmul(x, w, layer, tm, tn, name):
    m, k = x.shape
    n = w.shape[2]
    return pl.pallas_call(
        _mm_kernel,
        grid=(m // tm, n // tn),
        in_specs=[pl.BlockSpec((tm, k), lambda i, j: (i, 0)),
                  pl.BlockSpec((None, k, tn), lambda i, j: (layer, 0, j))],
        out_specs=pl.BlockSpec((tm, tn), lambda i, j: (i, j)),
        out_shape=jax.ShapeDtypeStruct((m, n), BF16),
        compiler_params=_params("parallel", "arbitrary"),
        name=name,
    )(x, w)


def _tail_kernel(x_ref, wk_ref, wvt_ref, waux_ref, ok_ref, ovt_ref, okc_ref, ovc_ref, osm_ref):
    x = x_ref[...].astype(BF16)
    ok_ref[...] = _dot(x, wk_ref[...]).astype(BF16)
    ovt_ref[...] = _nt_dot(wvt_ref[...], x).astype(BF16)
    aux = _dot(x, waux_ref[...])
    okc_ref[...] = aux[:, 0:LANE]
    ovc_ref[...] = aux[:, LANE:2 * LANE]
    osm_ref[...] = aux[:, 2 * LANE:3 * LANE]


def _tail_matmul(x, wk, wvt, waux, layer, tm):
    m, k = x.shape
    tok = lambda w: pl.BlockSpec((tm, w), lambda i: (i, 0))
    f32_out = jax.ShapeDtypeStruct((m, LANE), F32)
    return pl.pallas_call(
        _tail_kernel,
        grid=(m // tm,),
        in_specs=[tok(k), _layer_spec(wk, layer, 1), _layer_spec(wvt, layer, 1), _layer_spec(waux, layer, 1)],
        out_specs=[tok(K_W), pl.BlockSpec((VT_W, tm), lambda i: (0, i)), tok(LANE), tok(LANE), tok(LANE)],
        out_shape=[jax.ShapeDtypeStruct((m, K_W), BF16), jax.ShapeDtypeStruct((VT_W, m), BF16),
                   f32_out, f32_out, f32_out],
        compiler_params=_params("parallel"),
        name="proj_tail",
    )(x, wk, wvt, waux)


def _compress_kernel(zk_ref, zv_ref, pk_ref, pv_ref, wk1_ref, wk2_ref, wv1_ref, wv2t_ref, ko_ref, vot_ref):
    n = zk_ref.shape[0] // STRIDE_CMP

    def hidden(z_ref, p_ref, w1_ref):
        r0 = [jnp.zeros((n, CMP_HID), F32) for _ in range(G_A)]
        r1 = [jnp.zeros((n, CMP_HID), F32) for _ in range(G_A)]
        for l in range(STRIDE_CMP):
            zl = z_ref[pl.ds(l, n, stride=STRIDE_CMP), :]
            w_a = w1_ref[l * DK_A:(l + 1) * DK_A, :].astype(BF16)
            w_b = w1_ref[(STRIDE_CMP + l) * DK_A:(STRIDE_CMP + l + 1) * DK_A, :].astype(BF16)
            for g in range(G_A):
                zg = zl[:, g * DK_A:(g + 1) * DK_A]
                r0[g] = r0[g] + _dot((zg + p_ref[l:l + 1, :]).astype(BF16), w_a)
                r1[g] = r1[g] + _dot((zg + p_ref[STRIDE_CMP + l:STRIDE_CMP + l + 1, :]).astype(BF16), w_b)
        return [jax.nn.gelu(r0[g] + pltpu.roll(r1[g], n - 1, 0)).astype(BF16) for g in range(G_A)]

    hk = hidden(zk_ref, pk_ref, wk1_ref)
    wk2 = wk2_ref[...].astype(BF16)
    for g in range(G_A):
        ko_ref[0, g] = _dot(hk[g], wk2).astype(BF16)
    hv = hidden(zv_ref, pv_ref, wv1_ref)
    wv2t = wv2t_ref[...].astype(BF16)
    vot_ref[0] = jnp.concatenate([_nt_dot(wv2t, hv[g]) for g in range(G_A)], axis=0).astype(BF16)


def _compress(kc, vc, pk, pv, wk1, wk2, wv1, wv2t, layer, bsz, t_len):
    n = t_len // STRIDE_CMP
    zspec = pl.BlockSpec((t_len, LANE), lambda i: (i, 0))
    full = lambda a: _layer_spec(a, layer, 1)
    return pl.pallas_call(
        _compress_kernel,
        grid=(bsz,),
        in_specs=[zspec, zspec, full(pk), full(pv), full(wk1), full(wk2), full(wv1), full(wv2t)],
        out_specs=[pl.BlockSpec((1, G_A, n, DK_A), lambda i: (i, 0, 0, 0)),
                   pl.BlockSpec((1, G_A * DK_A, n), lambda i: (i, 0, 0))],
        out_shape=[jax.ShapeDtypeStruct((bsz, G_A, n, DK_A), BF16), jax.ShapeDtypeStruct((bsz, G_A * DK_A, n), BF16)],
        compiler_params=_params("parallel"),
        name="nsa_compress",
    )(kc, vc, pk, pv, wk1, wk2, wv1, wv2t)


def _nsa_kernel(q_ref, z_ref, g_ref, kc_ref, vct_ref, k_ref, vt_ref, o_ref,
                s_sc, acc_sc, ow_sc, bw_sc, sel_sc, idx_sc):
    qb = pl.program_id(1)
    t0 = qb * QB
    cols = HPG_A * QB
    n_qtile = QB // LANE
    n_wchunk = (WINDOW + QB) // LANE
    n_wslot = n_wchunk + 1
    k_i = lax.broadcasted_iota(jnp.int32, (LANE, 1), 0)
    t_i = lax.broadcasted_iota(jnp.int32, (1, QB), 1)
    tk = (t_i - k_i).astype(F32)
    ones_row = jnp.where(lax.broadcasted_iota(jnp.int32, (LANE, CK), 0) == DK_A, 1.0, 0.0).astype(BF16)
    slopes = [[LOG2E * 2.0 ** -(g * HPG_A + h + 1) for h in range(HPG_A)] for g in range(G_A)]
    hs = [slice(h * QB, (h + 1) * QB) for h in range(HPG_A)]

    def biased(s, bias_of_head):
        return jnp.concatenate([s[:, hs[h]] + bias_of_head(h) for h in range(HPG_A)], axis=1)

    @pl.when(qb == 0)
    def _():
        for g in range(G_A):
            for h in range(HPG_A):
                for j in range(n_wchunk):
                    dist = tk + float(WINDOW - j * LANE)
                    ok = (dist >= 0.0) & (dist < WINDOW)
                    bw_sc[g, h * n_wslot + j] = jnp.where(ok, -slopes[g][h] * dist, NEG)
                bw_sc[g, h * n_wslot + n_wchunk] = jnp.full((LANE, QB), NEG, F32)

    gsig = jax.nn.sigmoid(g_ref[...])
    gates_t = jnp.concatenate([jnp.transpose(gsig[r * LANE:(r + 1) * LANE]) for r in range(n_qtile)], axis=1)

    jb = lax.broadcasted_iota(jnp.int32, (N_BLK, QB), 0)
    cur = (t0 + lax.broadcasted_iota(jnp.int32, (N_BLK, QB), 1)) >> 6
    forced = (jb == 0) | (jb == cur) | (jb == cur - 1)
    allowed = jb <= cur
    ov_j = lax.broadcasted_iota(jnp.int32, (N_BLK, N_CMP_PAD), 0) * L_SEL
    ov_n = lax.broadcasted_iota(jnp.int32, (N_BLK, N_CMP_PAD), 1) * STRIDE_CMP
    ov_t = jnp.where((ov_n < ov_j + L_SEL) & (ov_n + L_CMP > ov_j), 1.0, 0.0).astype(BF16)
    jb8 = lax.broadcasted_iota(jnp.int32, (SUBLANE, QB), 0)

    d_cmp = (t0 - (L_CMP - 1)).astype(F32) + (t_i - STRIDE_CMP * k_i).astype(F32)
    valid_cmp = (d_cmp >= 0.0) & (k_i < N_CMP_PAD - 1)

    gls = [slice(g * DK_A, (g + 1) * DK_A) for g in range(G_A)]
    q4s, o_cmps = [], []
    for g in range(G_A):
        q4 = jnp.concatenate(
            [q_ref[:, (g * HPG_A + h) * DK_A:(g * HPG_A + h + 1) * DK_A] for h in range(HPG_A)], axis=0)
        q4 = (q4.astype(F32) * (LOG2E * DK_A ** -0.5)).astype(BF16)
        q4s.append(q4)

        s = biased(_nt_dot(kc_ref[0, g], q4),
                   lambda h: jnp.where(valid_cmp, -slopes[g][h] * d_cmp, -jnp.inf))
        mx = jnp.max(s, axis=0, keepdims=True)
        mx = jnp.where(mx > -jnp.inf, mx, 0.0)
        e = jnp.exp2(s - mx)
        p = e * (1.0 / jnp.maximum(jnp.sum(e, axis=0, keepdims=True), 1e-30))
        o_cmps.append(_dot(vct_ref[0][gls[g], :], p.astype(BF16)))

        psum = p[:, hs[0]] + p[:, hs[1]] + p[:, hs[2]] + p[:, hs[3]]
        p1, p2, p3 = _split3(psum)
        imp = _dot(ov_t, p1) + _dot(ov_t, p2) + _dot(ov_t, p3)
        score = jnp.where(allowed, jnp.where(forced, BIG, imp), -BIG)
        tiles = [score[r * SUBLANE:(r + 1) * SUBLANE] for r in range(N_BLK // SUBLANE)]
        cnt = [jnp.zeros((SUBLANE, QB), F32) for _ in tiles]
        for j in range(N_BLK):
            sj = score[j:j + 1, :]
            for r, tile in enumerate(tiles):
                lo = r * SUBLANE
                if j >= lo + SUBLANE:
                    beats = sj > tile
                elif j < lo:
                    beats = sj >= tile
                else:
                    beats = (sj > tile) | ((sj >= tile) & (jb8 > j - lo))
                cnt[r] = cnt[r] + jnp.where(beats, 1.0, 0.0)
        sel_sc[g] = jnp.where((jnp.concatenate(cnt, axis=0) < N_SEL) & allowed, 1.0, 0.0)

        s_w, v_w = [], []
        for j in range(n_wchunk):
            c = qb * n_qtile - WINDOW // LANE + j
            k0 = pl.multiple_of(jnp.maximum(c, 0) * LANE, LANE)
            slot = jnp.where(c >= 0, j, n_wchunk)
            sj = _nt_dot(k_ref[pl.ds(k0, LANE), LANE + g * DK_A:LANE + (g + 1) * DK_A], q4)
            s_w.append(biased(sj, lambda h: bw_sc[g, h * n_wslot + slot]))
            v_w.append(vt_ref[G_A * LANE + g * LANE:G_A * LANE + (g + 1) * LANE, pl.ds(k0, LANE)]
                       + ones_row[:, :LANE])
        mx = s_w[0]
        for j in range(1, n_wchunk):
            mx = jnp.maximum(mx, s_w[j])
        mx = jnp.max(mx, axis=0, keepdims=True)
        o_win = _dot(v_w[0], jnp.exp2((s_w[0] - mx).astype(BF16)))
        for j in range(1, n_wchunk):
            o_win = o_win + _dot(v_w[j], jnp.exp2((s_w[j] - mx).astype(BF16)))
        ow_sc[g] = o_win

    chosen_any = jnp.maximum(sel_sc[0], sel_sc[1])
    blk_per_chunk = CK // L_SEL
    n_causal = ((qb + 1) * QB - 1) // CK + 1
    n_chunk = jnp.int32(0)
    for c in range(s_sc.shape[1]):
        need = (jnp.max(chosen_any[c * blk_per_chunk:(c + 1) * blk_per_chunk, :]) > 0.5) & (c < n_causal)

        @pl.when(need)
        def _():
            idx_sc[n_chunk] = c

        n_chunk = n_chunk + jnp.where(need, 1, 0)

    def score_chunk(i, m_runs):
        c = idx_sc[i]
        k0 = pl.multiple_of(c * CK, CK)
        out = []
        for g in range(G_A):
            sc = _nt_dot(k_ref[pl.ds(k0, CK), gls[g]], q4s[g])
            halves = []
            for u in range(CK // LANE):
                dist = tk + (t0 - k0 - u * LANE).astype(F32)
                blk0 = (CK // L_SEL) * c + (LANE // L_SEL) * u
                chosen = jnp.concatenate(
                    [jnp.broadcast_to(sel_sc[g, pl.ds(blk0 + i, 1), :], (L_SEL, QB))
                     for i in range(LANE // L_SEL)], axis=0)
                ok = (chosen > 0.5) & (dist >= 0.0)
                far = jnp.where(ok, dist, -NEG)
                halves.append(biased(sc[u * LANE:(u + 1) * LANE], lambda h: far * -slopes[g][h]))
            sc = jnp.concatenate(halves, axis=0)
            s_sc[g, i] = sc
            out.append(jnp.maximum(m_runs[g], jnp.max(sc.reshape(CK // SUBLANE, SUBLANE, cols), axis=0)))
        return tuple(out)

    m_runs = lax.fori_loop(0, n_chunk, score_chunk, (jnp.full((SUBLANE, cols), NEG, F32),) * G_A)
    m_rows = [jnp.max(m_runs[g], axis=0, keepdims=True) for g in range(G_A)]
    acc_sc[...] = jnp.zeros(acc_sc.shape, F32)

    def value_chunk(i, carry):
        k0 = pl.multiple_of(idx_sc[i] * CK, CK)
        for g in range(G_A):
            pc = jnp.exp2((s_sc[g, i] - m_rows[g]).astype(BF16))
            acc_sc[g] += _dot(vt_ref[g * LANE:(g + 1) * LANE, pl.ds(k0, CK)] + ones_row, pc)
        return carry

    lax.fori_loop(0, n_chunk, value_chunk, 0)

    pieces = []
    for g in range(G_A):
        o_sel, o_win = acc_sc[g], ow_sc[g]
        for h in range(HPG_A):
            col = g * HPG_A + h
            os_h, ow_h = o_sel[:, hs[h]], o_win[:, hs[h]]
            w_sel = gates_t[NH_A + col:NH_A + col + 1, :] / jnp.maximum(os_h[DK_A:DK_A + 1, :], 1e-30)
            w_win = gates_t[2 * NH_A + col:2 * NH_A + col + 1, :] / jnp.maximum(ow_h[DK_A:DK_A + 1, :], 1e-30)
            pieces.append(gates_t[col:col + 1, :] * o_cmps[g][:, hs[h]] + w_sel * os_h[:DK_A] + w_win * ow_h[:DK_A])

    out_t = jnp.concatenate(pieces, axis=0)
    out = jnp.concatenate(
        [jnp.concatenate([jnp.transpose(out_t[i * LANE:(i + 1) * LANE, r * LANE:(r + 1) * LANE])
                          for i in range(D_A // LANE)], axis=1) for r in range(n_qtile)], axis=0)
    o_ref[...] = (out * jax.nn.silu(z_ref[...].astype(F32))).astype(BF16)


def _nsa(main, kk, vt, small, kcmp, vcmp_t, bsz, t_len):
    nqb = t_len // QB
    cols = HPG_A * QB
    n_wslot = (WINDOW + QB) // LANE + 1
    return pl.pallas_call(
        _nsa_kernel,
        grid=(bsz, nqb),
        in_specs=[pl.BlockSpec((QB, D_A), lambda b, i: (b * nqb + i, COL_AQ)),
                  pl.BlockSpec((QB, D_A), lambda b, i: (b * nqb + i, COL_AZ)),
                  pl.BlockSpec((QB, LANE), lambda b, i: (b * nqb + i, 0)),
                  pl.BlockSpec((1, G_A, N_CMP_PAD, DK_A), lambda b, i: (b, 0, 0, 0)),
                  pl.BlockSpec((1, G_A * DK_A, N_CMP_PAD), lambda b, i: (b, 0, 0)),
                  pl.BlockSpec((t_len, K_W), lambda b, i: (b, 0)),
                  pl.BlockSpec((NSA_VT_W, t_len), lambda b, i: (0, b))],
        out_specs=pl.BlockSpec((QB, D_A), lambda b, i: (b * nqb + i, 0)),
        out_shape=jax.ShapeDtypeStruct((bsz * t_len, D_A), BF16),
        scratch_shapes=[pltpu.VMEM((G_A, t_len // CK, CK, cols), F32), pltpu.VMEM((G_A, LANE, cols), F32),
                        pltpu.VMEM((G_A, LANE, cols), F32),
                        pltpu.VMEM((G_A, HPG_A * n_wslot, LANE, QB), F32), pltpu.VMEM((G_A, N_BLK, QB), F32),
                        pltpu.SMEM((t_len // CK,), jnp.int32)],
        compiler_params=_params("parallel", "arbitrary"),
        name="nsa_attention",
    )(main, main, small, kcmp, vcmp_t, kk, vt)


def _sgu_kernel(u_ref, v_ref, z_ref, lng_ref, lnb_ref, w_ref, b_ref, o_ref):
    u = jax.nn.gelu(u_ref[...].astype(F32))
    v = jax.nn.gelu(v_ref[...].astype(F32))
    mu = jnp.mean(v, axis=-1, keepdims=True)
    var = jnp.mean(jnp.square(v - mu), axis=-1, keepdims=True)
    vn = ((v - mu) * lax.rsqrt(var + LN_EPS) * lng_ref[...] + lnb_ref[...]).astype(BF16)
    gate = u * jax.nn.silu(z_ref[...].astype(F32))
    ti = lax.broadcasted_iota(jnp.int32, (CHUNK_B, CHUNK_B), 0)
    si = lax.broadcasted_iota(jnp.int32, (CHUNK_B, CHUNK_B), 1)
    n_chunks = u_ref.shape[0] // CHUNK_B
    for g in range(G_B):
        w = jnp.where(si <= ti, w_ref[g], 0.0).astype(BF16)
        bias = b_ref[:, g:g + 1]
        ls = slice(g * LANE, (g + 1) * LANE)
        for c in range(n_chunks):
            rs = slice(c * CHUNK_B, (c + 1) * CHUNK_B)
            o_ref[rs, ls] = (gate[rs, ls] * (_dot(w, vn[rs, ls]) + bias)).astype(BF16)


def _sgu(main, lng, lnb, w, b_t, layer, tm):
    n = main.shape[0]
    blk = lambda col: pl.BlockSpec((tm, D_B), lambda i: (i, col))
    full = lambda a: _layer_spec(a, layer, 1)
    return pl.pallas_call(
        _sgu_kernel,
        grid=(n // tm,),
        in_specs=[blk(COL_BU), blk(COL_BV), blk(COL_BZ), full(lng), full(lnb), full(w), full(b_t)],
        out_specs=pl.BlockSpec((tm, D_B), lambda i: (i, 0)),
        out_shape=jax.ShapeDtypeStruct((n, D_B), BF16),
        compiler_params=_params("parallel"),
        name="spatial_gating",
    )(main, main, main, lng, lnb, w, b_t)


def _mlstm_kernel(q_ref, k_ref, vt_ref, o_ref, z_ref, g_ref, cw_ref, cb_ref, gb_ref, ng_ref, y_ref,
                  qc_sc, kc_sc, halo_sc, st_sc, m_sc):
    sb = pl.program_id(1)
    blk = q_ref.shape[0]

    @pl.when(sb == 0)
    def _():
        halo_sc[...] = jnp.zeros(halo_sc.shape, F32)
        st_sc[...] = jnp.zeros(st_sc.shape, F32)
        m_sc[...] = jnp.zeros(m_sc.shape, F32)

    r8 = lax.broadcasted_iota(jnp.int32, (SUBLANE, 1), 0)

    def conv_silu(x_ref, which, dst_ref, scale):
        x = x_ref[...].astype(F32)
        prev = halo_sc[which]
        w = cw_ref[:, which * D_C:(which + 1) * D_C]
        acc = x * w[CONV_W - 1:CONV_W, :] + cb_ref[:, which * D_C:(which + 1) * D_C]
        for j in range(1, CONV_W):
            rolled = pltpu.roll(x, j, 0)
            head = jnp.where(r8 < j, pltpu.roll(prev, j, 0), rolled[0:SUBLANE])
            shifted = jnp.concatenate([head, rolled[SUBLANE:]], axis=0)
            acc = acc + shifted * w[CONV_W - 1 - j:CONV_W - j, :]
        halo_sc[which] = x[blk - SUBLANE:blk]
        dst_ref[...] = (jax.nn.silu(acc) * scale).astype(BF16)

    conv_silu(q_ref, 0, qc_sc, 1.0)
    conv_silu(k_ref, 1, kc_sc, DH_C ** -0.5)

    si = lax.broadcasted_iota(jnp.int32, (CHUNK_C, CHUNK_C), 0)
    ti = lax.broadcasted_iota(jnp.int32, (CHUNK_C, CHUNK_C), 1)
    causal = si <= ti
    tri_u = jnp.where(causal, 1.0, 0.0).astype(BF16)
    ones_rows = jnp.where(lax.broadcasted_iota(jnp.int32, (DVX - DH_C, CHUNK_C), 0) == 0, 1.0, 0.0).astype(BF16)
    ng = ng_ref[...]
    gbias = gb_ref[...]

    for c in range(blk // CHUNK_C):
        rs = slice(c * CHUNK_C, (c + 1) * CHUNK_C)
        gate_rows = jnp.transpose(g_ref[rs, :] + gbias)[GI_LANE:GI_LANE + 2 * NH_C]
        f1, f2, f3 = _split3(_log_sigmoid(gate_rows))
        bc_rows = _dot(f1, tri_u) + _dot(f2, tri_u) + _dot(f3, tri_u)
        g_rows = gate_rows[0:NH_C] - bc_rows[NH_C:2 * NH_C]
        g_cols = jnp.transpose(jnp.concatenate([g_rows, jnp.zeros((CHUNK_C - NH_C, CHUNK_C), F32)], axis=0))
        for h in range(NH_C):
            ls = slice(h * DH_C, (h + 1) * DH_C)
            q = qc_sc[rs, ls]
            k = kc_sc[rs, ls]
            vext = jnp.concatenate([vt_ref[ls, rs], ones_rows], axis=0)
            g_col = g_cols[:, h:h + 1]
            g_row = g_rows[h:h + 1, :]
            bc_row = bc_rows[NH_C + h:NH_C + h + 1, :]
            b_last = bc_row[:, CHUNK_C - 1:CHUNK_C]
            m_prev = m_sc[h][:, 0:1]
            st = st_sc[h]
            m_loc = b_last + jnp.max(g_row, axis=1, keepdims=True)
            e_end = jnp.exp(b_last + g_row - m_loc)

            gmat = jnp.where(causal, g_col, -jnp.inf)
            mg = jnp.maximum(m_prev, jnp.max(gmat, axis=0, keepdims=True))
            s_t = _nt_dot(k, q) * jnp.exp(gmat - mg)
            e_int = jnp.exp(m_prev - mg)
            lhs = jnp.concatenate([s_t, jnp.transpose(q.astype(F32)) * e_int], axis=0).astype(BF16)
            rhs = jnp.concatenate([vext, st.astype(BF16)], axis=1)
            both = _dot(rhs, lhs)
            den = both[DH_C:DH_C + 1, :]
            hval = both[:DH_C] * (1.0 / jnp.maximum(jnp.abs(den), jnp.exp(-(bc_row + mg))))
            mu = jnp.mean(hval, axis=0, keepdims=True)
            var = jnp.mean(jnp.square(hval - mu), axis=0, keepdims=True)
            hn = jnp.transpose((hval - mu) * lax.rsqrt(var + LN_EPS))
            o_gate = o_ref[rs, ls].astype(F32)
            z_gate = z_ref[rs, ls].astype(F32)
            gated = hn * ng[:, ls] * z_gate / ((1.0 + jnp.exp(-o_gate)) * (1.0 + jnp.exp(-z_gate)))
            y_ref[rs, ls] = gated.astype(BF16)

            loc = _dot((vext.astype(F32) * e_end).astype(BF16), k)
            m_new = jnp.maximum(b_last + m_prev, m_loc)
            st_sc[h] = jnp.exp(b_last + m_prev - m_new) * st + jnp.exp(m_loc - m_new) * loc
            m_sc[h] = jnp.broadcast_to(m_new, (1, LANE))


def _mlstm(main, vt, small, conv_w, conv_b, gbias, norm_g, layer, bsz, t_len):
    nsb = t_len // SEQ_BLK_C
    blk = lambda col: pl.BlockSpec((SEQ_BLK_C, D_C), lambda b, s: (b * nsb + s, col))
    full = lambda a: _layer_spec(a, layer, 2)
    return pl.pallas_call(
        _mlstm_kernel,
        grid=(bsz, nsb),
        in_specs=[blk(COL_CQ), blk(COL_CK),
                  pl.BlockSpec((D_C, SEQ_BLK_C), lambda b, s: (NSA_VT_W // D_C, b * nsb + s)),
                  blk(COL_CO), blk(COL_CZ),
                  pl.BlockSpec((SEQ_BLK_C, LANE), lambda b, s: (b * nsb + s, 0)),
                  full(conv_w), full(conv_b), full(gbias), full(norm_g)],
        out_specs=pl.BlockSpec((SEQ_BLK_C, D_C), lambda b, s: (b * nsb + s, 0)),
        out_shape=jax.ShapeDtypeStruct((bsz * t_len, D_C), BF16),
        scratch_shapes=[pltpu.VMEM((SEQ_BLK_C, D_C), BF16), pltpu.VMEM((SEQ_BLK_C, D_C), BF16),
                        pltpu.VMEM((2, SUBLANE, D_C), F32), pltpu.VMEM((NH_C, DVX, DH_C), F32),
                        pltpu.VMEM((NH_C, 1, LANE), F32)],
        compiler_params=_params("parallel", "arbitrary"),
        name="mlstm",
    )(main, main, vt, main, main, small, conv_w, conv_b, gbias, norm_g)


def _merge_kernel(x_ref, ya_ref, yb_ref, yc_ref, g0_ref, g1_ref, g2_ref, p_ref,
                  wa_ref, wb_ref, wc_ref, wo_ref, wp_ref, wg_ref, lng_ref, lnb_ref, o_ref, ob_ref):
    merged = (jax.nn.sigmoid(g0_ref[...].astype(F32)) * _dot(ya_ref[...], wa_ref[...])
              + jax.nn.sigmoid(g1_ref[...].astype(F32)) * _dot(yb_ref[...], wb_ref[...])
              + jax.nn.sigmoid(g2_ref[...].astype(F32)) * _dot(yc_ref[...], wc_ref[...]))
    r = ALPHA * x_ref[...] + _dot(merged.astype(BF16), wo_ref[...])
    r = r + jax.nn.sigmoid(_dot(r.astype(BF16), wg_ref[...])) * _dot(p_ref[...].astype(BF16), wp_ref[...])
    mu = jnp.mean(r, axis=-1, keepdims=True)
    var = jnp.mean(jnp.square(r - mu), axis=-1, keepdims=True)
    y = (r - mu) * lax.rsqrt(var + LN_EPS) * lng_ref[...] + lnb_ref[...]
    o_ref[...] = y
    ob_ref[...] = y.astype(BF16)


def _merge(x, ya, yb, yc, main, p, wa, wb, wc, wo, wp, wg, lng, lnb, layer, tm):
    n = x.shape[0]
    row = lambda w: pl.BlockSpec((tm, w), lambda i: (i, 0))
    gate = lambda j: pl.BlockSpec((tm, D_MODEL), lambda i: (i, j))
    full = lambda a: _layer_spec(a, layer, 1)
    return pl.pallas_call(
        _merge_kernel,
        grid=(n // tm,),
        in_specs=[row(D_MODEL), row(D_A), row(D_B), row(D_C), gate(0), gate(1), gate(2),
                  pl.BlockSpec((None, tm, PLE_DIM), lambda i: (layer, i, 0)),
                  full(wa), full(wb), full(wc), full(wo), full(wp), full(wg), full(lng), full(lnb)],
        out_specs=[row(D_MODEL), row(D_MODEL)],
        out_shape=[jax.ShapeDtypeStruct((n, D_MODEL), F32), jax.ShapeDtypeStruct((n, D_MODEL), BF16)],
        compiler_params=_params("parallel"),
        name="merge",
    )(x, ya, yb, yc, main, main, main, p, wa, wb, wc, wo, wp, wg, lng, lnb)


def _pad_groups(w):
    nl, d = w.shape[0], w.shape[1]
    return jnp.pad(w.reshape(nl, d, G_A, DK_A), ((0, 0), (0, 0), (0, 0), (0, LANE - DK_A))).reshape(nl, d, G_A * LANE)


def _prepare(w_in, cmp_wv2, sg_ln_g, sg_ln_b, sg_b,
             ml_conv_b, ml_b_i, ml_b_f, ml_norm_g, w_br_a, w_br_b, w_br_c, w_out, ple_w, ple_gate, ln_g, ln_b):
    nl = w_in.shape[0]
    w = w_in
    prm = {}
    prm['w_main'] = jnp.concatenate([w[:, :, 5920:8992], w[:, :, 0:512], w[:, :, 1304:4376], w[:, :, 4896:5920]],
                                    axis=2).astype(BF16)
    prm['w_k'] = jnp.concatenate([w[:, :, 768:896], w[:, :, 1024:1152]], axis=2).astype(BF16)
    prm['w_vt'] = jnp.swapaxes(jnp.concatenate([_pad_groups(w[:, :, 896:1024]), _pad_groups(w[:, :, 1152:1280]),
                                                w[:, :, 4376:4888]], axis=2), 1, 2).astype(BF16)
    prm['w_aux'] = jnp.concatenate([w[:, :, 512:768], w[:, :, 1280:1304], jnp.zeros((nl, D_MODEL, 8), F32),
                                    w[:, :, 4888:4896], jnp.zeros((nl, D_MODEL, 88), F32)], axis=2).astype(BF16)
    prm['wv2t'] = jnp.swapaxes(cmp_wv2, 1, 2)
    prm['sg_ln_g'], prm['sg_ln_b'] = sg_ln_g[:, None, :], sg_ln_b[:, None, :]
    prm['sg_b_t'] = jnp.swapaxes(sg_b, 1, 2)
    gbias = jnp.zeros((nl, 1, LANE), F32).at[:, 0, GI_LANE:GI_LANE + NH_C].set(ml_b_i)
    prm['gbias'] = gbias.at[:, 0, GF_LANE:GF_LANE + NH_C].set(ml_b_f)
    prm['conv_b'], prm['norm_g'] = ml_conv_b[:, None, :], ml_norm_g[:, None, :]
    for name, a in (('wa', w_br_a), ('wb', w_br_b), ('wc', w_br_c), ('wo', w_out), ('wp', ple_w), ('wg', ple_gate)):
        prm[name] = a.astype(BF16)
    prm['ln_g'], prm['ln_b'] = ln_g[:, None, :], ln_b[:, None, :]
    return prm


def _layer(i, x, xin, p, prm, cmp, sg_w, ml_conv_w, bsz, t_len):
    main = _matmul(xin, prm['w_main'], i, 2048, 512, "proj_main")
    kk, vt, kc, vc, small = _tail_matmul(xin, prm['w_k'], prm['w_vt'], prm['w_aux'], i, 1024)

    pos_k, pos_v, wk1, wk2, wv1 = cmp
    kcmp, vcmp_t = _compress(kc, vc, pos_k, pos_v, wk1, wk2, wv1, prm['wv2t'], i, bsz, t_len)
    ya = _nsa(main, kk, vt, small, kcmp, vcmp_t, bsz, t_len)
    yb = _sgu(main, prm['sg_ln_g'], prm['sg_ln_b'], sg_w, prm['sg_b_t'], i, 512)
    yc = _mlstm(main, vt, small, ml_conv_w, prm['conv_b'], prm['gbias'], prm['norm_g'], i, bsz, t_len)
    return _merge(x, ya, yb, yc, main, p, prm['wa'], prm['wb'], prm['wc'], prm['wo'], prm['wp'], prm['wg'],
                  prm['ln_g'], prm['ln_b'], i, 256)


def kernel(x, p, w_in, cmp_pos_k, cmp_pos_v, cmp_wk1, cmp_wk2, cmp_wv1, cmp_wv2, sg_ln_g, sg_ln_b, sg_w, sg_b,
           ml_conv_w, ml_conv_b, ml_b_i, ml_b_f, ml_norm_g, w_br_a, w_br_b, w_br_c, w_out, ple_w, ple_gate,
           ln_g, ln_b):
    bsz, t_len, d = x.shape
    assert d == D_MODEL and t_len == N_CMP_PAD * STRIDE_CMP and t_len // L_SEL == N_BLK
    assert t_len % SEQ_BLK_C == 0 and t_len % CK == 0 and w_in.shape[0] == DEPTH
    prm = _prepare(w_in, cmp_wv2, sg_ln_g, sg_ln_b, sg_b,
                   ml_conv_b, ml_b_i, ml_b_f, ml_norm_g, w_br_a, w_br_b, w_br_c, w_out, ple_w, ple_gate, ln_g, ln_b)
    xf = x.reshape(bsz * t_len, d)
    xin = xf
    pf = p.reshape(DEPTH, bsz * t_len, PLE_DIM)
    for i in range(DEPTH):
        xf, xin = _layer(i, xf, xin, pf, prm, (cmp_pos_k, cmp_pos_v, cmp_wk1, cmp_wk2, cmp_wv1), sg_w, ml_conv_w,
                         bsz, t_len)
    return xf.reshape(bsz, t_len, d)
```

```python
import jax
import jax.numpy as jnp
from jax import lax
from jax.experimental import pallas as pl
from jax.experimental.pallas import tpu as pltpu

F32 = jnp.float32
BF16 = jnp.bfloat16

D_MODEL = 1024
PLE_DIM = 256
D_A = 512
NH_A = 8
G_A = 2
HPG_A = 4
DK_A = 64
L_CMP = 32
STRIDE_CMP = 16
CMP_HID = 128
L_SEL = 64
N_SEL = 8
N_BLK = 32
WINDOW = 256
QB = 256
CK = 256
N_CMP_PAD = 128
BIG = 1e9
D_B = 512
CHUNK_B = 128
G_B = 4
D_C = 512
NH_C = 4
DH_C = 128
CONV_W = 4
CHUNK_C = 128
SEQ_BLK_C = 512
LN_EPS = 1e-5
DEPTH = 2
ALPHA = (2.0 * DEPTH) ** 0.25

LANE = 128
SUBLANE = 8
_SEG_SIZES = (('a_q', D_A), ('a_kc', G_A * DK_A), ('a_vc', G_A * DK_A), ('a_ks', G_A * DK_A), ('a_vs', G_A * DK_A),
              ('a_kw', G_A * DK_A), ('a_vw', G_A * DK_A), ('a_g', 3 * NH_A), ('a_z', D_A),
              ('b_u', D_B), ('b_v', D_B), ('b_z', D_B),
              ('c_q', D_C), ('c_k', D_C), ('c_v', D_C), ('c_if', 2 * NH_C), ('c_o', D_C), ('c_z', D_C),
              ('m_g', 3 * D_MODEL))
SEG = {}
for _name, _size in _SEG_SIZES:
    _lo = sum(s for _, s in _SEG_SIZES[:len(SEG)])
    SEG[_name] = (_lo, _lo + _size)
MAIN_W = 7680
COL_AQ, COL_AZ, COL_BU, COL_BV, COL_BZ, COL_CQ, COL_CK, COL_CO, COL_CZ = 6, 7, 8, 9, 10, 11, 12, 13, 14
K_W = 256
NSA_VT_W = 512
VT_W = NSA_VT_W + 512
DVX = 128 + 16
GI_LANE = 32
GF_LANE = 36
V7X_VMEM_BYTES = 64 * 1024 * 1024
VMEM_LIMIT = V7X_VMEM_BYTES * 13 // 16
NEG = -1e30
LOG2E = 1.4426950408889634


def _params(*sem):
    return pltpu.CompilerParams(dimension_semantics=sem, vmem_limit_bytes=VMEM_LIMIT)


def _nt_dot(a, b):
    return lax.dot_general(a, b, (((1,), (1,)), ((), ())), preferred_element_type=F32)


def _dot(a, b):
    return jnp.dot(a, b, preferred_element_type=F32)


def _split3(a):
    a1 = a.astype(BF16)
    r1 = a - a1.astype(F32)
    a2 = r1.astype(BF16)
    a3 = (r1 - a2.astype(F32)).astype(BF16)
    return a1, a2, a3


def _log_sigmoid(x):
    return jnp.minimum(x, 0.0) - jnp.log1p(jnp.exp(-jnp.abs(x)))


def _layer_spec(a, layer, grid_rank):
    zeros = (0,) * (a.ndim - 1)
    if grid_rank == 1:
        return pl.BlockSpec((None,) + a.shape[1:], lambda i: (layer,) + zeros)
    return pl.BlockSpec((None,) + a.shape[1:], lambda i, j: (layer,) + zeros)


def _mm_kernel(x_ref, w_ref, o_ref):
    o_ref[...] = _dot(x_ref[...].astype(BF16), w_ref[...]).astype(o_ref.dtype)


def _matmul(x, w, layer, tm, tn, name):
    m, k = x.shape
    n = w.shape[2]
    return pl.pallas_call(
        _mm_kernel,
        grid=(m // tm, n // tn),
        in_specs=[pl.BlockSpec((tm, k), lambda i, j: (i, 0)),
                  pl.BlockSpec((None, k, tn), lambda i, j: (layer, 0, j))],
        out_specs=pl.BlockSpec((tm, tn), lambda i, j: (i, j)),
        out_shape=jax.ShapeDtypeStruct((m, n), BF16),
        compiler_params=_params("parallel", "arbitrary"),
        name=name,
    )(x, w)


def _tail_kernel(x_ref, wk_ref, wvt_ref, waux_ref, ok_ref, ovt_ref, okc_ref, ovc_ref, osm_ref):
    x = x_ref[...].astype(BF16)
    ok_ref[...] = _dot(x, wk_ref[...]).astype(BF16)
    ovt_ref[...] = _nt_dot(wvt_ref[...], x).astype(BF16)
    aux = _dot(x, waux_ref[...])
    okc_ref[...] = aux[:, 0:LANE]
    ovc_ref[...] = aux[:, LANE:2 * LANE]
    osm_ref[...] = aux[:, 2 * LANE:3 * LANE]


def _tail_matmul(x, wk, wvt, waux, layer, tm):
    m, k = x.shape
    tok = lambda w: pl.BlockSpec((tm, w), lambda i: (i, 0))
    f32_out = jax.ShapeDtypeStruct((m, LANE), F32)
    return pl.pallas_call(
        _tail_kernel,
        grid=(m // tm,),
        in_specs=[tok(k), _layer_spec(wk, layer, 1), _layer_spec(wvt, layer, 1), _layer_spec(waux, layer, 1)],
        out_specs=[tok(K_W), pl.BlockSpec((VT_W, tm), lambda i: (0, i)), tok(LANE), tok(LANE), tok(LANE)],
        out_shape=[jax.ShapeDtypeStruct((m, K_W), BF16), jax.ShapeDtypeStruct((VT_W, m), BF16),
                   f32_out, f32_out, f32_out],
        compiler_params=_params("parallel"),
        name="proj_tail",
    )(x, wk, wvt, waux)


def _compress_kernel(zk_ref, zv_ref, pk_ref, pv_ref, wk1_ref, wk2_ref, wv1_ref, wv2t_ref, ko_ref, vot_ref):
    n = zk_ref.shape[0] // STRIDE_CMP

    def hidden(z_ref, p_ref, w1_ref):
        r0 = [jnp.zeros((n, CMP_HID), F32) for _ in range(G_A)]
        r1 = [jnp.zeros((n, CMP_HID), F32) for _ in range(G_A)]
        for l in range(STRIDE_CMP):
            zl = z_ref[pl.ds(l, n, stride=STRIDE_CMP), :]
            w_a = w1_ref[l * DK_A:(l + 1) * DK_A, :].astype(BF16)
            w_b = w1_ref[(STRIDE_CMP + l) * DK_A:(STRIDE_CMP + l + 1) * DK_A, :].astype(BF16)
            for g in range(G_A):
                zg = zl[:, g * DK_A:(g + 1) * DK_A]
                r0[g] = r0[g] + _dot((zg + p_ref[l:l + 1, :]).astype(BF16), w_a)
                r1[g] = r1[g] + _dot((zg + p_ref[STRIDE_CMP + l:STRIDE_CMP + l + 1, :]).astype(BF16), w_b)
        return [jax.nn.gelu(r0[g] + pltpu.roll(r1[g], n - 1, 0)).astype(BF16) for g in range(G_A)]

    hk = hidden(zk_ref, pk_ref, wk1_ref)
    wk2 = wk2_ref[...].astype(BF16)
    for g in range(G_A):
        ko_ref[0, g] = _dot(hk[g], wk2).astype(BF16)
    hv = hidden(zv_ref, pv_ref, wv1_ref)
    wv2t = wv2t_ref[...].astype(BF16)
    vot_ref[0] = jnp.concatenate([_nt_dot(wv2t, hv[g]) for g in range(G_A)], axis=0).astype(BF16)


def _compress(kc, vc, pk, pv, wk1, wk2, wv1, wv2t, layer, bsz, t_len):
    n = t_len // STRIDE_CMP
    zspec = pl.BlockSpec((t_len, LANE), lambda i: (i, 0))
    full = lambda a: _layer_spec(a, layer, 1)
    return pl.pallas_call(
        _compress_kernel,
        grid=(bsz,),
        in_specs=[zspec, zspec, full(pk), full(pv), full(wk1), full(wk2), full(wv1), full(wv2t)],
        out_specs=[pl.BlockSpec((1, G_A, n, DK_A), lambda i: (i, 0, 0, 0)),
                   pl.BlockSpec((1, G_A * DK_A, n), lambda i: (i, 0, 0))],
        out_shape=[jax.ShapeDtypeStruct((bsz, G_A, n, DK_A), BF16), jax.ShapeDtypeStruct((bsz, G_A * DK_A, n), BF16)],
        compiler_params=_params("parallel"),
        name="nsa_compress",
    )(kc, vc, pk, pv, wk1, wk2, wv1, wv2t)


def _nsa_kernel(q_ref, z_ref, g_ref, kc_ref, vct_ref, k_ref, vt_ref, o_ref,
                s_sc, acc_sc, ow_sc, bw_sc, sel_sc, idx_sc):
    qb = pl.program_id(1)
    t0 = qb * QB
    cols = HPG_A * QB
    n_qtile = QB // LANE
    n_wchunk = (WINDOW + QB) // LANE
    n_wslot = n_wchunk + 1
    k_i = lax.broadcasted_iota(jnp.int32, (LANE, 1), 0)
    t_i = lax.broadcasted_iota(jnp.int32, (1, QB), 1)
    tk = (t_i - k_i).astype(F32)
    ones_row = jnp.where(lax.broadcasted_iota(jnp.int32, (LANE, CK), 0) == DK_A, 1.0, 0.0).astype(BF16)
    slopes = [[LOG2E * 2.0 ** -(g * HPG_A + h + 1) for h in range(HPG_A)] for g in range(G_A)]
    hs = [slice(h * QB, (h + 1) * QB) for h in range(HPG_A)]

    def biased(s, bias_of_head):
        return jnp.concatenate([s[:, hs[h]] + bias_of_head(h) for h in range(HPG_A)], axis=1)

    @pl.when(qb == 0)
    def _():
        for g in range(G_A):
            for h in range(HPG_A):
                for j in range(n_wchunk):
                    dist = tk + float(WINDOW - j * LANE)
                    ok = (dist >= 0.0) & (dist < WINDOW)
                    bw_sc[g, h * n_wslot + j] = jnp.where(ok, -slopes[g][h] * dist, NEG)
                bw_sc[g, h * n_wslot + n_wchunk] = jnp.full((LANE, QB), NEG, F32)

    gsig = jax.nn.sigmoid(g_ref[...])
    gates_t = jnp.concatenate([jnp.transpose(gsig[r * LANE:(r + 1) * LANE]) for r in range(n_qtile)], axis=1)

    jb = lax.broadcasted_iota(jnp.int32, (N_BLK, QB), 0)
    cur = (t0 + lax.broadcasted_iota(jnp.int32, (N_BLK, QB), 1)) >> 6
    forced = (jb == 0) | (jb == cur) | (jb == cur - 1)
    allowed = jb <= cur
    ov_j = lax.broadcasted_iota(jnp.int32, (N_BLK, N_CMP_PAD), 0) * L_SEL
    ov_n = lax.broadcasted_iota(jnp.int32, (N_BLK, N_CMP_PAD), 1) * STRIDE_CMP
    ov_t = jnp.where((ov_n < ov_j + L_SEL) & (ov_n + L_CMP > ov_j), 1.0, 0.0).astype(BF16)
    jb8 = lax.broadcasted_iota(jnp.int32, (SUBLANE, QB), 0)

    d_cmp = (t0 - (L_CMP - 1)).astype(F32) + (t_i - STRIDE_CMP * k_i).astype(F32)
    valid_cmp = (d_cmp >= 0.0) & (k_i < N_CMP_PAD - 1)

    gls = [slice(g * DK_A, (g + 1) * DK_A) for g in range(G_A)]
    q4s, o_cmps = [], []
    for g in range(G_A):
        q4 = jnp.concatenate(
            [q_ref[:, (g * HPG_A + h) * DK_A:(g * HPG_A + h + 1) * DK_A] for h in range(HPG_A)], axis=0)
        q4 = (q4.astype(F32) * (LOG2E * DK_A ** -0.5)).astype(BF16)
        q4s.append(q4)

        s = biased(_nt_dot(kc_ref[0, g], q4),
                   lambda h: jnp.where(valid_cmp, -slopes[g][h] * d_cmp, -jnp.inf))
        mx = jnp.max(s, axis=0, keepdims=True)
        mx = jnp.where(mx > -jnp.inf, mx, 0.0)
        e = jnp.exp2(s - mx)
        p = e * (1.0 / jnp.maximum(jnp.sum(e, axis=0, keepdims=True), 1e-30))
        o_cmps.append(_dot(vct_ref[0][gls[g], :], p.astype(BF16)))

        psum = p[:, hs[0]] + p[:, hs[1]] + p[:, hs[2]] + p[:, hs[3]]
        p1, p2, p3 = _split3(psum)
        imp = _dot(ov_t, p1) + _dot(ov_t, p2) + _dot(ov_t, p3)
        score = jnp.where(allowed, jnp.where(forced, BIG, imp), -BIG)
        tiles = [score[r * SUBLANE:(r + 1) * SUBLANE] for r in range(N_BLK // SUBLANE)]
        cnt = [jnp.zeros((SUBLANE, QB), F32) for _ in tiles]
        for j in range(N_BLK):
            sj = score[j:j + 1, :]
            for r, tile in enumerate(tiles):
                lo = r * SUBLANE
                if j >= lo + SUBLANE:
                    beats = sj > tile
                elif j < lo:
                    beats = sj >= tile
                else:
                    beats = (sj > tile) | ((sj >= tile) & (jb8 > j - lo))
                cnt[r] = cnt[r] + jnp.where(beats, 1.0, 0.0)
        sel_sc[g] = jnp.where((jnp.concatenate(cnt, axis=0) < N_SEL) & allowed, 1.0, 0.0)

        s_w, v_w = [], []
        for j in range(n_wchunk):
            c = qb * n_qtile - WINDOW // LANE + j
            k0 = pl.multiple_of(jnp.maximum(c, 0) * LANE, LANE)
            slot = jnp.where(c >= 0, j, n_wchunk)
            sj = _nt_dot(k_ref[pl.ds(k0, LANE), LANE + g * DK_A:LANE + (g + 1) * DK_A], q4)
            s_w.append(biased(sj, lambda h: bw_sc[g, h * n_wslot + slot]))
            v_w.append(vt_ref[G_A * LANE + g * LANE:G_A * LANE + (g + 1) * LANE, pl.ds(k0, LANE)]
                       + ones_row[:, :LANE])
        mx = s_w[0]
        for j in range(1, n_wchunk):
            mx = jnp.maximum(mx, s_w[j])
        mx = jnp.max(mx, axis=0, keepdims=True)
        o_win = _dot(v_w[0], jnp.exp2((s_w[0] - mx).astype(BF16)))
        for j in range(1, n_wchunk):
            o_win = o_win + _dot(v_w[j], jnp.exp2((s_w[j] - mx).astype(BF16)))
        ow_sc[g] = o_win

    chosen_any = jnp.maximum(sel_sc[0], sel_sc[1])
    blk_per_chunk = CK // L_SEL
    n_causal = ((qb + 1) * QB - 1) // CK + 1
    n_chunk = jnp.int32(0)
    for c in range(s_sc.shape[1]):
        need = (jnp.max(chosen_any[c * blk_per_chunk:(c + 1) * blk_per_chunk, :]) > 0.5) & (c < n_causal)

        @pl.when(need)
        def _():
            idx_sc[n_chunk] = c

        n_chunk = n_chunk + jnp.where(need, 1, 0)

    def score_chunk(i, m_runs):
        c = idx_sc[i]
        k0 = pl.multiple_of(c * CK, CK)
        out = []
        for g in range(G_A):
            sc = _nt_dot(k_ref[pl.ds(k0, CK), gls[g]], q4s[g])
            halves = []
            for u in range(CK // LANE):
                dist = tk + (t0 - k0 - u * LANE).astype(F32)
                blk0 = (CK // L_SEL) * c + (LANE // L_SEL) * u
                chosen = jnp.concatenate(
                    [jnp.broadcast_to(sel_sc[g, pl.ds(blk0 + i, 1), :], (L_SEL, QB))
                     for i in range(LANE // L_SEL)], axis=0)
                ok = (chosen > 0.5) & (dist >= 0.0)
                far = jnp.where(ok, dist, -NEG)
                halves.append(biased(sc[u * LANE:(u + 1) * LANE], lambda h: far * -slopes[g][h]))
            sc = jnp.concatenate(halves, axis=0)
            s_sc[g, i] = sc
            out.append(jnp.maximum(m_runs[g], jnp.max(sc.reshape(CK // SUBLANE, SUBLANE, cols), axis=0)))
        return tuple(out)

    m_runs = lax.fori_loop(0, n_chunk, score_chunk, (jnp.full((SUBLANE, cols), NEG, F32),) * G_A)
    m_rows = [jnp.max(m_runs[g], axis=0, keepdims=True) for g in range(G_A)]
    acc_sc[...] = jnp.zeros(acc_sc.shape, F32)

    def value_chunk(i, carry):
        k0 = pl.multiple_of(idx_sc[i] * CK, CK)
        for g in range(G_A):
            pc = jnp.exp2((s_sc[g, i] - m_rows[g]).astype(BF16))
            acc_sc[g] += _dot(vt_ref[g * LANE:(g + 1) * LANE, pl.ds(k0, CK)] + ones_row, pc)
        return carry

    lax.fori_loop(0, n_chunk, value_chunk, 0)

    pieces = []
    for g in range(G_A):
        o_sel, o_win = acc_sc[g], ow_sc[g]
        for h in range(HPG_A):
            col = g * HPG_A + h
            os_h, ow_h = o_sel[:, hs[h]], o_win[:, hs[h]]
            w_sel = gates_t[NH_A + col:NH_A + col + 1, :] / jnp.maximum(os_h[DK_A:DK_A + 1, :], 1e-30)
            w_win = gates_t[2 * NH_A + col:2 * NH_A + col + 1, :] / jnp.maximum(ow_h[DK_A:DK_A + 1, :], 1e-30)
            pieces.append(gates_t[col:col + 1, :] * o_cmps[g][:, hs[h]] + w_sel * os_h[:DK_A] + w_win * ow_h[:DK_A])

    out_t = jnp.concatenate(pieces, axis=0)
    out = jnp.concatenate(
        [jnp.concatenate([jnp.transpose(out_t[i * LANE:(i + 1) * LANE, r * LANE:(r + 1) * LANE])
                          for i in range(D_A // LANE)], axis=1) for r in range(n_qtile)], axis=0)
    o_ref[...] = (out * jax.nn.silu(z_ref[...].astype(F32))).astype(BF16)


def _nsa(main, kk, vt, small, kcmp, vcmp_t, bsz, t_len):
    nqb = t_len // QB
    cols = HPG_A * QB
    n_wslot = (WINDOW + QB) // LANE + 1
    return pl.pallas_call(
        _nsa_kernel,
        grid=(bsz, nqb),
        in_specs=[pl.BlockSpec((QB, D_A), lambda b, i: (b * nqb + i, COL_AQ)),
                  pl.BlockSpec((QB, D_A), lambda b, i: (b * nqb + i, COL_AZ)),
                  pl.BlockSpec((QB, LANE), lambda b, i: (b * nqb + i, 0)),
                  pl.BlockSpec((1, G_A, N_CMP_PAD, DK_A), lambda b, i: (b, 0, 0, 0)),
                  pl.BlockSpec((1, G_A * DK_A, N_CMP_PAD), lambda b, i: (b, 0, 0)),
                  pl.BlockSpec((t_len, K_W), lambda b, i: (b, 0)),
                  pl.BlockSpec((NSA_VT_W, t_len), lambda b, i: (0, b))],
        out_specs=pl.BlockSpec((QB, D_A), lambda b, i: (b * nqb + i, 0)),
        out_shape=jax.ShapeDtypeStruct((bsz * t_len, D_A), BF16),
        scratch_shapes=[pltpu.VMEM((G_A, t_len // CK, CK, cols), F32), pltpu.VMEM((G_A, LANE, cols), F32),
                        pltpu.VMEM((G_A, LANE, cols), F32),
                        pltpu.VMEM((G_A, HPG_A * n_wslot, LANE, QB), F32), pltpu.VMEM((G_A, N_BLK, QB), F32),
                        pltpu.SMEM((t_len // CK,), jnp.int32)],
        compiler_params=_params("parallel", "arbitrary"),
        name="nsa_attention",
    )(main, main, small, kcmp, vcmp_t, kk, vt)


def _sgu_kernel(u_ref, v_ref, z_ref, lng_ref, lnb_ref, w_ref, b_ref, o_ref):
    u = jax.nn.gelu(u_ref[...].astype(F32))
    v = jax.nn.gelu(v_ref[...].astype(F32))
    mu = jnp.mean(v, axis=-1, keepdims=True)
    var = jnp.mean(jnp.square(v - mu), axis=-1, keepdims=True)
    vn = ((v - mu) * lax.rsqrt(var + LN_EPS) * lng_ref[...] + lnb_ref[...]).astype(BF16)
    gate = u * jax.nn.silu(z_ref[...].astype(F32))
    ti = lax.broadcasted_iota(jnp.int32, (CHUNK_B, CHUNK_B), 0)
    si = lax.broadcasted_iota(jnp.int32, (CHUNK_B, CHUNK_B), 1)
    n_chunks = u_ref.shape[0] // CHUNK_B
    for g in range(G_B):
        w = jnp.where(si <= ti, w_ref[g], 0.0).astype(BF16)
        bias = b_ref[:, g:g + 1]
        ls = slice(g * LANE, (g + 1) * LANE)
        for c in range(n_chunks):
            rs = slice(c * CHUNK_B, (c + 1) * CHUNK_B)
            o_ref[rs, ls] = (gate[rs, ls] * (_dot(w, vn[rs, ls]) + bias)).astype(BF16)


def _sgu(main, lng, lnb, w, b_t, layer, tm):
    n = main.shape[0]
    blk = lambda col: pl.BlockSpec((tm, D_B), lambda i: (i, col))
    full = lambda a: _layer_spec(a, layer, 1)
    return pl.pallas_call(
        _sgu_kernel,
        grid=(n // tm,),
        in_specs=[blk(COL_BU), blk(COL_BV), blk(COL_BZ), full(lng), full(lnb), full(w), full(b_t)],
        out_specs=pl.BlockSpec((tm, D_B), lambda i: (i, 0)),
        out_shape=jax.ShapeDtypeStruct((n, D_B), BF16),
        compiler_params=_params("parallel"),
        name="spatial_gating",
    )(main, main, main, lng, lnb, w, b_t)


def _mlstm_kernel(q_ref, k_ref, vt_ref, o_ref, z_ref, g_ref, cw_ref, cb_ref, gb_ref, ng_ref, y_ref,
                  qc_sc, kc_sc, halo_sc, st_sc, m_sc):
    sb = pl.program_id(1)
    blk = q_ref.shape[0]

    @pl.when(sb == 0)
    def _():
        halo_sc[...] = jnp.zeros(halo_sc.shape, F32)
        st_sc[...] = jnp.zeros(st_sc.shape, F32)
        m_sc[...] = jnp.zeros(m_sc.shape, F32)

    r8 = lax.broadcasted_iota(jnp.int32, (SUBLANE, 1), 0)

    def conv_silu(x_ref, which, dst_ref, scale):
        x = x_ref[...].astype(F32)
        prev = halo_sc[which]
        w = cw_ref[:, which * D_C:(which + 1) * D_C]
        acc = x * w[CONV_W - 1:CONV_W, :] + cb_ref[:, which * D_C:(which + 1) * D_C]
        for j in range(1, CONV_W):
            rolled = pltpu.roll(x, j, 0)
            head = jnp.where(r8 < j, pltpu.roll(prev, j, 0), rolled[0:SUBLANE])
            shifted = jnp.concatenate([head, rolled[SUBLANE:]], axis=0)
            acc = acc + shifted * w[CONV_W - 1 - j:CONV_W - j, :]
        halo_sc[which] = x[blk - SUBLANE:blk]
        dst_ref[...] = (jax.nn.silu(acc) * scale).astype(BF16)

    conv_silu(q_ref, 0, qc_sc, 1.0)
    conv_silu(k_ref, 1, kc_sc, DH_C ** -0.5)

    si = lax.broadcasted_iota(jnp.int32, (CHUNK_C, CHUNK_C), 0)
    ti = lax.broadcasted_iota(jnp.int32, (CHUNK_C, CHUNK_C), 1)
    causal = si <= ti
    tri_u = jnp.where(causal, 1.0, 0.0).astype(BF16)
    ones_rows = jnp.where(lax.broadcasted_iota(jnp.int32, (DVX - DH_C, CHUNK_C), 0) == 0, 1.0, 0.0).astype(BF16)
    ng = ng_ref[...]
    gbias = gb_ref[...]

    for c in range(blk // CHUNK_C):
        rs = slice(c * CHUNK_C, (c + 1) * CHUNK_C)
        gate_rows = jnp.transpose(g_ref[rs, :] + gbias)[GI_LANE:GI_LANE + 2 * NH_C]
        f1, f2, f3 = _split3(_log_sigmoid(gate_rows))
        bc_rows = _dot(f1, tri_u) + _dot(f2, tri_u) + _dot(f3, tri_u)
        g_rows = gate_rows[0:NH_C] - bc_rows[NH_C:2 * NH_C]
        g_cols = jnp.transpose(jnp.concatenate([g_rows, jnp.zeros((CHUNK_C - NH_C, CHUNK_C), F32)], axis=0))
        for h in range(NH_C):
            ls = slice(h * DH_C, (h + 1) * DH_C)
            q = qc_sc[rs, ls]
            k = kc_sc[rs, ls]
            vext = jnp.concatenate([vt_ref[ls, rs], ones_rows], axis=0)
            g_col = g_cols[:, h:h + 1]
            g_row = g_rows[h:h + 1, :]
            bc_row = bc_rows[NH_C + h:NH_C + h + 1, :]
            b_last = bc_row[:, CHUNK_C - 1:CHUNK_C]
            m_prev = m_sc[h][:, 0:1]
            st = st_sc[h]
            m_loc = b_last + jnp.max(g_row, axis=1, keepdims=True)
            e_end = jnp.exp(b_last + g_row - m_loc)

            gmat = jnp.where(causal, g_col, -jnp.inf)
            mg = jnp.maximum(m_prev, jnp.max(gmat, axis=0, keepdims=True))
            s_t = _nt_dot(k, q) * jnp.exp(gmat - mg)
            e_int = jnp.exp(m_prev - mg)
            lhs = jnp.concatenate([s_t, jnp.transpose(q.astype(F32)) * e_int], axis=0).astype(BF16)
            rhs = jnp.concatenate([vext, st.astype(BF16)], axis=1)
            both = _dot(rhs, lhs)
            den = both[DH_C:DH_C + 1, :]
            hval = both[:DH_C] * (1.0 / jnp.maximum(jnp.abs(den), jnp.exp(-(bc_row + mg))))
            mu = jnp.mean(hval, axis=0, keepdims=True)
            var = jnp.mean(jnp.square(hval - mu), axis=0, keepdims=True)
            hn = jnp.transpose((hval - mu) * lax.rsqrt(var + LN_EPS))
            o_gate = o_ref[rs, ls].astype(F32)
            z_gate = z_ref[rs, ls].astype(F32)
            gated = hn * ng[:, ls] * z_gate / ((1.0 + jnp.exp(-o_gate)) * (1.0 + jnp.exp(-z_gate)))
            y_ref[rs, ls] = gated.astype(BF16)

            loc = _dot((vext.astype(F32) * e_end).astype(BF16), k)
            m_new = jnp.maximum(b_last + m_prev, m_loc)
            st_sc[h] = jnp.exp(b_last + m_prev - m_new) * st + jnp.exp(m_loc - m_new) * loc
            m_sc[h] = jnp.broadcast_to(m_new, (1, LANE))


def _mlstm(main, vt, small, conv_w, conv_b, gbias, norm_g, layer, bsz, t_len):
    nsb = t_len // SEQ_BLK_C
    blk = lambda col: pl.BlockSpec((SEQ_BLK_C, D_C), lambda b, s: (b * nsb + s, col))
    full = lambda a: _layer_spec(a, layer, 2)
    return pl.pallas_call(
        _mlstm_kernel,
        grid=(bsz, nsb),
        in_specs=[blk(COL_CQ), blk(COL_CK),
                  pl.BlockSpec((D_C, SEQ_BLK_C), lambda b, s: (NSA_VT_W // D_C, b * nsb + s)),
                  blk(COL_CO), blk(COL_CZ),
                  pl.BlockSpec((SEQ_BLK_C, LANE), lambda b, s: (b * nsb + s, 0)),
                  full(conv_w), full(conv_b), full(gbias), full(norm_g)],
        out_specs=pl.BlockSpec((SEQ_BLK_C, D_C), lambda b, s: (b * nsb + s, 0)),
        out_shape=jax.ShapeDtypeStruct((bsz * t_len, D_C), BF16),
        scratch_shapes=[pltpu.VMEM((SEQ_BLK_C, D_C), BF16), pltpu.VMEM((SEQ_BLK_C, D_C), BF16),
                        pltpu.VMEM((2, SUBLANE, D_C), F32), pltpu.VMEM((NH_C, DVX, DH_C), F32),
                        pltpu.VMEM((NH_C, 1, LANE), F32)],
        compiler_params=_params("parallel", "arbitrary"),
        name="mlstm",
    )(main, main, vt, main, main, small, conv_w, conv_b, gbias, norm_g)


def _merge_kernel(x_ref, ya_ref, yb_ref, yc_ref, g0_ref, g1_ref, g2_ref, p_ref,
                  wa_ref, wb_ref, wc_ref, wo_ref, wp_ref, wg_ref, lng_ref, lnb_ref, o_ref, ob_ref):
    merged = (jax.nn.sigmoid(g0_ref[...].astype(F32)) * _dot(ya_ref[...], wa_ref[...])
              + jax.nn.sigmoid(g1_ref[...].astype(F32)) * _dot(yb_ref[...], wb_ref[...])
              + jax.nn.sigmoid(g2_ref[...].astype(F32)) * _dot(yc_ref[...], wc_ref[...]))
    r = ALPHA * x_ref[...] + _dot(merged.astype(BF16), wo_ref[...])
    r = r + jax.nn.sigmoid(_dot(r.astype(BF16), wg_ref[...])) * _dot(p_ref[...].astype(BF16), wp_ref[...])
    mu = jnp.mean(r, axis=-1, keepdims=True)
    var = jnp.mean(jnp.square(r - mu), axis=-1, keepdims=True)
    y = (r - mu) * lax.rsqrt(var + LN_EPS) * lng_ref[...] + lnb_ref[...]
    o_ref[...] = y
    ob_ref[...] = y.astype(BF16)


def _merge(x, ya, yb, yc, main, p, wa, wb, wc, wo, wp, wg, lng, lnb, layer, tm):
    n = x.shape[0]
    row = lambda w: pl.BlockSpec((tm, w), lambda i: (i, 0))
    gate = lambda j: pl.BlockSpec((tm, D_MODEL), lambda i: (i, j))
    full = lambda a: _layer_spec(a, layer, 1)
    return pl.pallas_call(
        _merge_kernel,
        grid=(n // tm,),
        in_specs=[row(D_MODEL), row(D_A), row(D_B), row(D_C), gate(0), gate(1), gate(2),
                  pl.BlockSpec((None, tm, PLE_DIM), lambda i: (layer, i, 0)),
                  full(wa), full(wb), full(wc), full(wo), full(wp), full(wg), full(lng), full(lnb)],
        out_specs=[row(D_MODEL), row(D_MODEL)],
        out_shape=[jax.ShapeDtypeStruct((n, D_MODEL), F32), jax.ShapeDtypeStruct((n, D_MODEL), BF16)],
        compiler_params=_params("parallel"),
        name="merge",
    )(x, ya, yb, yc, main, main, main, p, wa, wb, wc, wo, wp, wg, lng, lnb)


def _regroup_kernel(w_ref, main_ref, k_ref, vt_ref, aux_ref):
    w = w_ref[...]
    rows = w.shape[0]
    run = lambda first, last: w[:, SEG[first][0]:SEG[last][1]]
    zeros = lambda n: jnp.zeros((rows, n), F32)
    main_ref[...] = jnp.concatenate([run('m_g', 'm_g'), run('a_q', 'a_q'), run('a_z', 'c_k'), run('c_o', 'c_z')],
                                    axis=1).astype(BF16)
    k_ref[...] = jnp.concatenate([run('a_ks', 'a_ks'), run('a_kw', 'a_kw')], axis=1).astype(BF16)
    v_cols = []
    for name in ('a_vs', 'a_vw'):
        for g in range(G_A):
            lo = SEG[name][0] + g * DK_A
            v_cols += [w[:, lo:lo + DK_A], zeros(LANE - DK_A)]
    v = jnp.concatenate(v_cols + [run('c_v', 'c_v')], axis=1)
    vt_ref[...] = jnp.concatenate(
        [jnp.concatenate([jnp.transpose(v[r * LANE:(r + 1) * LANE, c * LANE:(c + 1) * LANE])
                          for r in range(rows // LANE)], axis=1) for c in range(VT_W // LANE)], axis=0).astype(BF16)
    n_ag = SEG['a_g'][1] - SEG['a_g'][0]
    aux_ref[...] = jnp.concatenate([run('a_kc', 'a_vc'), run('a_g', 'a_g'), zeros(GI_LANE - n_ag), run('c_if', 'c_if'),
                                    zeros(LANE - GI_LANE - 2 * NH_C)], axis=1).astype(BF16)


def _regroup(w_in, rows):
    nl, d, n_in = w_in.shape
    blk = lambda w: pl.BlockSpec((None, rows, w), lambda l, r: (l, r, 0))
    return pl.pallas_call(
        _regroup_kernel,
        grid=(nl, d // rows),
        in_specs=[blk(n_in)],
        out_specs=[blk(MAIN_W), blk(K_W), pl.BlockSpec((None, VT_W, rows), lambda l, r: (l, 0, r)), blk(3 * LANE)],
        out_shape=[jax.ShapeDtypeStruct((nl, d, MAIN_W), BF16), jax.ShapeDtypeStruct((nl, d, K_W), BF16),
                   jax.ShapeDtypeStruct((nl, VT_W, d), BF16), jax.ShapeDtypeStruct((nl, d, 3 * LANE), BF16)],
        compiler_params=_params("parallel", "parallel"),
        name="regroup_w_in",
    )(w_in)


def _prepare(w_in, cmp_wv2, sg_ln_g, sg_ln_b, sg_b,
             ml_conv_b, ml_b_i, ml_b_f, ml_norm_g, w_br_a, w_br_b, w_br_c, w_out, ple_w, ple_gate, ln_g, ln_b):
    nl = w_in.shape[0]
    prm = {}
    prm['w_main'], prm['w_k'], prm['w_vt'], prm['w_aux'] = _regroup(w_in, 256)
    prm['wv2t'] = jnp.swapaxes(cmp_wv2, 1, 2)
    prm['sg_ln_g'], prm['sg_ln_b'] = sg_ln_g[:, None, :], sg_ln_b[:, None, :]
    prm['sg_b_t'] = jnp.swapaxes(sg_b, 1, 2)
    gbias = jnp.zeros((nl, 1, LANE), F32).at[:, 0, GI_LANE:GI_LANE + NH_C].set(ml_b_i)
    prm['gbias'] = gbias.at[:, 0, GF_LANE:GF_LANE + NH_C].set(ml_b_f)
    prm['conv_b'], prm['norm_g'] = ml_conv_b[:, None, :], ml_norm_g[:, None, :]
    for name, a in (('wa', w_br_a), ('wb', w_br_b), ('wc', w_br_c), ('wo', w_out), ('wp', ple_w), ('wg', ple_gate)):
        prm[name] = a.astype(BF16)
    prm['ln_g'], prm['ln_b'] = ln_g[:, None, :], ln_b[:, None, :]
    return prm


def _layer(i, x, xin, p, prm, cmp, sg_w, ml_conv_w, bsz, t_len):
    main = _matmul(xin, prm['w_main'], i, 2048, 1536, "proj_main")
    kk, vt, kc, vc, small = _tail_matmul(xin, prm['w_k'], prm['w_vt'], prm['w_aux'], i, 1024)

    pos_k, pos_v, wk1, wk2, wv1 = cmp
    kcmp, vcmp_t = _compress(kc, vc, pos_k, pos_v, wk1, wk2, wv1, prm['wv2t'], i, bsz, t_len)
    ya = _nsa(main, kk, vt, small, kcmp, vcmp_t, bsz, t_len)
    yb = _sgu(main, prm['sg_ln_g'], prm['sg_ln_b'], sg_w, prm['sg_b_t'], i, 1024)
    yc = _mlstm(main, vt, small, ml_conv_w, prm['conv_b'], prm['gbias'], prm['norm_g'], i, bsz, t_len)
    return _merge(x, ya, yb, yc, main, p, prm['wa'], prm['wb'], prm['wc'], prm['wo'], prm['wp'], prm['wg'],
                  prm['ln_g'], prm['ln_b'], i, 512)


def kernel(x, p, w_in, cmp_pos_k, cmp_pos_v, cmp_wk1, cmp_wk2, cmp_wv1, cmp_wv2, sg_ln_g, sg_ln_b, sg_w, sg_b,
           ml_conv_w, ml_conv_b, ml_b_i, ml_b_f, ml_norm_g, w_br_a, w_br_b, w_br_c, w_out, ple_w, ple_gate,
           ln_g, ln_b):
    bsz, t_len, d = x.shape
    assert d == D_MODEL and t_len == N_CMP_PAD * STRIDE_CMP and t_len // L_SEL == N_BLK
    assert t_len % SEQ_BLK_C == 0 and t_len % CK == 0 and w_in.shape[0] == DEPTH
    prm = _prepare(w_in, cmp_wv2, sg_ln_g, sg_ln_b, sg_b,
                   ml_conv_b, ml_b_i, ml_b_f, ml_norm_g, w_br_a, w_br_b, w_br_c, w_out, ple_w, ple_gate, ln_g, ln_b)
    xf = x.reshape(bsz * t_len, d)
    xin = xf
    pf = p.reshape(DEPTH, bsz * t_len, PLE_DIM)
    for i in range(DEPTH):
        xf, xin = _layer(i, xf, xin, pf, prm, (cmp_pos_k, cmp_pos_v, cmp_wk1, cmp_wk2, cmp_wv1), sg_w, ml_conv_w,
                         bsz, t_len)
    return xf.reshape(bsz, t_len, d)
```

```python
import jax
import jax.numpy as jnp
from jax import lax
from jax.experimental import pallas as pl
from jax.experimental.pallas import tpu as pltpu

F32 = jnp.float32
BF16 = jnp.bfloat16

D_MODEL = 1024
PLE_DIM = 256
D_A = 512
NH_A = 8
G_A = 2
HPG_A = 4
DK_A = 64
L_CMP = 32
STRIDE_CMP = 16
CMP_HID = 128
L_SEL = 64
N_SEL = 8
N_BLK = 32
WINDOW = 256
QB = 256
CK = 256
N_CMP_PAD = 128
BIG = 1e9
D_B = 512
CHUNK_B = 128
G_B = 4
D_C = 512
NH_C = 4
DH_C = 128
CONV_W = 4
CHUNK_C = 128
SEQ_BLK_C = 512
LN_EPS = 1e-5
DEPTH = 2
ALPHA = (2.0 * DEPTH) ** 0.25

LANE = 128
SUBLANE = 8
_SEG_SIZES = (('a_q', D_A), ('a_kc', G_A * DK_A), ('a_vc', G_A * DK_A), ('a_ks', G_A * DK_A), ('a_vs', G_A * DK_A),
              ('a_kw', G_A * DK_A), ('a_vw', G_A * DK_A), ('a_g', 3 * NH_A), ('a_z', D_A),
              ('b_u', D_B), ('b_v', D_B), ('b_z', D_B),
              ('c_q', D_C), ('c_k', D_C), ('c_v', D_C), ('c_if', 2 * NH_C), ('c_o', D_C), ('c_z', D_C),
              ('m_g', 3 * D_MODEL))
SEG = {}
for _name, _size in _SEG_SIZES:
    _lo = sum(s for _, s in _SEG_SIZES[:len(SEG)])
    SEG[_name] = (_lo, _lo + _size)
MAIN_W = 7680
COL_AQ, COL_AZ, COL_BU, COL_BV, COL_BZ, COL_CQ, COL_CK, COL_CO, COL_CZ = 6, 7, 8, 9, 10, 11, 12, 13, 14
K_W = 256
NSA_VT_W = 512
VT_W = NSA_VT_W + 512
DVX = 128 + 16
GI_LANE = 32
GF_LANE = 36
V7X_VMEM_BYTES = 64 * 1024 * 1024
VMEM_LIMIT = V7X_VMEM_BYTES * 13 // 16
NEG = -1e30
LOG2E = 1.4426950408889634


def _params(*sem):
    return pltpu.CompilerParams(dimension_semantics=sem, vmem_limit_bytes=VMEM_LIMIT)


def _nt_dot(a, b):
    return lax.dot_general(a, b, (((1,), (1,)), ((), ())), preferred_element_type=F32)


def _dot(a, b):
    return jnp.dot(a, b, preferred_element_type=F32)


def _split3(a):
    a1 = a.astype(BF16)
    r1 = a - a1.astype(F32)
    a2 = r1.astype(BF16)
    a3 = (r1 - a2.astype(F32)).astype(BF16)
    return a1, a2, a3


def _log_sigmoid(x):
    return jnp.minimum(x, 0.0) - jnp.log1p(jnp.exp(-jnp.abs(x)))


def _layer_spec(a, layer, grid_rank):
    zeros = (0,) * (a.ndim - 1)
    if grid_rank == 1:
        return pl.BlockSpec((None,) + a.shape[1:], lambda i: (layer,) + zeros)
    return pl.BlockSpec((None,) + a.shape[1:], lambda i, j: (layer,) + zeros)


def _mm_kernel(x_ref, wt_ref, o_ref):
    o_ref[...] = _nt_dot(x_ref[...].astype(BF16), wt_ref[...]).astype(o_ref.dtype)


def _matmul(x, wt, layer, tm, tn, name):
    m, k = x.shape
    n = wt.shape[1]
    return pl.pallas_call(
        _mm_kernel,
        grid=(m // tm, n // tn),
        in_specs=[pl.BlockSpec((tm, k), lambda i, j: (i, 0)),
                  pl.BlockSpec((None, tn, k), lambda i, j: (layer, j, 0))],
        out_specs=pl.BlockSpec((tm, tn), lambda i, j: (i, j)),
        out_shape=jax.ShapeDtypeStruct((m, n), BF16),
        compiler_params=_params("parallel", "arbitrary"),
        name=name,
    )(x, wt)


def _tail_kernel(x_ref, wkt_ref, wvt_ref, wauxt_ref, ok_ref, ovt_ref, okc_ref, ovc_ref, osm_ref):
    x = x_ref[...].astype(BF16)
    ok_ref[...] = _nt_dot(x, wkt_ref[...]).astype(BF16)
    ovt_ref[...] = _nt_dot(wvt_ref[...], x).astype(BF16)
    aux = _nt_dot(x, wauxt_ref[...])
    okc_ref[...] = aux[:, 0:LANE]
    ovc_ref[...] = aux[:, LANE:2 * LANE]
    osm_ref[...] = aux[:, 2 * LANE:3 * LANE]


def _tail_matmul(x, wk, wvt, waux, layer, tm):
    m, k = x.shape
    tok = lambda w: pl.BlockSpec((tm, w), lambda i: (i, 0))
    f32_out = jax.ShapeDtypeStruct((m, LANE), F32)
    return pl.pallas_call(
        _tail_kernel,
        grid=(m // tm,),
        in_specs=[tok(k), _layer_spec(wk, layer, 1), _layer_spec(wvt, layer, 1), _layer_spec(waux, layer, 1)],
        out_specs=[tok(K_W), pl.BlockSpec((VT_W, tm), lambda i: (0, i)), tok(LANE), tok(LANE), tok(LANE)],
        out_shape=[jax.ShapeDtypeStruct((m, K_W), BF16), jax.ShapeDtypeStruct((VT_W, m), BF16),
                   f32_out, f32_out, f32_out],
        compiler_params=_params("parallel"),
        name="proj_tail",
    )(x, wk, wvt, waux)


def _compress_kernel(zk_ref, zv_ref, pk_ref, pv_ref, wk1_ref, wk2_ref, wv1_ref, wv2t_ref, ko_ref, vot_ref):
    n = zk_ref.shape[0] // STRIDE_CMP

    def hidden(z_ref, p_ref, w1_ref):
        r0 = [jnp.zeros((n, CMP_HID), F32) for _ in range(G_A)]
        r1 = [jnp.zeros((n, CMP_HID), F32) for _ in range(G_A)]
        for l in range(STRIDE_CMP):
            zl = z_ref[pl.ds(l, n, stride=STRIDE_CMP), :]
            w_a = w1_ref[l * DK_A:(l + 1) * DK_A, :].astype(BF16)
            w_b = w1_ref[(STRIDE_CMP + l) * DK_A:(STRIDE_CMP + l + 1) * DK_A, :].astype(BF16)
            for g in range(G_A):
                zg = zl[:, g * DK_A:(g + 1) * DK_A]
                r0[g] = r0[g] + _dot((zg + p_ref[l:l + 1, :]).astype(BF16), w_a)
                r1[g] = r1[g] + _dot((zg + p_ref[STRIDE_CMP + l:STRIDE_CMP + l + 1, :]).astype(BF16), w_b)
        return [jax.nn.gelu(r0[g] + pltpu.roll(r1[g], n - 1, 0)).astype(BF16) for g in range(G_A)]

    hk = hidden(zk_ref, pk_ref, wk1_ref)
    wk2 = wk2_ref[...].astype(BF16)
    for g in range(G_A):
        ko_ref[0, g] = _dot(hk[g], wk2).astype(BF16)
    hv = hidden(zv_ref, pv_ref, wv1_ref)
    wv2t = wv2t_ref[...].astype(BF16)
    vot_ref[0] = jnp.concatenate([_nt_dot(wv2t, hv[g]) for g in range(G_A)], axis=0).astype(BF16)


def _compress(kc, vc, pk, pv, wk1, wk2, wv1, wv2t, layer, bsz, t_len):
    n = t_len // STRIDE_CMP
    zspec = pl.BlockSpec((t_len, LANE), lambda i: (i, 0))
    full = lambda a: _layer_spec(a, layer, 1)
    return pl.pallas_call(
        _compress_kernel,
        grid=(bsz,),
        in_specs=[zspec, zspec, full(pk), full(pv), full(wk1), full(wk2), full(wv1), full(wv2t)],
        out_specs=[pl.BlockSpec((1, G_A, n, DK_A), lambda i: (i, 0, 0, 0)),
                   pl.BlockSpec((1, G_A * DK_A, n), lambda i: (i, 0, 0))],
        out_shape=[jax.ShapeDtypeStruct((bsz, G_A, n, DK_A), BF16), jax.ShapeDtypeStruct((bsz, G_A * DK_A, n), BF16)],
        compiler_params=_params("parallel"),
        name="nsa_compress",
    )(kc, vc, pk, pv, wk1, wk2, wv1, wv2t)


def _nsa_kernel(q_ref, z_ref, g_ref, kc_ref, vct_ref, k_ref, vt_ref, o_ref,
                s_sc, acc_sc, ow_sc, bw_sc, sel_sc, idx_sc):
    qb = pl.program_id(1)
    t0 = qb * QB
    cols = HPG_A * QB
    n_qtile = QB // LANE
    n_wchunk = (WINDOW + QB) // LANE
    n_wslot = n_wchunk + 1
    k_i = lax.broadcasted_iota(jnp.int32, (LANE, 1), 0)
    t_i = lax.broadcasted_iota(jnp.int32, (1, QB), 1)
    tk = (t_i - k_i).astype(F32)
    ones_row = jnp.where(lax.broadcasted_iota(jnp.int32, (LANE, CK), 0) == DK_A, 1.0, 0.0).astype(BF16)
    slopes = [[LOG2E * 2.0 ** -(g * HPG_A + h + 1) for h in range(HPG_A)] for g in range(G_A)]
    hs = [slice(h * QB, (h + 1) * QB) for h in range(HPG_A)]

    def biased(s, bias_of_head):
        return jnp.concatenate([s[:, hs[h]] + bias_of_head(h) for h in range(HPG_A)], axis=1)

    @pl.when(qb == 0)
    def _():
        for g in range(G_A):
            for h in range(HPG_A):
                for j in range(n_wchunk):
                    dist = tk + float(WINDOW - j * LANE)
                    ok = (dist >= 0.0) & (dist < WINDOW)
                    bw_sc[g, h * n_wslot + j] = jnp.where(ok, -slopes[g][h] * dist, NEG)
                bw_sc[g, h * n_wslot + n_wchunk] = jnp.full((LANE, QB), NEG, F32)

    gsig = jax.nn.sigmoid(g_ref[...])
    gates_t = jnp.concatenate([jnp.transpose(gsig[r * LANE:(r + 1) * LANE]) for r in range(n_qtile)], axis=1)

    jb = lax.broadcasted_iota(jnp.int32, (N_BLK, QB), 0)
    cur = (t0 + lax.broadcasted_iota(jnp.int32, (N_BLK, QB), 1)) >> 6
    forced = (jb == 0) | (jb == cur) | (jb == cur - 1)
    allowed = jb <= cur
    ov_j = lax.broadcasted_iota(jnp.int32, (N_BLK, N_CMP_PAD), 0) * L_SEL
    ov_n = lax.broadcasted_iota(jnp.int32, (N_BLK, N_CMP_PAD), 1) * STRIDE_CMP
    ov_t = jnp.where((ov_n < ov_j + L_SEL) & (ov_n + L_CMP > ov_j), 1.0, 0.0).astype(BF16)
    jb8 = lax.broadcasted_iota(jnp.int32, (SUBLANE, QB), 0)

    d_cmp = (t0 - (L_CMP - 1)).astype(F32) + (t_i - STRIDE_CMP * k_i).astype(F32)
    valid_cmp = (d_cmp >= 0.0) & (k_i < N_CMP_PAD - 1)

    gls = [slice(g * DK_A, (g + 1) * DK_A) for g in range(G_A)]
    q4s, o_cmps = [], []
    for g in range(G_A):
        q4 = jnp.concatenate(
            [q_ref[:, (g * HPG_A + h) * DK_A:(g * HPG_A + h + 1) * DK_A] for h in range(HPG_A)], axis=0)
        q4 = (q4.astype(F32) * (LOG2E * DK_A ** -0.5)).astype(BF16)
        q4s.append(q4)

        s = biased(_nt_dot(kc_ref[0, g], q4),
                   lambda h: jnp.where(valid_cmp, -slopes[g][h] * d_cmp, -jnp.inf))
        mx = jnp.max(s, axis=0, keepdims=True)
        mx = jnp.where(mx > -jnp.inf, mx, 0.0)
        e = jnp.exp2(s - mx)
        p = e * (1.0 / jnp.maximum(jnp.sum(e, axis=0, keepdims=True), 1e-30))
        o_cmps.append(_dot(vct_ref[0][gls[g], :], p.astype(BF16)))

        psum = p[:, hs[0]] + p[:, hs[1]] + p[:, hs[2]] + p[:, hs[3]]
        p1, p2, p3 = _split3(psum)
        imp = _dot(ov_t, p1) + _dot(ov_t, p2) + _dot(ov_t, p3)
        score = jnp.where(allowed, jnp.where(forced, BIG, imp), -BIG)
        tiles = [score[r * SUBLANE:(r + 1) * SUBLANE] for r in range(N_BLK // SUBLANE)]
        cnt = [jnp.zeros((SUBLANE, QB), F32) for _ in tiles]
        for j in range(N_BLK):
            sj = score[j:j + 1, :]
            for r, tile in enumerate(tiles):
                lo = r * SUBLANE
                if j >= lo + SUBLANE:
                    beats = sj > tile
                elif j < lo:
                    beats = sj >= tile
                else:
                    beats = (sj > tile) | ((sj >= tile) & (jb8 > j - lo))
                cnt[r] = cnt[r] + jnp.where(beats, 1.0, 0.0)
        sel_sc[g] = jnp.where((jnp.concatenate(cnt, axis=0) < N_SEL) & allowed, 1.0, 0.0)

        s_w, v_w = [], []
        for j in range(n_wchunk):
            c = qb * n_qtile - WINDOW // LANE + j
            k0 = pl.multiple_of(jnp.maximum(c, 0) * LANE, LANE)
            slot = jnp.where(c >= 0, j, n_wchunk)
            sj = _nt_dot(k_ref[pl.ds(k0, LANE), LANE + g * DK_A:LANE + (g + 1) * DK_A], q4)
            s_w.append(biased(sj, lambda h: bw_sc[g, h * n_wslot + slot]))
            v_w.append(vt_ref[G_A * LANE + g * LANE:G_A * LANE + (g + 1) * LANE, pl.ds(k0, LANE)]
                       + ones_row[:, :LANE])
        mx = s_w[0]
        for j in range(1, n_wchunk):
            mx = jnp.maximum(mx, s_w[j])
        mx = jnp.max(mx, axis=0, keepdims=True)
        o_win = _dot(v_w[0], jnp.exp2((s_w[0] - mx).astype(BF16)))
        for j in range(1, n_wchunk):
            o_win = o_win + _dot(v_w[j], jnp.exp2((s_w[j] - mx).astype(BF16)))
        ow_sc[g] = o_win

    chosen_any = jnp.maximum(sel_sc[0], sel_sc[1])
    blk_per_chunk = CK // L_SEL
    n_causal = ((qb + 1) * QB - 1) // CK + 1
    n_chunk = jnp.int32(0)
    for c in range(s_sc.shape[1]):
        need = (jnp.max(chosen_any[c * blk_per_chunk:(c + 1) * blk_per_chunk, :]) > 0.5) & (c < n_causal)

        @pl.when(need)
        def _():
            idx_sc[n_chunk] = c

        n_chunk = n_chunk + jnp.where(need, 1, 0)

    def score_chunk(i, m_runs):
        c = idx_sc[i]
        k0 = pl.multiple_of(c * CK, CK)
        out = []
        for g in range(G_A):
            sc = _nt_dot(k_ref[pl.ds(k0, CK), gls[g]], q4s[g])
            halves = []
            for u in range(CK // LANE):
                dist = tk + (t0 - k0 - u * LANE).astype(F32)
                blk0 = (CK // L_SEL) * c + (LANE // L_SEL) * u
                chosen = jnp.concatenate(
                    [jnp.broadcast_to(sel_sc[g, pl.ds(blk0 + i, 1), :], (L_SEL, QB))
                     for i in range(LANE // L_SEL)], axis=0)
                ok = (chosen > 0.5) & (dist >= 0.0)
                far = jnp.where(ok, dist, -NEG)
                halves.append(biased(sc[u * LANE:(u + 1) * LANE], lambda h: far * -slopes[g][h]))
            sc = jnp.concatenate(halves, axis=0)
            s_sc[g, i] = sc
            out.append(jnp.maximum(m_runs[g], jnp.max(sc.reshape(CK // SUBLANE, SUBLANE, cols), axis=0)))
        return tuple(out)

    m_runs = lax.fori_loop(0, n_chunk, score_chunk, (jnp.full((SUBLANE, cols), NEG, F32),) * G_A)
    m_rows = [jnp.max(m_runs[g], axis=0, keepdims=True) for g in range(G_A)]
    acc_sc[...] = jnp.zeros(acc_sc.shape, F32)

    def value_chunk(i, carry):
        k0 = pl.multiple_of(idx_sc[i] * CK, CK)
        for g in range(G_A):
            pc = jnp.exp2((s_sc[g, i] - m_rows[g]).astype(BF16))
            acc_sc[g] += _dot(vt_ref[g * LANE:(g + 1) * LANE, pl.ds(k0, CK)] + ones_row, pc)
        return carry

    lax.fori_loop(0, n_chunk, value_chunk, 0)

    pieces = []
    for g in range(G_A):
        o_sel, o_win = acc_sc[g], ow_sc[g]
        for h in range(HPG_A):
            col = g * HPG_A + h
            os_h, ow_h = o_sel[:, hs[h]], o_win[:, hs[h]]
            w_sel = gates_t[NH_A + col:NH_A + col + 1, :] / jnp.maximum(os_h[DK_A:DK_A + 1, :], 1e-30)
            w_win = gates_t[2 * NH_A + col:2 * NH_A + col + 1, :] / jnp.maximum(ow_h[DK_A:DK_A + 1, :], 1e-30)
            pieces.append(gates_t[col:col + 1, :] * o_cmps[g][:, hs[h]] + w_sel * os_h[:DK_A] + w_win * ow_h[:DK_A])

    out_t = jnp.concatenate(pieces, axis=0)
    out = jnp.concatenate(
        [jnp.concatenate([jnp.transpose(out_t[i * LANE:(i + 1) * LANE, r * LANE:(r + 1) * LANE])
                          for i in range(D_A // LANE)], axis=1) for r in range(n_qtile)], axis=0)
    o_ref[...] = (out * jax.nn.silu(z_ref[...].astype(F32))).astype(BF16)


def _nsa(main, kk, vt, small, kcmp, vcmp_t, bsz, t_len):
    nqb = t_len // QB
    cols = HPG_A * QB
    n_wslot = (WINDOW + QB) // LANE + 1
    return pl.pallas_call(
        _nsa_kernel,
        grid=(bsz, nqb),
        in_specs=[pl.BlockSpec((QB, D_A), lambda b, i: (b * nqb + i, COL_AQ)),
                  pl.BlockSpec((QB, D_A), lambda b, i: (b * nqb + i, COL_AZ)),
                  pl.BlockSpec((QB, LANE), lambda b, i: (b * nqb + i, 0)),
                  pl.BlockSpec((1, G_A, N_CMP_PAD, DK_A), lambda b, i: (b, 0, 0, 0)),
                  pl.BlockSpec((1, G_A * DK_A, N_CMP_PAD), lambda b, i: (b, 0, 0)),
                  pl.BlockSpec((t_len, K_W), lambda b, i: (b, 0)),
                  pl.BlockSpec((NSA_VT_W, t_len), lambda b, i: (0, b))],
        out_specs=pl.BlockSpec((QB, D_A), lambda b, i: (b * nqb + i, 0)),
        out_shape=jax.ShapeDtypeStruct((bsz * t_len, D_A), BF16),
        scratch_shapes=[pltpu.VMEM((G_A, t_len // CK, CK, cols), F32), pltpu.VMEM((G_A, LANE, cols), F32),
                        pltpu.VMEM((G_A, LANE, cols), F32),
                        pltpu.VMEM((G_A, HPG_A * n_wslot, LANE, QB), F32), pltpu.VMEM((G_A, N_BLK, QB), F32),
                        pltpu.SMEM((t_len // CK,), jnp.int32)],
        compiler_params=_params("parallel", "arbitrary"),
        name="nsa_attention",
    )(main, main, small, kcmp, vcmp_t, kk, vt)


def _sgu_kernel(u_ref, v_ref, z_ref, lng_ref, lnb_ref, w_ref, b_ref, o_ref):
    u = jax.nn.gelu(u_ref[...].astype(F32))
    v = jax.nn.gelu(v_ref[...].astype(F32))
    mu = jnp.mean(v, axis=-1, keepdims=True)
    var = jnp.mean(jnp.square(v - mu), axis=-1, keepdims=True)
    vn = ((v - mu) * lax.rsqrt(var + LN_EPS) * lng_ref[...] + lnb_ref[...]).astype(BF16)
    gate = u * jax.nn.silu(z_ref[...].astype(F32))
    ti = lax.broadcasted_iota(jnp.int32, (CHUNK_B, CHUNK_B), 0)
    si = lax.broadcasted_iota(jnp.int32, (CHUNK_B, CHUNK_B), 1)
    n_chunks = u_ref.shape[0] // CHUNK_B
    for g in range(G_B):
        w = jnp.where(si <= ti, w_ref[g], 0.0).astype(BF16)
        bias = b_ref[:, g:g + 1]
        ls = slice(g * LANE, (g + 1) * LANE)
        for c in range(n_chunks):
            rs = slice(c * CHUNK_B, (c + 1) * CHUNK_B)
            o_ref[rs, ls] = (gate[rs, ls] * (_dot(w, vn[rs, ls]) + bias)).astype(BF16)


def _sgu(main, lng, lnb, w, b_t, layer, tm):
    n = main.shape[0]
    blk = lambda col: pl.BlockSpec((tm, D_B), lambda i: (i, col))
    full = lambda a: _layer_spec(a, layer, 1)
    return pl.pallas_call(
        _sgu_kernel,
        grid=(n // tm,),
        in_specs=[blk(COL_BU), blk(COL_BV), blk(COL_BZ), full(lng), full(lnb), full(w), full(b_t)],
        out_specs=pl.BlockSpec((tm, D_B), lambda i: (i, 0)),
        out_shape=jax.ShapeDtypeStruct((n, D_B), BF16),
        compiler_params=_params("parallel"),
        name="spatial_gating",
    )(main, main, main, lng, lnb, w, b_t)


def _mlstm_kernel(q_ref, k_ref, vt_ref, o_ref, z_ref, g_ref, cw_ref, cb_ref, gb_ref, ng_ref, y_ref,
                  qc_sc, kc_sc, halo_sc, st_sc, m_sc):
    sb = pl.program_id(1)
    blk = q_ref.shape[0]

    @pl.when(sb == 0)
    def _():
        halo_sc[...] = jnp.zeros(halo_sc.shape, F32)
        st_sc[...] = jnp.zeros(st_sc.shape, F32)
        m_sc[...] = jnp.zeros(m_sc.shape, F32)

    r8 = lax.broadcasted_iota(jnp.int32, (SUBLANE, 1), 0)

    def conv_silu(x_ref, which, dst_ref, scale):
        x = x_ref[...].astype(F32)
        prev = halo_sc[which]
        w = cw_ref[:, which * D_C:(which + 1) * D_C]
        acc = x * w[CONV_W - 1:CONV_W, :] + cb_ref[:, which * D_C:(which + 1) * D_C]
        for j in range(1, CONV_W):
            rolled = pltpu.roll(x, j, 0)
            head = jnp.where(r8 < j, pltpu.roll(prev, j, 0), rolled[0:SUBLANE])
            shifted = jnp.concatenate([head, rolled[SUBLANE:]], axis=0)
            acc = acc + shifted * w[CONV_W - 1 - j:CONV_W - j, :]
        halo_sc[which] = x[blk - SUBLANE:blk]
        dst_ref[...] = (jax.nn.silu(acc) * scale).astype(BF16)

    conv_silu(q_ref, 0, qc_sc, 1.0)
    conv_silu(k_ref, 1, kc_sc, DH_C ** -0.5)

    si = lax.broadcasted_iota(jnp.int32, (CHUNK_C, CHUNK_C), 0)
    ti = lax.broadcasted_iota(jnp.int32, (CHUNK_C, CHUNK_C), 1)
    causal = si <= ti
    tri_u = jnp.where(causal, 1.0, 0.0).astype(BF16)
    ones_rows = jnp.where(lax.broadcasted_iota(jnp.int32, (DVX - DH_C, CHUNK_C), 0) == 0, 1.0, 0.0).astype(BF16)
    ng = ng_ref[...]
    gbias = gb_ref[...]

    for c in range(blk // CHUNK_C):
        rs = slice(c * CHUNK_C, (c + 1) * CHUNK_C)
        gate_rows = jnp.transpose(g_ref[rs, :] + gbias)[GI_LANE:GI_LANE + 2 * NH_C]
        f1, f2, f3 = _split3(_log_sigmoid(gate_rows))
        bc_rows = _dot(f1, tri_u) + _dot(f2, tri_u) + _dot(f3, tri_u)
        g_rows = gate_rows[0:NH_C] - bc_rows[NH_C:2 * NH_C]
        g_cols = jnp.transpose(jnp.concatenate([g_rows, jnp.zeros((CHUNK_C - NH_C, CHUNK_C), F32)], axis=0))
        for h in range(NH_C):
            ls = slice(h * DH_C, (h + 1) * DH_C)
            q = qc_sc[rs, ls]
            k = kc_sc[rs, ls]
            vext = jnp.concatenate([vt_ref[ls, rs], ones_rows], axis=0)
            g_col = g_cols[:, h:h + 1]
            g_row = g_rows[h:h + 1, :]
            bc_row = bc_rows[NH_C + h:NH_C + h + 1, :]
            b_last = bc_row[:, CHUNK_C - 1:CHUNK_C]
            m_prev = m_sc[h][:, 0:1]
            st = st_sc[h]
            m_loc = b_last + jnp.max(g_row, axis=1, keepdims=True)
            e_end = jnp.exp(b_last + g_row - m_loc)

            gmat = jnp.where(causal, g_col, -jnp.inf)
            mg = jnp.maximum(m_prev, jnp.max(gmat, axis=0, keepdims=True))
            s_t = _nt_dot(k, q) * jnp.exp(gmat - mg)
            e_int = jnp.exp(m_prev - mg)
            lhs = jnp.concatenate([s_t, jnp.transpose(q.astype(F32)) * e_int], axis=0).astype(BF16)
            rhs = jnp.concatenate([vext, st.astype(BF16)], axis=1)
            both = _dot(rhs, lhs)
            den = both[DH_C:DH_C + 1, :]
            hval = both[:DH_C] * (1.0 / jnp.maximum(jnp.abs(den), jnp.exp(-(bc_row + mg))))
            mu = jnp.mean(hval, axis=0, keepdims=True)
            var = jnp.mean(jnp.square(hval - mu), axis=0, keepdims=True)
            hn = jnp.transpose((hval - mu) * lax.rsqrt(var + LN_EPS))
            o_gate = o_ref[rs, ls].astype(F32)
            z_gate = z_ref[rs, ls].astype(F32)
            gated = hn * ng[:, ls] * z_gate / ((1.0 + jnp.exp(-o_gate)) * (1.0 + jnp.exp(-z_gate)))
            y_ref[rs, ls] = gated.astype(BF16)

            loc = _dot((vext.astype(F32) * e_end).astype(BF16), k)
            m_new = jnp.maximum(b_last + m_prev, m_loc)
            st_sc[h] = jnp.exp(b_last + m_prev - m_new) * st + jnp.exp(m_loc - m_new) * loc
            m_sc[h] = jnp.broadcast_to(m_new, (1, LANE))


def _mlstm(main, vt, small, conv_w, conv_b, gbias, norm_g, layer, bsz, t_len):
    nsb = t_len // SEQ_BLK_C
    blk = lambda col: pl.BlockSpec((SEQ_BLK_C, D_C), lambda b, s: (b * nsb + s, col))
    full = lambda a: _layer_spec(a, layer, 2)
    return pl.pallas_call(
        _mlstm_kernel,
        grid=(bsz, nsb),
        in_specs=[blk(COL_CQ), blk(COL_CK),
                  pl.BlockSpec((D_C, SEQ_BLK_C), lambda b, s: (NSA_VT_W // D_C, b * nsb + s)),
                  blk(COL_CO), blk(COL_CZ),
                  pl.BlockSpec((SEQ_BLK_C, LANE), lambda b, s: (b * nsb + s, 0)),
                  full(conv_w), full(conv_b), full(gbias), full(norm_g)],
        out_specs=pl.BlockSpec((SEQ_BLK_C, D_C), lambda b, s: (b * nsb + s, 0)),
        out_shape=jax.ShapeDtypeStruct((bsz * t_len, D_C), BF16),
        scratch_shapes=[pltpu.VMEM((SEQ_BLK_C, D_C), BF16), pltpu.VMEM((SEQ_BLK_C, D_C), BF16),
                        pltpu.VMEM((2, SUBLANE, D_C), F32), pltpu.VMEM((NH_C, DVX, DH_C), F32),
                        pltpu.VMEM((NH_C, 1, LANE), F32)],
        compiler_params=_params("parallel", "arbitrary"),
        name="mlstm",
    )(main, main, vt, main, main, small, conv_w, conv_b, gbias, norm_g)


def _merge_kernel(x_ref, ya_ref, yb_ref, yc_ref, g0_ref, g1_ref, g2_ref, p_ref,
                  wa_ref, wb_ref, wc_ref, wo_ref, wp_ref, wg_ref, lng_ref, lnb_ref, o_ref, ob_ref):
    merged = (jax.nn.sigmoid(g0_ref[...].astype(F32)) * _dot(ya_ref[...], wa_ref[...])
              + jax.nn.sigmoid(g1_ref[...].astype(F32)) * _dot(yb_ref[...], wb_ref[...])
              + jax.nn.sigmoid(g2_ref[...].astype(F32)) * _dot(yc_ref[...], wc_ref[...]))
    r = ALPHA * x_ref[...] + _dot(merged.astype(BF16), wo_ref[...])
    r = r + jax.nn.sigmoid(_dot(r.astype(BF16), wg_ref[...])) * _dot(p_ref[...].astype(BF16), wp_ref[...])
    mu = jnp.mean(r, axis=-1, keepdims=True)
    var = jnp.mean(jnp.square(r - mu), axis=-1, keepdims=True)
    y = (r - mu) * lax.rsqrt(var + LN_EPS) * lng_ref[...] + lnb_ref[...]
    o_ref[...] = y
    ob_ref[...] = y.astype(BF16)


def _merge(x, ya, yb, yc, main, p, wa, wb, wc, wo, wp, wg, lng, lnb, layer, tm):
    n = x.shape[0]
    row = lambda w: pl.BlockSpec((tm, w), lambda i: (i, 0))
    gate = lambda j: pl.BlockSpec((tm, D_MODEL), lambda i: (i, j))
    full = lambda a: _layer_spec(a, layer, 1)
    return pl.pallas_call(
        _merge_kernel,
        grid=(n // tm,),
        in_specs=[row(D_MODEL), row(D_A), row(D_B), row(D_C), gate(0), gate(1), gate(2),
                  pl.BlockSpec((None, tm, PLE_DIM), lambda i: (layer, i, 0)),
                  full(wa), full(wb), full(wc), full(wo), full(wp), full(wg), full(lng), full(lnb)],
        out_specs=[row(D_MODEL), row(D_MODEL)],
        out_shape=[jax.ShapeDtypeStruct((n, D_MODEL), F32), jax.ShapeDtypeStruct((n, D_MODEL), BF16)],
        compiler_params=_params("parallel"),
        name="merge",
    )(x, ya, yb, yc, main, main, main, p, wa, wb, wc, wo, wp, wg, lng, lnb)


def _regroup_kernel(wt_ref, main_ref, k_ref, v_ref, aux_ref):
    wt = wt_ref[...]
    cols = wt.shape[1]
    run = lambda first, last: wt[SEG[first][0]:SEG[last][1]]
    zeros = lambda n: jnp.zeros((n, cols), F32)
    main_ref[...] = jnp.concatenate([run('m_g', 'm_g'), run('a_q', 'a_q'), run('a_z', 'c_k'), run('c_o', 'c_z')],
                                    axis=0).astype(BF16)
    k_ref[...] = jnp.concatenate([run('a_ks', 'a_ks'), run('a_kw', 'a_kw')], axis=0).astype(BF16)
    v_rows = []
    for name in ('a_vs', 'a_vw'):
        for g in range(G_A):
            lo = SEG[name][0] + g * DK_A
            v_rows += [wt[lo:lo + DK_A], zeros(LANE - DK_A)]
    v_ref[...] = jnp.concatenate(v_rows + [run('c_v', 'c_v')], axis=0).astype(BF16)
    n_ag = SEG['a_g'][1] - SEG['a_g'][0]
    aux_ref[...] = jnp.concatenate([run('a_kc', 'a_vc'), run('a_g', 'a_g'), zeros(GI_LANE - n_ag), run('c_if', 'c_if'),
                                    zeros(LANE - GI_LANE - 2 * NH_C)], axis=0).astype(BF16)


def _regroup(w_in_t, cols):
    nl, n_in, d = w_in_t.shape
    blk = lambda rows: pl.BlockSpec((None, rows, cols), lambda l, c: (l, 0, c))
    return pl.pallas_call(
        _regroup_kernel,
        grid=(nl, d // cols),
        in_specs=[blk(n_in)],
        out_specs=[blk(MAIN_W), blk(K_W), blk(VT_W), blk(3 * LANE)],
        out_shape=[jax.ShapeDtypeStruct((nl, MAIN_W, d), BF16), jax.ShapeDtypeStruct((nl, K_W, d), BF16),
                   jax.ShapeDtypeStruct((nl, VT_W, d), BF16), jax.ShapeDtypeStruct((nl, 3 * LANE, d), BF16)],
        compiler_params=_params("parallel", "parallel"),
        name="regroup_w_in",
    )(w_in_t)


def _prepare(w_in, cmp_wv2, sg_ln_g, sg_ln_b, sg_b,
             ml_conv_b, ml_b_i, ml_b_f, ml_norm_g, w_br_a, w_br_b, w_br_c, w_out, ple_w, ple_gate, ln_g, ln_b):
    nl = w_in.shape[0]
    prm = {}
    prm['w_main'], prm['w_k'], prm['w_vt'], prm['w_aux'] = _regroup(jnp.swapaxes(w_in, 1, 2), 256)
    prm['wv2t'] = jnp.swapaxes(cmp_wv2, 1, 2)
    prm['sg_ln_g'], prm['sg_ln_b'] = sg_ln_g[:, None, :], sg_ln_b[:, None, :]
    prm['sg_b_t'] = jnp.swapaxes(sg_b, 1, 2)
    gbias = jnp.zeros((nl, 1, LANE), F32).at[:, 0, GI_LANE:GI_LANE + NH_C].set(ml_b_i)
    prm['gbias'] = gbias.at[:, 0, GF_LANE:GF_LANE + NH_C].set(ml_b_f)
    prm['conv_b'], prm['norm_g'] = ml_conv_b[:, None, :], ml_norm_g[:, None, :]
    for name, a in (('wa', w_br_a), ('wb', w_br_b), ('wc', w_br_c), ('wo', w_out), ('wp', ple_w), ('wg', ple_gate)):
        prm[name] = a.astype(BF16)
    prm['ln_g'], prm['ln_b'] = ln_g[:, None, :], ln_b[:, None, :]
    return prm


def _layer(i, x, xin, p, prm, cmp, sg_w, ml_conv_w, bsz, t_len):
    main = _matmul(xin, prm['w_main'], i, 2048, 1536, "proj_main")
    kk, vt, kc, vc, small = _tail_matmul(xin, prm['w_k'], prm['w_vt'], prm['w_aux'], i, 1024)

    pos_k, pos_v, wk1, wk2, wv1 = cmp
    kcmp, vcmp_t = _compress(kc, vc, pos_k, pos_v, wk1, wk2, wv1, prm['wv2t'], i, bsz, t_len)
    ya = _nsa(main, kk, vt, small, kcmp, vcmp_t, bsz, t_len)
    yb = _sgu(main, prm['sg_ln_g'], prm['sg_ln_b'], sg_w, prm['sg_b_t'], i, 1024)
    yc = _mlstm(main, vt, small, ml_conv_w, prm['conv_b'], prm['gbias'], prm['norm_g'], i, bsz, t_len)
    return _merge(x, ya, yb, yc, main, p, prm['wa'], prm['wb'], prm['wc'], prm['wo'], prm['wp'], prm['wg'],
                  prm['ln_g'], prm['ln_b'], i, 512)


def kernel(x, p, w_in, cmp_pos_k, cmp_pos_v, cmp_wk1, cmp_wk2, cmp_wv1, cmp_wv2, sg_ln_g, sg_ln_b, sg_w, sg_b,
           ml_conv_w, ml_conv_b, ml_b_i, ml_b_f, ml_norm_g, w_br_a, w_br_b, w_br_c, w_out, ple_w, ple_gate,
           ln_g, ln_b):
    bsz, t_len, d = x.shape
    assert d == D_MODEL and t_len == N_CMP_PAD * STRIDE_CMP and t_len // L_SEL == N_BLK
    assert t_len % SEQ_BLK_C == 0 and t_len % CK == 0 and w_in.shape[0] == DEPTH
    prm = _prepare(w_in, cmp_wv2, sg_ln_g, sg_ln_b, sg_b,
                   ml_conv_b, ml_b_i, ml_b_f, ml_norm_g, w_br_a, w_br_b, w_br_c, w_out, ple_w, ple_gate, ln_g, ln_b)
    xf = x.reshape(bsz * t_len, d)
    xin = xf
    pf = p.reshape(DEPTH, bsz * t_len, PLE_DIM)
    for i in range(DEPTH):
        xf, xin = _layer(i, xf, xin, pf, prm, (cmp_pos_k, cmp_pos_v, cmp_wk1, cmp_wk2, cmp_wv1), sg_w, ml_conv_w,
                         bsz, t_len)
    return xf.reshape(bsz, t_len, d)
```

```python
import jax
import jax.numpy as jnp
from jax import lax
from jax.experimental import pallas as pl
from jax.experimental.pallas import tpu as pltpu

F32 = jnp.float32
BF16 = jnp.bfloat16

D_MODEL = 1024
PLE_DIM = 256
D_A = 512
NH_A = 8
G_A = 2
HPG_A = 4
DK_A = 64
L_CMP = 32
STRIDE_CMP = 16
CMP_HID = 128
L_SEL = 64
N_SEL = 8
N_BLK = 32
WINDOW = 256
QB = 256
CK = 256
N_CMP_PAD = 128
BIG = 1e9
D_B = 512
CHUNK_B = 128
G_B = 4
D_C = 512
NH_C = 4
DH_C = 128
CONV_W = 4
CHUNK_C = 128
SEQ_BLK_C = 512
LN_EPS = 1e-5
DEPTH = 2
ALPHA = (2.0 * DEPTH) ** 0.25

LANE = 128
SUBLANE = 8
_SEG_SIZES = (('a_q', D_A), ('a_kc', G_A * DK_A), ('a_vc', G_A * DK_A), ('a_ks', G_A * DK_A), ('a_vs', G_A * DK_A),
              ('a_kw', G_A * DK_A), ('a_vw', G_A * DK_A), ('a_g', 3 * NH_A), ('a_z', D_A),
              ('b_u', D_B), ('b_v', D_B), ('b_z', D_B),
              ('c_q', D_C), ('c_k', D_C), ('c_v', D_C), ('c_if', 2 * NH_C), ('c_o', D_C), ('c_z', D_C),
              ('m_g', 3 * D_MODEL))
SEG = {}
for _name, _size in _SEG_SIZES:
    _lo = sum(s for _, s in _SEG_SIZES[:len(SEG)])
    SEG[_name] = (_lo, _lo + _size)
MAIN_W = 7680
COL_AQ, COL_AZ, COL_BU, COL_BV, COL_BZ, COL_CQ, COL_CK, COL_CO, COL_CZ = 6, 7, 8, 9, 10, 11, 12, 13, 14
K_W = 256
NSA_VT_W = 512
VT_W = NSA_VT_W + 512
DVX = 128 + 16
GI_LANE = 32
GF_LANE = 36
V7X_VMEM_BYTES = 64 * 1024 * 1024
VMEM_LIMIT = V7X_VMEM_BYTES * 13 // 16
NEG = -1e30
LOG2E = 1.4426950408889634


def _params(*sem):
    return pltpu.CompilerParams(dimension_semantics=sem, vmem_limit_bytes=VMEM_LIMIT)


def _nt_dot(a, b):
    return lax.dot_general(a, b, (((1,), (1,)), ((), ())), preferred_element_type=F32)


def _dot(a, b):
    return jnp.dot(a, b, preferred_element_type=F32)


def _split3(a):
    a1 = a.astype(BF16)
    r1 = a - a1.astype(F32)
    a2 = r1.astype(BF16)
    a3 = (r1 - a2.astype(F32)).astype(BF16)
    return a1, a2, a3


def _sigmoid(x):
    return 0.5 * jnp.tanh(0.5 * x) + 0.5


def _silu(x):
    return x * _sigmoid(x)


def _log_sigmoid(x):
    return jnp.minimum(x, 0.0) - jnp.log1p(jnp.exp(-jnp.abs(x)))


def _layer_spec(a, layer, grid_rank):
    zeros = (0,) * (a.ndim - 1)
    if grid_rank == 1:
        return pl.BlockSpec((None,) + a.shape[1:], lambda i: (layer,) + zeros)
    return pl.BlockSpec((None,) + a.shape[1:], lambda i, j: (layer,) + zeros)


def _mm_kernel(x_ref, wt_ref, o_ref):
    o_ref[...] = _nt_dot(x_ref[...].astype(BF16), wt_ref[...]).astype(o_ref.dtype)


def _matmul(x, wt, layer, tm, tn, name):
    m, k = x.shape
    n = wt.shape[1]
    return pl.pallas_call(
        _mm_kernel,
        grid=(m // tm, n // tn),
        in_specs=[pl.BlockSpec((tm, k), lambda i, j: (i, 0)),
                  pl.BlockSpec((None, tn, k), lambda i, j: (layer, j, 0))],
        out_specs=pl.BlockSpec((tm, tn), lambda i, j: (i, j)),
        out_shape=jax.ShapeDtypeStruct((m, n), BF16),
        compiler_params=_params("parallel", "arbitrary"),
        name=name,
    )(x, wt)


def _tail_kernel(x_ref, wkt_ref, wvt_ref, wauxt_ref, ok_ref, ovt_ref, okc_ref, ovc_ref, osm_ref):
    x = x_ref[...].astype(BF16)
    ok_ref[...] = _nt_dot(x, wkt_ref[...]).astype(BF16)
    ovt_ref[...] = _nt_dot(wvt_ref[...], x).astype(BF16)
    aux = _nt_dot(x, wauxt_ref[...])
    okc_ref[...] = aux[:, 0:LANE]
    ovc_ref[...] = aux[:, LANE:2 * LANE]
    osm_ref[...] = aux[:, 2 * LANE:3 * LANE]


def _tail_matmul(x, wk, wvt, waux, layer, tm):
    m, k = x.shape
    tok = lambda w: pl.BlockSpec((tm, w), lambda i: (i, 0))
    f32_out = jax.ShapeDtypeStruct((m, LANE), F32)
    return pl.pallas_call(
        _tail_kernel,
        grid=(m // tm,),
        in_specs=[tok(k), _layer_spec(wk, layer, 1), _layer_spec(wvt, layer, 1), _layer_spec(waux, layer, 1)],
        out_specs=[tok(K_W), pl.BlockSpec((VT_W, tm), lambda i: (0, i)), tok(LANE), tok(LANE), tok(LANE)],
        out_shape=[jax.ShapeDtypeStruct((m, K_W), BF16), jax.ShapeDtypeStruct((VT_W, m), BF16),
                   f32_out, f32_out, f32_out],
        compiler_params=_params("parallel"),
        name="proj_tail",
    )(x, wk, wvt, waux)


def _compress_kernel(zk_ref, zv_ref, pk_ref, pv_ref, wk1_ref, wk2_ref, wv1_ref, wv2t_ref, ko_ref, vot_ref):
    n = zk_ref.shape[0] // STRIDE_CMP

    def hidden(z_ref, p_ref, w1_ref):
        r0 = [jnp.zeros((n, CMP_HID), F32) for _ in range(G_A)]
        r1 = [jnp.zeros((n, CMP_HID), F32) for _ in range(G_A)]
        for l in range(STRIDE_CMP):
            zl = z_ref[pl.ds(l, n, stride=STRIDE_CMP), :]
            w_a = w1_ref[l * DK_A:(l + 1) * DK_A, :].astype(BF16)
            w_b = w1_ref[(STRIDE_CMP + l) * DK_A:(STRIDE_CMP + l + 1) * DK_A, :].astype(BF16)
            for g in range(G_A):
                zg = zl[:, g * DK_A:(g + 1) * DK_A]
                r0[g] = r0[g] + _dot((zg + p_ref[l:l + 1, :]).astype(BF16), w_a)
                r1[g] = r1[g] + _dot((zg + p_ref[STRIDE_CMP + l:STRIDE_CMP + l + 1, :]).astype(BF16), w_b)
        return [jax.nn.gelu(r0[g] + pltpu.roll(r1[g], n - 1, 0)).astype(BF16) for g in range(G_A)]

    hk = hidden(zk_ref, pk_ref, wk1_ref)
    wk2 = wk2_ref[...].astype(BF16)
    for g in range(G_A):
        ko_ref[0, g] = _dot(hk[g], wk2).astype(BF16)
    hv = hidden(zv_ref, pv_ref, wv1_ref)
    wv2t = wv2t_ref[...].astype(BF16)
    vot_ref[0] = jnp.concatenate([_nt_dot(wv2t, hv[g]) for g in range(G_A)], axis=0).astype(BF16)


def _compress(kc, vc, pk, pv, wk1, wk2, wv1, wv2t, layer, bsz, t_len):
    n = t_len // STRIDE_CMP
    zspec = pl.BlockSpec((t_len, LANE), lambda i: (i, 0))
    full = lambda a: _layer_spec(a, layer, 1)
    return pl.pallas_call(
        _compress_kernel,
        grid=(bsz,),
        in_specs=[zspec, zspec, full(pk), full(pv), full(wk1), full(wk2), full(wv1), full(wv2t)],
        out_specs=[pl.BlockSpec((1, G_A, n, DK_A), lambda i: (i, 0, 0, 0)),
                   pl.BlockSpec((1, G_A * DK_A, n), lambda i: (i, 0, 0))],
        out_shape=[jax.ShapeDtypeStruct((bsz, G_A, n, DK_A), BF16), jax.ShapeDtypeStruct((bsz, G_A * DK_A, n), BF16)],
        compiler_params=_params("parallel"),
        name="nsa_compress",
    )(kc, vc, pk, pv, wk1, wk2, wv1, wv2t)


def _nsa_kernel(q_ref, z_ref, g_ref, kc_ref, vct_ref, k_ref, vt_ref, o_ref,
                s_sc, acc_sc, ow_sc, bw_sc, sel_sc, idx_sc):
    qb = pl.program_id(1)
    t0 = qb * QB
    cols = HPG_A * QB
    n_qtile = QB // LANE
    n_wchunk = (WINDOW + QB) // LANE
    n_wslot = n_wchunk + 1
    k_i = lax.broadcasted_iota(jnp.int32, (LANE, 1), 0)
    t_i = lax.broadcasted_iota(jnp.int32, (1, QB), 1)
    tk = (t_i - k_i).astype(F32)
    ones_row = jnp.where(lax.broadcasted_iota(jnp.int32, (LANE, CK), 0) == DK_A, 1.0, 0.0).astype(BF16)
    slopes = [[LOG2E * 2.0 ** -(g * HPG_A + h + 1) for h in range(HPG_A)] for g in range(G_A)]
    hs = [slice(h * QB, (h + 1) * QB) for h in range(HPG_A)]

    def biased(s, bias_of_head):
        return jnp.concatenate([s[:, hs[h]] + bias_of_head(h) for h in range(HPG_A)], axis=1)

    @pl.when(qb == 0)
    def _():
        for g in range(G_A):
            for h in range(HPG_A):
                for j in range(n_wchunk):
                    dist = tk + float(WINDOW - j * LANE)
                    ok = (dist >= 0.0) & (dist < WINDOW)
                    bw_sc[g, h * n_wslot + j] = jnp.where(ok, -slopes[g][h] * dist, NEG)
                bw_sc[g, h * n_wslot + n_wchunk] = jnp.full((LANE, QB), NEG, F32)

    gsig = _sigmoid(g_ref[...])
    gates_t = jnp.concatenate([jnp.transpose(gsig[r * LANE:(r + 1) * LANE]) for r in range(n_qtile)], axis=1)

    jb = lax.broadcasted_iota(jnp.int32, (N_BLK, QB), 0)
    cur = (t0 + lax.broadcasted_iota(jnp.int32, (N_BLK, QB), 1)) >> 6
    forced = (jb == 0) | (jb == cur) | (jb == cur - 1)
    allowed = jb <= cur
    ov_j = lax.broadcasted_iota(jnp.int32, (N_BLK, N_CMP_PAD), 0) * L_SEL
    ov_n = lax.broadcasted_iota(jnp.int32, (N_BLK, N_CMP_PAD), 1) * STRIDE_CMP
    ov_t = jnp.where((ov_n < ov_j + L_SEL) & (ov_n + L_CMP > ov_j), 1.0, 0.0).astype(BF16)
    jb8 = lax.broadcasted_iota(jnp.int32, (SUBLANE, QB), 0)

    d_cmp = (t0 - (L_CMP - 1)).astype(F32) + (t_i - STRIDE_CMP * k_i).astype(F32)
    valid_cmp = (d_cmp >= 0.0) & (k_i < N_CMP_PAD - 1)

    gls = [slice(g * DK_A, (g + 1) * DK_A) for g in range(G_A)]
    q4s, o_cmps = [], []
    for g in range(G_A):
        q4 = jnp.concatenate(
            [q_ref[:, (g * HPG_A + h) * DK_A:(g * HPG_A + h + 1) * DK_A] for h in range(HPG_A)], axis=0)
        q4 = (q4.astype(F32) * (LOG2E * DK_A ** -0.5)).astype(BF16)
        q4s.append(q4)

        s = biased(_nt_dot(kc_ref[0, g], q4),
                   lambda h: jnp.where(valid_cmp, -slopes[g][h] * d_cmp, -jnp.inf))
        mx = jnp.max(s, axis=0, keepdims=True)
        mx = jnp.where(mx > -jnp.inf, mx, 0.0)
        e = jnp.exp2(s - mx)
        p = e * (1.0 / jnp.maximum(jnp.sum(e, axis=0, keepdims=True), 1e-30))
        o_cmps.append(_dot(vct_ref[0][gls[g], :], p.astype(BF16)))

        psum = p[:, hs[0]] + p[:, hs[1]] + p[:, hs[2]] + p[:, hs[3]]
        p1, p2, p3 = _split3(psum)
        imp = _dot(ov_t, p1) + _dot(ov_t, p2) + _dot(ov_t, p3)
        score = jnp.where(allowed, jnp.where(forced, BIG, imp), -BIG)
        tiles = [score[r * SUBLANE:(r + 1) * SUBLANE] for r in range(N_BLK // SUBLANE)]
        cnt = [jnp.zeros((SUBLANE, QB), F32) for _ in tiles]
        for j in range(N_BLK):
            sj = score[j:j + 1, :]
            for r, tile in enumerate(tiles):
                lo = r * SUBLANE
                if j >= lo + SUBLANE:
                    beats = sj > tile
                elif j < lo:
                    beats = sj >= tile
                else:
                    beats = (sj > tile) | ((sj >= tile) & (jb8 > j - lo))
                cnt[r] = cnt[r] + jnp.where(beats, 1.0, 0.0)
        sel_sc[g] = jnp.where((jnp.concatenate(cnt, axis=0) < N_SEL) & allowed, 1.0, 0.0)

        s_w, v_w = [], []
        for j in range(n_wchunk):
            c = qb * n_qtile - WINDOW // LANE + j
            k0 = pl.multiple_of(jnp.maximum(c, 0) * LANE, LANE)
            slot = jnp.where(c >= 0, j, n_wchunk)
            sj = _nt_dot(k_ref[pl.ds(k0, LANE), LANE + g * DK_A:LANE + (g + 1) * DK_A], q4)
            s_w.append(biased(sj, lambda h: bw_sc[g, h * n_wslot + slot]))
            v_w.append(vt_ref[G_A * LANE + g * LANE:G_A * LANE + (g + 1) * LANE, pl.ds(k0, LANE)]
                       + ones_row[:, :LANE])
        mx = s_w[0]
        for j in range(1, n_wchunk):
            mx = jnp.maximum(mx, s_w[j])
        mx = jnp.max(mx, axis=0, keepdims=True)
        o_win = _dot(v_w[0], jnp.exp2((s_w[0] - mx).astype(BF16)))
        for j in range(1, n_wchunk):
            o_win = o_win + _dot(v_w[j], jnp.exp2((s_w[j] - mx).astype(BF16)))
        ow_sc[g] = o_win

    chosen_any = jnp.maximum(sel_sc[0], sel_sc[1])
    blk_per_chunk = CK // L_SEL
    n_causal = ((qb + 1) * QB - 1) // CK + 1
    n_chunk = jnp.int32(0)
    for c in range(s_sc.shape[1]):
        need = (jnp.max(chosen_any[c * blk_per_chunk:(c + 1) * blk_per_chunk, :]) > 0.5) & (c < n_causal)

        @pl.when(need)
        def _():
            idx_sc[n_chunk] = c

        n_chunk = n_chunk + jnp.where(need, 1, 0)

    def score_chunk(i, m_runs):
        c = idx_sc[i]
        k0 = pl.multiple_of(c * CK, CK)
        out = []
        for g in range(G_A):
            sc = _nt_dot(k_ref[pl.ds(k0, CK), gls[g]], q4s[g])
            halves = []
            for u in range(CK // LANE):
                dist = tk + (t0 - k0 - u * LANE).astype(F32)
                blk0 = (CK // L_SEL) * c + (LANE // L_SEL) * u
                chosen = jnp.concatenate(
                    [jnp.broadcast_to(sel_sc[g, pl.ds(blk0 + b, 1), :], (L_SEL, QB))
                     for b in range(LANE // L_SEL)], axis=0)
                ok = (chosen > 0.5) & (dist >= 0.0)
                far = jnp.where(ok, dist, -NEG)
                halves.append(biased(sc[u * LANE:(u + 1) * LANE], lambda h: far * -slopes[g][h]))
            sc = jnp.concatenate(halves, axis=0)
            s_sc[g, i] = sc
            out.append(jnp.maximum(m_runs[g], jnp.max(sc.reshape(CK // SUBLANE, SUBLANE, cols), axis=0)))
        return tuple(out)

    m_runs = lax.fori_loop(0, n_chunk, score_chunk, (jnp.full((SUBLANE, cols), NEG, F32),) * G_A)
    m_rows = [jnp.max(m_runs[g], axis=0, keepdims=True) for g in range(G_A)]
    acc_sc[...] = jnp.zeros(acc_sc.shape, F32)

    def value_chunk(i, carry):
        k0 = pl.multiple_of(idx_sc[i] * CK, CK)
        for g in range(G_A):
            pc = jnp.exp2((s_sc[g, i] - m_rows[g]).astype(BF16))
            acc_sc[g] += _dot(vt_ref[g * LANE:(g + 1) * LANE, pl.ds(k0, CK)] + ones_row, pc)
        return carry

    lax.fori_loop(0, n_chunk, value_chunk, 0)

    pieces = []
    for g in range(G_A):
        o_sel, o_win = acc_sc[g], ow_sc[g]
        for h in range(HPG_A):
            col = g * HPG_A + h
            os_h, ow_h = o_sel[:, hs[h]], o_win[:, hs[h]]
            w_sel = gates_t[NH_A + col:NH_A + col + 1, :] / jnp.maximum(os_h[DK_A:DK_A + 1, :], 1e-30)
            w_win = gates_t[2 * NH_A + col:2 * NH_A + col + 1, :] / jnp.maximum(ow_h[DK_A:DK_A + 1, :], 1e-30)
            pieces.append(gates_t[col:col + 1, :] * o_cmps[g][:, hs[h]] + w_sel * os_h[:DK_A] + w_win * ow_h[:DK_A])

    out_t = jnp.concatenate(pieces, axis=0)
    out = jnp.concatenate(
        [jnp.concatenate([jnp.transpose(out_t[i * LANE:(i + 1) * LANE, r * LANE:(r + 1) * LANE])
                          for i in range(D_A // LANE)], axis=1) for r in range(n_qtile)], axis=0)
    o_ref[...] = (out * _silu(z_ref[...].astype(F32))).astype(BF16)


def _nsa(main, kk, vt, small, kcmp, vcmp_t, bsz, t_len):
    nqb = t_len // QB
    cols = HPG_A * QB
    n_wslot = (WINDOW + QB) // LANE + 1
    return pl.pallas_call(
        _nsa_kernel,
        grid=(bsz, nqb),
        in_specs=[pl.BlockSpec((QB, D_A), lambda b, i: (b * nqb + i, COL_AQ)),
                  pl.BlockSpec((QB, D_A), lambda b, i: (b * nqb + i, COL_AZ)),
                  pl.BlockSpec((QB, LANE), lambda b, i: (b * nqb + i, 0)),
                  pl.BlockSpec((1, G_A, N_CMP_PAD, DK_A), lambda b, i: (b, 0, 0, 0)),
                  pl.BlockSpec((1, G_A * DK_A, N_CMP_PAD), lambda b, i: (b, 0, 0)),
                  pl.BlockSpec((t_len, K_W), lambda b, i: (b, 0)),
                  pl.BlockSpec((NSA_VT_W, t_len), lambda b, i: (0, b))],
        out_specs=pl.BlockSpec((QB, D_A), lambda b, i: (b * nqb + i, 0)),
        out_shape=jax.ShapeDtypeStruct((bsz * t_len, D_A), BF16),
        scratch_shapes=[pltpu.VMEM((G_A, t_len // CK, CK, cols), F32), pltpu.VMEM((G_A, LANE, cols), F32),
                        pltpu.VMEM((G_A, LANE, cols), F32),
                        pltpu.VMEM((G_A, HPG_A * n_wslot, LANE, QB), F32), pltpu.VMEM((G_A, N_BLK, QB), F32),
                        pltpu.SMEM((t_len // CK,), jnp.int32)],
        compiler_params=_params("parallel", "arbitrary"),
        name="nsa_attention",
    )(main, main, small, kcmp, vcmp_t, kk, vt)


def _sgu_tile(u_ref, v_ref, z_ref, lng_ref, lnb_ref, w_ref, b_ref):
    u = jax.nn.gelu(u_ref[...].astype(F32))
    v = jax.nn.gelu(v_ref[...].astype(F32))
    mu = jnp.mean(v, axis=-1, keepdims=True)
    var = jnp.mean(jnp.square(v - mu), axis=-1, keepdims=True)
    vn = ((v - mu) * lax.rsqrt(var + LN_EPS) * lng_ref[...] + lnb_ref[...]).astype(BF16)
    gate = u * _silu(z_ref[...].astype(F32))
    ti = lax.broadcasted_iota(jnp.int32, (CHUNK_B, CHUNK_B), 0)
    si = lax.broadcasted_iota(jnp.int32, (CHUNK_B, CHUNK_B), 1)
    ws = [jnp.where(si <= ti, w_ref[g], 0.0).astype(BF16) for g in range(G_B)]
    rows = []
    for c in range(u_ref.shape[0] // CHUNK_B):
        rs = slice(c * CHUNK_B, (c + 1) * CHUNK_B)
        rows.append(jnp.concatenate(
            [gate[rs, g * LANE:(g + 1) * LANE] * (_dot(ws[g], vn[rs, g * LANE:(g + 1) * LANE]) + b_ref[:, g:g + 1])
             for g in range(G_B)], axis=1))
    return jnp.concatenate(rows, axis=0)


def _mlstm_kernel(q_ref, k_ref, vt_ref, o_ref, z_ref, g_ref, cw_ref, cb_ref, gb_ref, ng_ref, y_ref,
                  qc_sc, kc_sc, halo_sc, st_sc, m_sc):
    sb = pl.program_id(1)
    blk = q_ref.shape[0]

    @pl.when(sb == 0)
    def _():
        halo_sc[...] = jnp.zeros(halo_sc.shape, F32)
        st_sc[...] = jnp.zeros(st_sc.shape, F32)
        m_sc[...] = jnp.zeros(m_sc.shape, F32)

    r8 = lax.broadcasted_iota(jnp.int32, (SUBLANE, 1), 0)

    def conv_silu(x_ref, which, dst_ref, scale):
        x = x_ref[...].astype(F32)
        prev = halo_sc[which]
        w = cw_ref[:, which * D_C:(which + 1) * D_C]
        acc = x * w[CONV_W - 1:CONV_W, :] + cb_ref[:, which * D_C:(which + 1) * D_C]
        for j in range(1, CONV_W):
            rolled = pltpu.roll(x, j, 0)
            head = jnp.where(r8 < j, pltpu.roll(prev, j, 0), rolled[0:SUBLANE])
            shifted = jnp.concatenate([head, rolled[SUBLANE:]], axis=0)
            acc = acc + shifted * w[CONV_W - 1 - j:CONV_W - j, :]
        halo_sc[which] = x[blk - SUBLANE:blk]
        dst_ref[...] = (_silu(acc) * scale).astype(BF16)

    conv_silu(q_ref, 0, qc_sc, 1.0)
    conv_silu(k_ref, 1, kc_sc, DH_C ** -0.5)

    si = lax.broadcasted_iota(jnp.int32, (CHUNK_C, CHUNK_C), 0)
    ti = lax.broadcasted_iota(jnp.int32, (CHUNK_C, CHUNK_C), 1)
    causal = si <= ti
    tri_u = jnp.where(causal, 1.0, 0.0).astype(BF16)
    ones_rows = jnp.where(lax.broadcasted_iota(jnp.int32, (DVX - DH_C, CHUNK_C), 0) == 0, 1.0, 0.0).astype(BF16)
    ng = ng_ref[...]
    gbias = gb_ref[...]

    for c in range(blk // CHUNK_C):
        rs = slice(c * CHUNK_C, (c + 1) * CHUNK_C)
        gate_rows = jnp.transpose(g_ref[rs, :] + gbias)[GI_LANE:GI_LANE + 2 * NH_C]
        f1, f2, f3 = _split3(_log_sigmoid(gate_rows))
        bc_rows = _dot(f1, tri_u) + _dot(f2, tri_u) + _dot(f3, tri_u)
        g_rows = gate_rows[0:NH_C] - bc_rows[NH_C:2 * NH_C]
        g_cols = jnp.transpose(jnp.concatenate([g_rows, jnp.zeros((CHUNK_C - NH_C, CHUNK_C), F32)], axis=0))
        for h in range(NH_C):
            ls = slice(h * DH_C, (h + 1) * DH_C)
            q = qc_sc[rs, ls]
            k = kc_sc[rs, ls]
            vext = jnp.concatenate([vt_ref[ls, rs], ones_rows], axis=0)
            g_col = g_cols[:, h:h + 1]
            g_row = g_rows[h:h + 1, :]
            bc_row = bc_rows[NH_C + h:NH_C + h + 1, :]
            b_last = bc_row[:, CHUNK_C - 1:CHUNK_C]
            m_prev = m_sc[h][:, 0:1]
            st = st_sc[h]
            m_loc = b_last + jnp.max(g_row, axis=1, keepdims=True)
            e_end = jnp.exp(b_last + g_row - m_loc)

            gmat = jnp.where(causal, g_col, -jnp.inf)
            mg = jnp.maximum(m_prev, jnp.max(gmat, axis=0, keepdims=True))
            s_t = _nt_dot(k, q) * jnp.exp(gmat - mg)
            e_int = jnp.exp(m_prev - mg)
            lhs = jnp.concatenate([s_t, jnp.transpose(q.astype(F32)) * e_int], axis=0).astype(BF16)
            rhs = jnp.concatenate([vext, st.astype(BF16)], axis=1)
            both = _dot(rhs, lhs)
            den = both[DH_C:DH_C + 1, :]
            hval = both[:DH_C] * (1.0 / jnp.maximum(jnp.abs(den), jnp.exp(-(bc_row + mg))))
            mu = jnp.mean(hval, axis=0, keepdims=True)
            var = jnp.mean(jnp.square(hval - mu), axis=0, keepdims=True)
            hn = jnp.transpose((hval - mu) * lax.rsqrt(var + LN_EPS))
            o_gate = o_ref[rs, ls].astype(F32)
            z_gate = z_ref[rs, ls].astype(F32)
            gated = hn * ng[:, ls] * _sigmoid(o_gate) * _silu(z_gate)
            y_ref[rs, ls] = gated.astype(BF16)

            loc = _dot((vext.astype(F32) * e_end).astype(BF16), k)
            m_new = jnp.maximum(b_last + m_prev, m_loc)
            st_sc[h] = jnp.exp(b_last + m_prev - m_new) * st + jnp.exp(m_loc - m_new) * loc
            m_sc[h] = jnp.broadcast_to(m_new, (1, LANE))


def _mlstm(main, vt, small, conv_w, conv_b, gbias, norm_g, layer, bsz, t_len):
    nsb = t_len // SEQ_BLK_C
    blk = lambda col: pl.BlockSpec((SEQ_BLK_C, D_C), lambda b, s: (b * nsb + s, col))
    full = lambda a: _layer_spec(a, layer, 2)
    return pl.pallas_call(
        _mlstm_kernel,
        grid=(bsz, nsb),
        in_specs=[blk(COL_CQ), blk(COL_CK),
                  pl.BlockSpec((D_C, SEQ_BLK_C), lambda b, s: (NSA_VT_W // D_C, b * nsb + s)),
                  blk(COL_CO), blk(COL_CZ),
                  pl.BlockSpec((SEQ_BLK_C, LANE), lambda b, s: (b * nsb + s, 0)),
                  full(conv_w), full(conv_b), full(gbias), full(norm_g)],
        out_specs=pl.BlockSpec((SEQ_BLK_C, D_C), lambda b, s: (b * nsb + s, 0)),
        out_shape=jax.ShapeDtypeStruct((bsz * t_len, D_C), BF16),
        scratch_shapes=[pltpu.VMEM((SEQ_BLK_C, D_C), BF16), pltpu.VMEM((SEQ_BLK_C, D_C), BF16),
                        pltpu.VMEM((2, SUBLANE, D_C), F32), pltpu.VMEM((NH_C, DVX, DH_C), F32),
                        pltpu.VMEM((NH_C, 1, LANE), F32)],
        compiler_params=_params("parallel", "arbitrary"),
        name="mlstm",
    )(main, main, vt, main, main, small, conv_w, conv_b, gbias, norm_g)


def _merge_kernel(x_ref, ya_ref, yc_ref, u_ref, v_ref, z_ref, g0_ref, g1_ref, g2_ref, p_ref,
                  sgg_ref, sgb_ref, sgw_ref, sgbt_ref,
                  wa_ref, wb_ref, wc_ref, wo_ref, wp_ref, wg_ref, lng_ref, lnb_ref, o_ref, ob_ref):
    yb = _sgu_tile(u_ref, v_ref, z_ref, sgg_ref, sgb_ref, sgw_ref, sgbt_ref).astype(BF16)
    merged = (_sigmoid(g0_ref[...].astype(F32)) * _dot(ya_ref[...], wa_ref[...])
              + _sigmoid(g1_ref[...].astype(F32)) * _dot(yb, wb_ref[...])
              + _sigmoid(g2_ref[...].astype(F32)) * _dot(yc_ref[...], wc_ref[...]))
    r = ALPHA * x_ref[...] + _dot(merged.astype(BF16), wo_ref[...])
    r = r + _sigmoid(_dot(r.astype(BF16), wg_ref[...])) * _dot(p_ref[...].astype(BF16), wp_ref[...])
    mu = jnp.mean(r, axis=-1, keepdims=True)
    var = jnp.mean(jnp.square(r - mu), axis=-1, keepdims=True)
    y = (r - mu) * lax.rsqrt(var + LN_EPS) * lng_ref[...] + lnb_ref[...]
    o_ref[...] = y
    ob_ref[...] = y.astype(BF16)


def _merge(x, ya, yc, main, p, sgg, sgb, sgw, sgbt, wa, wb, wc, wo, wp, wg, lng, lnb, layer, tm):
    n = x.shape[0]
    row = lambda w: pl.BlockSpec((tm, w), lambda i: (i, 0))
    blk = lambda col: pl.BlockSpec((tm, D_B), lambda i: (i, col))
    gate = lambda j: pl.BlockSpec((tm, D_MODEL), lambda i: (i, j))
    full = lambda a: _layer_spec(a, layer, 1)
    return pl.pallas_call(
        _merge_kernel,
        grid=(n // tm,),
        in_specs=[row(D_MODEL), row(D_A), row(D_C), blk(COL_BU), blk(COL_BV), blk(COL_BZ),
                  gate(0), gate(1), gate(2), pl.BlockSpec((None, tm, PLE_DIM), lambda i: (layer, i, 0)),
                  full(sgg), full(sgb), full(sgw), full(sgbt),
                  full(wa), full(wb), full(wc), full(wo), full(wp), full(wg), full(lng), full(lnb)],
        out_specs=[row(D_MODEL), row(D_MODEL)],
        out_shape=[jax.ShapeDtypeStruct((n, D_MODEL), F32), jax.ShapeDtypeStruct((n, D_MODEL), BF16)],
        compiler_params=_params("parallel"),
        name="merge",
    )(x, ya, yc, main, main, main, main, main, main, p, sgg, sgb, sgw, sgbt, wa, wb, wc, wo, wp, wg, lng, lnb)


def _regroup_kernel(wt_ref, main_ref, k_ref, v_ref, aux_ref):
    wt = wt_ref[...]
    cols = wt.shape[1]
    run = lambda first, last: wt[SEG[first][0]:SEG[last][1]]
    zeros = lambda n: jnp.zeros((n, cols), F32)
    main_ref[...] = jnp.concatenate([run('m_g', 'm_g'), run('a_q', 'a_q'), run('a_z', 'c_k'), run('c_o', 'c_z')],
                                    axis=0).astype(BF16)
    k_ref[...] = jnp.concatenate([run('a_ks', 'a_ks'), run('a_kw', 'a_kw')], axis=0).astype(BF16)
    v_rows = []
    for name in ('a_vs', 'a_vw'):
        for g in range(G_A):
            lo = SEG[name][0] + g * DK_A
            v_rows += [wt[lo:lo + DK_A], zeros(LANE - DK_A)]
    v_ref[...] = jnp.concatenate(v_rows + [run('c_v', 'c_v')], axis=0).astype(BF16)
    n_ag = SEG['a_g'][1] - SEG['a_g'][0]
    aux_ref[...] = jnp.concatenate([run('a_kc', 'a_vc'), run('a_g', 'a_g'), zeros(GI_LANE - n_ag), run('c_if', 'c_if'),
                                    zeros(LANE - GI_LANE - 2 * NH_C)], axis=0).astype(BF16)


def _regroup(w_in_t, cols):
    nl, n_in, d = w_in_t.shape
    blk = lambda rows: pl.BlockSpec((None, rows, cols), lambda l, c: (l, 0, c))
    return pl.pallas_call(
        _regroup_kernel,
        grid=(nl, d // cols),
        in_specs=[blk(n_in)],
        out_specs=[blk(MAIN_W), blk(K_W), blk(VT_W), blk(3 * LANE)],
        out_shape=[jax.ShapeDtypeStruct((nl, MAIN_W, d), BF16), jax.ShapeDtypeStruct((nl, K_W, d), BF16),
                   jax.ShapeDtypeStruct((nl, VT_W, d), BF16), jax.ShapeDtypeStruct((nl, 3 * LANE, d), BF16)],
        compiler_params=_params("parallel", "parallel"),
        name="regroup_w_in",
    )(w_in_t)


def _prepare(w_in, cmp_wv2, sg_ln_g, sg_ln_b, sg_b,
             ml_conv_b, ml_b_i, ml_b_f, ml_norm_g, w_br_a, w_br_b, w_br_c, w_out, ple_w, ple_gate, ln_g, ln_b):
    nl = w_in.shape[0]
    prm = {}
    prm['w_main'], prm['w_k'], prm['w_vt'], prm['w_aux'] = _regroup(jnp.swapaxes(w_in, 1, 2), 256)
    prm['wv2t'] = jnp.swapaxes(cmp_wv2, 1, 2)
    prm['sg_ln_g'], prm['sg_ln_b'] = sg_ln_g[:, None, :], sg_ln_b[:, None, :]
    prm['sg_b_t'] = jnp.swapaxes(sg_b, 1, 2)
    gbias = jnp.zeros((nl, 1, LANE), F32).at[:, 0, GI_LANE:GI_LANE + NH_C].set(ml_b_i)
    prm['gbias'] = gbias.at[:, 0, GF_LANE:GF_LANE + NH_C].set(ml_b_f)
    prm['conv_b'], prm['norm_g'] = ml_conv_b[:, None, :], ml_norm_g[:, None, :]
    for name, a in (('wa', w_br_a), ('wb', w_br_b), ('wc', w_br_c), ('wo', w_out), ('wp', ple_w), ('wg', ple_gate)):
        prm[name] = a.astype(BF16)
    prm['ln_g'], prm['ln_b'] = ln_g[:, None, :], ln_b[:, None, :]
    return prm


def _layer(i, x, xin, p, prm, cmp, sg_w, ml_conv_w, bsz, t_len):
    main = _matmul(xin, prm['w_main'], i, 2048, 1536, "proj_main")
    kk, vt, kc, vc, small = _tail_matmul(xin, prm['w_k'], prm['w_vt'], prm['w_aux'], i, 1024)

    pos_k, pos_v, wk1, wk2, wv1 = cmp
    kcmp, vcmp_t = _compress(kc, vc, pos_k, pos_v, wk1, wk2, wv1, prm['wv2t'], i, bsz, t_len)
    ya = _nsa(main, kk, vt, small, kcmp, vcmp_t, bsz, t_len)
    yc = _mlstm(main, vt, small, ml_conv_w, prm['conv_b'], prm['gbias'], prm['norm_g'], i, bsz, t_len)
    return _merge(x, ya, yc, main, p, prm['sg_ln_g'], prm['sg_ln_b'], sg_w, prm['sg_b_t'],
                  prm['wa'], prm['wb'], prm['wc'], prm['wo'], prm['wp'], prm['wg'], prm['ln_g'], prm['ln_b'], i, 512)


def kernel(x, p, w_in, cmp_pos_k, cmp_pos_v, cmp_wk1, cmp_wk2, cmp_wv1, cmp_wv2, sg_ln_g, sg_ln_b, sg_w, sg_b,
           ml_conv_w, ml_conv_b, ml_b_i, ml_b_f, ml_norm_g, w_br_a, w_br_b, w_br_c, w_out, ple_w, ple_gate,
           ln_g, ln_b):
    bsz, t_len, d = x.shape
    assert d == D_MODEL and t_len == N_CMP_PAD * STRIDE_CMP and t_len // L_SEL == N_BLK
    assert t_len % SEQ_BLK_C == 0 and t_len % CK == 0 and w_in.shape[0] == DEPTH
    prm = _prepare(w_in, cmp_wv2, sg_ln_g, sg_ln_b, sg_b,
                   ml_conv_b, ml_b_i, ml_b_f, ml_norm_g, w_br_a, w_br_b, w_br_c, w_out, ple_w, ple_gate, ln_g, ln_b)
    xf = x.reshape(bsz * t_len, d)
    xin = xf
    pf = p.reshape(DEPTH, bsz * t_len, PLE_DIM)
    for i in range(DEPTH):
        xf, xin = _layer(i, xf, xin, pf, prm, (cmp_pos_k, cmp_pos_v, cmp_wk1, cmp_wk2, cmp_wv1), sg_w, ml_conv_w,
                         bsz, t_len)
    return xf.reshape(bsz, t_len, d)
```

```python
import jax
import jax.numpy as jnp
from jax import lax
from jax.experimental import pallas as pl
from jax.experimental.pallas import tpu as pltpu

F32 = jnp.float32
BF16 = jnp.bfloat16

D_MODEL = 1024
PLE_DIM = 256
D_A = 512
NH_A = 8
G_A = 2
HPG_A = 4
DK_A = 64
L_CMP = 32
STRIDE_CMP = 16
CMP_HID = 128
L_SEL = 64
N_SEL = 8
N_BLK = 32
WINDOW = 256
QB = 256
CK = 256
N_CMP_PAD = 128
BIG = 1e9
D_B = 512
CHUNK_B = 128
G_B = 4
D_C = 512
NH_C = 4
DH_C = 128
CONV_W = 4
CHUNK_C = 128
SEQ_BLK_C = 512
LN_EPS = 1e-5
DEPTH = 2
ALPHA = (2.0 * DEPTH) ** 0.25

LANE = 128
SUBLANE = 8
_SEG_SIZES = (('a_q', D_A), ('a_kc', G_A * DK_A), ('a_vc', G_A * DK_A), ('a_ks', G_A * DK_A), ('a_vs', G_A * DK_A),
              ('a_kw', G_A * DK_A), ('a_vw', G_A * DK_A), ('a_g', 3 * NH_A), ('a_z', D_A),
              ('b_u', D_B), ('b_v', D_B), ('b_z', D_B),
              ('c_q', D_C), ('c_k', D_C), ('c_v', D_C), ('c_if', 2 * NH_C), ('c_o', D_C), ('c_z', D_C),
              ('m_g', 3 * D_MODEL))
SEG = {}
for _name, _size in _SEG_SIZES:
    _lo = sum(s for _, s in _SEG_SIZES[:len(SEG)])
    SEG[_name] = (_lo, _lo + _size)
MAIN_W = 7680
COL_AQ, COL_AZ, COL_BU, COL_BV, COL_BZ, COL_CQ, COL_CK, COL_CO, COL_CZ = 6, 7, 8, 9, 10, 11, 12, 13, 14
K_W = 256
NSA_VT_W = 512
VT_W = NSA_VT_W + 512
DVX = 128 + 16
GI_LANE = 32
GF_LANE = 36
V7X_VMEM_BYTES = 64 * 1024 * 1024
VMEM_LIMIT = V7X_VMEM_BYTES * 13 // 16
NEG = -1e30
LOG2E = 1.4426950408889634


def _params(*sem):
    return pltpu.CompilerParams(dimension_semantics=sem, vmem_limit_bytes=VMEM_LIMIT)


def _nt_dot(a, b):
    return lax.dot_general(a, b, (((1,), (1,)), ((), ())), preferred_element_type=F32)


def _dot(a, b):
    return jnp.dot(a, b, preferred_element_type=F32)


def _split3(a):
    a1 = a.astype(BF16)
    r1 = a - a1.astype(F32)
    a2 = r1.astype(BF16)
    a3 = (r1 - a2.astype(F32)).astype(BF16)
    return a1, a2, a3


def _sigmoid(x):
    return 0.5 * jnp.tanh(0.5 * x) + 0.5


def _silu(x):
    return x * _sigmoid(x)


def _log_sigmoid(x):
    return jnp.minimum(x, 0.0) - jnp.log1p(jnp.exp(-jnp.abs(x)))


def _layer_spec(a, layer, grid_rank):
    zeros = (0,) * (a.ndim - 1)
    if grid_rank == 1:
        return pl.BlockSpec((None,) + a.shape[1:], lambda i: (layer,) + zeros)
    return pl.BlockSpec((None,) + a.shape[1:], lambda i, j: (layer,) + zeros)


def _mm_kernel(x_ref, wt_ref, o_ref):
    o_ref[...] = _nt_dot(x_ref[...].astype(BF16), wt_ref[...]).astype(o_ref.dtype)


def _matmul(x, wt, layer, tm, tn, name):
    m, k = x.shape
    n = wt.shape[1]
    return pl.pallas_call(
        _mm_kernel,
        grid=(m // tm, n // tn),
        in_specs=[pl.BlockSpec((tm, k), lambda i, j: (i, 0)),
                  pl.BlockSpec((None, tn, k), lambda i, j: (layer, j, 0))],
        out_specs=pl.BlockSpec((tm, tn), lambda i, j: (i, j)),
        out_shape=jax.ShapeDtypeStruct((m, n), BF16),
        compiler_params=_params("parallel", "arbitrary"),
        name=name,
    )(x, wt)


def _tail_kernel(x_ref, wkt_ref, wvt_ref, wauxt_ref, ok_ref, ovt_ref, okc_ref, ovc_ref, osm_ref):
    x = x_ref[...].astype(BF16)
    ok_ref[...] = _nt_dot(x, wkt_ref[...]).astype(BF16)
    ovt_ref[...] = _nt_dot(wvt_ref[...], x).astype(BF16)
    aux = _nt_dot(x, wauxt_ref[...])
    okc_ref[...] = aux[:, 0:LANE]
    ovc_ref[...] = aux[:, LANE:2 * LANE]
    osm_ref[...] = aux[:, 2 * LANE:3 * LANE]


def _tail_matmul(x, wk, wvt, waux, layer, tm):
    m, k = x.shape
    tok = lambda w: pl.BlockSpec((tm, w), lambda i: (i, 0))
    f32_out = jax.ShapeDtypeStruct((m, LANE), F32)
    return pl.pallas_call(
        _tail_kernel,
        grid=(m // tm,),
        in_specs=[tok(k), _layer_spec(wk, layer, 1), _layer_spec(wvt, layer, 1), _layer_spec(waux, layer, 1)],
        out_specs=[tok(K_W), pl.BlockSpec((VT_W, tm), lambda i: (0, i)), tok(LANE), tok(LANE), tok(LANE)],
        out_shape=[jax.ShapeDtypeStruct((m, K_W), BF16), jax.ShapeDtypeStruct((VT_W, m), BF16),
                   f32_out, f32_out, f32_out],
        compiler_params=_params("parallel"),
        name="proj_tail",
    )(x, wk, wvt, waux)


def _compress_kernel(zk_ref, zv_ref, pk_ref, pv_ref, wk1_ref, wk2_ref, wv1_ref, wv2t_ref, ko_ref, vot_ref):
    n = zk_ref.shape[0] // STRIDE_CMP

    def hidden(z_ref, p_ref, w1_ref):
        r0 = [jnp.zeros((n, CMP_HID), F32) for _ in range(G_A)]
        r1 = [jnp.zeros((n, CMP_HID), F32) for _ in range(G_A)]
        for l in range(STRIDE_CMP):
            zl = z_ref[pl.ds(l, n, stride=STRIDE_CMP), :]
            w_a = w1_ref[l * DK_A:(l + 1) * DK_A, :].astype(BF16)
            w_b = w1_ref[(STRIDE_CMP + l) * DK_A:(STRIDE_CMP + l + 1) * DK_A, :].astype(BF16)
            for g in range(G_A):
                zg = zl[:, g * DK_A:(g + 1) * DK_A]
                r0[g] = r0[g] + _dot((zg + p_ref[l:l + 1, :]).astype(BF16), w_a)
                r1[g] = r1[g] + _dot((zg + p_ref[STRIDE_CMP + l:STRIDE_CMP + l + 1, :]).astype(BF16), w_b)
        return [jax.nn.gelu(r0[g] + pltpu.roll(r1[g], n - 1, 0)).astype(BF16) for g in range(G_A)]

    hk = hidden(zk_ref, pk_ref, wk1_ref)
    wk2 = wk2_ref[...].astype(BF16)
    for g in range(G_A):
        ko_ref[0, g] = _dot(hk[g], wk2).astype(BF16)
    hv = hidden(zv_ref, pv_ref, wv1_ref)
    wv2t = wv2t_ref[...].astype(BF16)
    vot_ref[0] = jnp.concatenate([_nt_dot(wv2t, hv[g]) for g in range(G_A)], axis=0).astype(BF16)


def _compress(kc, vc, pk, pv, wk1, wk2, wv1, wv2t, layer, bsz, t_len):
    n = t_len // STRIDE_CMP
    zspec = pl.BlockSpec((t_len, LANE), lambda i: (i, 0))
    full = lambda a: _layer_spec(a, layer, 1)
    return pl.pallas_call(
        _compress_kernel,
        grid=(bsz,),
        in_specs=[zspec, zspec, full(pk), full(pv), full(wk1), full(wk2), full(wv1), full(wv2t)],
        out_specs=[pl.BlockSpec((1, G_A, n, DK_A), lambda i: (i, 0, 0, 0)),
                   pl.BlockSpec((1, G_A * DK_A, n), lambda i: (i, 0, 0))],
        out_shape=[jax.ShapeDtypeStruct((bsz, G_A, n, DK_A), BF16), jax.ShapeDtypeStruct((bsz, G_A * DK_A, n), BF16)],
        compiler_params=_params("parallel"),
        name="nsa_compress",
    )(kc, vc, pk, pv, wk1, wk2, wv1, wv2t)


def _nsa_kernel(q_ref, z_ref, g_ref, kc_ref, vct_ref, k_ref, vt_ref, o_ref,
                s_sc, acc_sc, ow_sc, bw_sc, sel_sc, idx_sc):
    qb = pl.program_id(1)
    t0 = qb * QB
    cols = HPG_A * QB
    n_qtile = QB // LANE
    n_wchunk = (WINDOW + QB) // LANE
    n_wslot = n_wchunk + 1
    k_i = lax.broadcasted_iota(jnp.int32, (LANE, 1), 0)
    t_i = lax.broadcasted_iota(jnp.int32, (1, QB), 1)
    tk = (t_i - k_i).astype(F32)
    ones_row = jnp.where(lax.broadcasted_iota(jnp.int32, (LANE, CK), 0) == DK_A, 1.0, 0.0).astype(BF16)
    slopes = [[LOG2E * 2.0 ** -(g * HPG_A + h + 1) for h in range(HPG_A)] for g in range(G_A)]
    hs = [slice(h * QB, (h + 1) * QB) for h in range(HPG_A)]

    def biased(s, bias_of_head):
        return jnp.concatenate([s[:, hs[h]] + bias_of_head(h) for h in range(HPG_A)], axis=1)

    @pl.when(qb == 0)
    def _():
        for g in range(G_A):
            for h in range(HPG_A):
                for j in range(n_wchunk):
                    dist = tk + float(WINDOW - j * LANE)
                    ok = (dist >= 0.0) & (dist < WINDOW)
                    bw_sc[g, h * n_wslot + j] = jnp.where(ok, -slopes[g][h] * dist, NEG)
                bw_sc[g, h * n_wslot + n_wchunk] = jnp.full((LANE, QB), NEG, F32)

    gsig = _sigmoid(g_ref[...])
    gates_t = jnp.concatenate([jnp.transpose(gsig[r * LANE:(r + 1) * LANE]) for r in range(n_qtile)], axis=1)

    jb = lax.broadcasted_iota(jnp.int32, (N_BLK, QB), 0)
    cur = (t0 + lax.broadcasted_iota(jnp.int32, (N_BLK, QB), 1)) >> 6
    forced = (jb == 0) | (jb == cur) | (jb == cur - 1)
    allowed = jb <= cur
    ov_j = lax.broadcasted_iota(jnp.int32, (N_BLK, N_CMP_PAD), 0) * L_SEL
    ov_n = lax.broadcasted_iota(jnp.int32, (N_BLK, N_CMP_PAD), 1) * STRIDE_CMP
    ov_t = jnp.where((ov_n < ov_j + L_SEL) & (ov_n + L_CMP > ov_j), 1.0, 0.0).astype(BF16)
    jb8 = lax.broadcasted_iota(jnp.int32, (SUBLANE, QB), 0)

    d_cmp = (t0 - (L_CMP - 1)).astype(F32) + (t_i - STRIDE_CMP * k_i).astype(F32)
    valid_cmp = (d_cmp >= 0.0) & (k_i < N_CMP_PAD - 1)

    gls = [slice(g * DK_A, (g + 1) * DK_A) for g in range(G_A)]
    q4s, o_cmps = [], []
    for g in range(G_A):
        q4 = jnp.concatenate(
            [q_ref[:, (g * HPG_A + h) * DK_A:(g * HPG_A + h + 1) * DK_A] for h in range(HPG_A)], axis=0)
        q4 = (q4.astype(F32) * (LOG2E * DK_A ** -0.5)).astype(BF16)
        q4s.append(q4)

        s = biased(_nt_dot(kc_ref[0, g], q4),
                   lambda h: jnp.where(valid_cmp, -slopes[g][h] * d_cmp, -jnp.inf))
        mx = jnp.max(s, axis=0, keepdims=True)
        mx = jnp.where(mx > -jnp.inf, mx, 0.0)
        e = jnp.exp2(s - mx)
        p = e * (1.0 / jnp.maximum(jnp.sum(e, axis=0, keepdims=True), 1e-30))
        o_cmps.append(_dot(vct_ref[0][gls[g], :], p.astype(BF16)))

        psum = p[:, hs[0]] + p[:, hs[1]] + p[:, hs[2]] + p[:, hs[3]]
        p1, p2, p3 = _split3(psum)
        imp = _dot(ov_t, p1) + _dot(ov_t, p2) + _dot(ov_t, p3)
        score = jnp.where(allowed, jnp.where(forced, BIG, imp), -BIG)
        tiles = [score[r * SUBLANE:(r + 1) * SUBLANE] for r in range(N_BLK // SUBLANE)]
        cnt = [jnp.zeros((SUBLANE, QB), F32) for _ in tiles]
        for j in range(N_BLK):
            sj = score[j:j + 1, :]
            for r, tile in enumerate(tiles):
                lo = r * SUBLANE
                if j >= lo + SUBLANE:
                    beats = sj > tile
                elif j < lo:
                    beats = sj >= tile
                else:
                    beats = (sj > tile) | ((sj >= tile) & (jb8 > j - lo))
                cnt[r] = cnt[r] + jnp.where(beats, 1.0, 0.0)
        sel_sc[g] = jnp.where((jnp.concatenate(cnt, axis=0) < N_SEL) & allowed, 1.0, 0.0)

        for r in range(n_qtile):
            q4_r = jnp.concatenate([q4[h * QB + r * LANE:h * QB + (r + 1) * LANE] for h in range(HPG_A)], axis=0)
            s_w, v_w = [], []
            for j in range(r, r + WINDOW // LANE + 1):
                c = qb * n_qtile - WINDOW // LANE + j
                k0 = pl.multiple_of(jnp.maximum(c, 0) * LANE, LANE)
                slot = jnp.where(c >= 0, j, n_wchunk)
                sj = _nt_dot(k_ref[pl.ds(k0, LANE), LANE + g * DK_A:LANE + (g + 1) * DK_A], q4_r)
                s_w.append(jnp.concatenate(
                    [sj[:, h * LANE:(h + 1) * LANE] + bw_sc[g, h * n_wslot + slot, :, r * LANE:(r + 1) * LANE]
                     for h in range(HPG_A)], axis=1))
                v_w.append(vt_ref[G_A * LANE + g * LANE:G_A * LANE + (g + 1) * LANE, pl.ds(k0, LANE)]
                           + ones_row[:, :LANE])
            mx = s_w[0]
            for sj in s_w[1:]:
                mx = jnp.maximum(mx, sj)
            mx = jnp.max(mx, axis=0, keepdims=True)
            o_win = _dot(v_w[0], jnp.exp2((s_w[0] - mx).astype(BF16)))
            for sj, vj in zip(s_w[1:], v_w[1:]):
                o_win = o_win + _dot(vj, jnp.exp2((sj - mx).astype(BF16)))
            ow_sc[g, r] = o_win

    chosen_any = jnp.maximum(sel_sc[0], sel_sc[1])
    blk_per_chunk = CK // L_SEL
    n_causal = ((qb + 1) * QB - 1) // CK + 1
    n_chunk = jnp.int32(0)
    for c in range(s_sc.shape[1]):
        need = (jnp.max(chosen_any[c * blk_per_chunk:(c + 1) * blk_per_chunk, :]) > 0.5) & (c < n_causal)

        @pl.when(need)
        def _():
            idx_sc[n_chunk] = c

        n_chunk = n_chunk + jnp.where(need, 1, 0)

    def score_chunk(i, m_runs):
        c = idx_sc[i]
        k0 = pl.multiple_of(c * CK, CK)
        out = []
        for g in range(G_A):
            sc = _nt_dot(k_ref[pl.ds(k0, CK), gls[g]], q4s[g])
            halves = []
            for u in range(CK // LANE):
                dist = tk + (t0 - k0 - u * LANE).astype(F32)
                blk0 = (CK // L_SEL) * c + (LANE // L_SEL) * u
                chosen = jnp.concatenate(
                    [jnp.broadcast_to(sel_sc[g, pl.ds(blk0 + b, 1), :], (L_SEL, QB))
                     for b in range(LANE // L_SEL)], axis=0)
                ok = (chosen > 0.5) & (dist >= 0.0)
                far = jnp.where(ok, dist, -NEG)
                halves.append(biased(sc[u * LANE:(u + 1) * LANE], lambda h: far * -slopes[g][h]))
            sc = jnp.concatenate(halves, axis=0)
            s_sc[g, i] = sc
            out.append(jnp.maximum(m_runs[g], jnp.max(sc.reshape(CK // SUBLANE, SUBLANE, cols), axis=0)))
        return tuple(out)

    m_runs = lax.fori_loop(0, n_chunk, score_chunk, (jnp.full((SUBLANE, cols), NEG, F32),) * G_A)
    m_rows = [jnp.max(m_runs[g], axis=0, keepdims=True) for g in range(G_A)]
    acc_sc[...] = jnp.zeros(acc_sc.shape, F32)

    def value_chunk(i, carry):
        k0 = pl.multiple_of(idx_sc[i] * CK, CK)
        for g in range(G_A):
            pc = jnp.exp2((s_sc[g, i] - m_rows[g]).astype(BF16))
            acc_sc[g] += _dot(vt_ref[g * LANE:(g + 1) * LANE, pl.ds(k0, CK)] + ones_row, pc)
        return carry

    lax.fori_loop(0, n_chunk, value_chunk, 0)

    pieces = []
    for g in range(G_A):
        o_sel = acc_sc[g]
        for h in range(HPG_A):
            col = g * HPG_A + h
            os_h = o_sel[:, hs[h]]
            ow_h = jnp.concatenate([ow_sc[g, r, :, h * LANE:(h + 1) * LANE] for r in range(n_qtile)], axis=1)
            w_sel = gates_t[NH_A + col:NH_A + col + 1, :] / jnp.maximum(os_h[DK_A:DK_A + 1, :], 1e-30)
            w_win = gates_t[2 * NH_A + col:2 * NH_A + col + 1, :] / jnp.maximum(ow_h[DK_A:DK_A + 1, :], 1e-30)
            pieces.append(gates_t[col:col + 1, :] * o_cmps[g][:, hs[h]] + w_sel * os_h[:DK_A] + w_win * ow_h[:DK_A])

    out_t = jnp.concatenate(pieces, axis=0)
    out = jnp.concatenate(
        [jnp.concatenate([jnp.transpose(out_t[i * LANE:(i + 1) * LANE, r * LANE:(r + 1) * LANE])
                          for i in range(D_A // LANE)], axis=1) for r in range(n_qtile)], axis=0)
    o_ref[...] = (out * _silu(z_ref[...].astype(F32))).astype(BF16)


def _nsa(main, kk, vt, small, kcmp, vcmp_t, bsz, t_len):
    nqb = t_len // QB
    cols = HPG_A * QB
    n_wslot = (WINDOW + QB) // LANE + 1
    return pl.pallas_call(
        _nsa_kernel,
        grid=(bsz, nqb),
        in_specs=[pl.BlockSpec((QB, D_A), lambda b, i: (b * nqb + i, COL_AQ)),
                  pl.BlockSpec((QB, D_A), lambda b, i: (b * nqb + i, COL_AZ)),
                  pl.BlockSpec((QB, LANE), lambda b, i: (b * nqb + i, 0)),
                  pl.BlockSpec((1, G_A, N_CMP_PAD, DK_A), lambda b, i: (b, 0, 0, 0)),
                  pl.BlockSpec((1, G_A * DK_A, N_CMP_PAD), lambda b, i: (b, 0, 0)),
                  pl.BlockSpec((t_len, K_W), lambda b, i: (b, 0)),
                  pl.BlockSpec((NSA_VT_W, t_len), lambda b, i: (0, b))],
        out_specs=pl.BlockSpec((QB, D_A), lambda b, i: (b * nqb + i, 0)),
        out_shape=jax.ShapeDtypeStruct((bsz * t_len, D_A), BF16),
        scratch_shapes=[pltpu.VMEM((G_A, t_len // CK, CK, cols), F32), pltpu.VMEM((G_A, LANE, cols), F32),
                        pltpu.VMEM((G_A, QB // LANE, LANE, HPG_A * LANE), F32),
                        pltpu.VMEM((G_A, HPG_A * n_wslot, LANE, QB), F32), pltpu.VMEM((G_A, N_BLK, QB), F32),
                        pltpu.SMEM((t_len // CK,), jnp.int32)],
        compiler_params=_params("parallel", "arbitrary"),
        name="nsa_attention",
    )(main, main, small, kcmp, vcmp_t, kk, vt)


def _sgu_tile(u_ref, v_ref, z_ref, lng_ref, lnb_ref, w_ref, b_ref):
    u = jax.nn.gelu(u_ref[...].astype(F32))
    v = jax.nn.gelu(v_ref[...].astype(F32))
    mu = jnp.mean(v, axis=-1, keepdims=True)
    var = jnp.mean(jnp.square(v - mu), axis=-1, keepdims=True)
    vn = ((v - mu) * lax.rsqrt(var + LN_EPS) * lng_ref[...] + lnb_ref[...]).astype(BF16)
    gate = u * _silu(z_ref[...].astype(F32))
    ti = lax.broadcasted_iota(jnp.int32, (CHUNK_B, CHUNK_B), 0)
    si = lax.broadcasted_iota(jnp.int32, (CHUNK_B, CHUNK_B), 1)
    ws = [jnp.where(si <= ti, w_ref[g], 0.0).astype(BF16) for g in range(G_B)]
    rows = []
    for c in range(u_ref.shape[0] // CHUNK_B):
        rs = slice(c * CHUNK_B, (c + 1) * CHUNK_B)
        rows.append(jnp.concatenate(
            [gate[rs, g * LANE:(g + 1) * LANE] * (_dot(ws[g], vn[rs, g * LANE:(g + 1) * LANE]) + b_ref[:, g:g + 1])
             for g in range(G_B)], axis=1))
    return jnp.concatenate(rows, axis=0)


def _mlstm_kernel(q_ref, k_ref, vt_ref, o_ref, z_ref, g_ref, cw_ref, cb_ref, gb_ref, ng_ref, y_ref,
                  qc_sc, kc_sc, halo_sc, st_sc, m_sc):
    sb = pl.program_id(1)
    blk = q_ref.shape[0]

    @pl.when(sb == 0)
    def _():
        halo_sc[...] = jnp.zeros(halo_sc.shape, F32)
        st_sc[...] = jnp.zeros(st_sc.shape, F32)
        m_sc[...] = jnp.zeros(m_sc.shape, F32)

    r8 = lax.broadcasted_iota(jnp.int32, (SUBLANE, 1), 0)

    def conv_silu(x_ref, which, dst_ref, scale):
        x = x_ref[...].astype(F32)
        prev = halo_sc[which]
        w = cw_ref[:, which * D_C:(which + 1) * D_C]
        acc = x * w[CONV_W - 1:CONV_W, :] + cb_ref[:, which * D_C:(which + 1) * D_C]
        for j in range(1, CONV_W):
            rolled = pltpu.roll(x, j, 0)
            head = jnp.where(r8 < j, pltpu.roll(prev, j, 0), rolled[0:SUBLANE])
            shifted = jnp.concatenate([head, rolled[SUBLANE:]], axis=0)
            acc = acc + shifted * w[CONV_W - 1 - j:CONV_W - j, :]
        halo_sc[which] = x[blk - SUBLANE:blk]
        dst_ref[...] = (_silu(acc) * scale).astype(BF16)

    conv_silu(q_ref, 0, qc_sc, 1.0)
    conv_silu(k_ref, 1, kc_sc, DH_C ** -0.5)

    si = lax.broadcasted_iota(jnp.int32, (CHUNK_C, CHUNK_C), 0)
    ti = lax.broadcasted_iota(jnp.int32, (CHUNK_C, CHUNK_C), 1)
    causal = si <= ti
    tri_u = jnp.where(causal, 1.0, 0.0).astype(BF16)
    ones_rows = jnp.where(lax.broadcasted_iota(jnp.int32, (DVX - DH_C, CHUNK_C), 0) == 0, 1.0, 0.0).astype(BF16)
    ng = ng_ref[...]
    gbias = gb_ref[...]

    for c in range(blk // CHUNK_C):
        rs = slice(c * CHUNK_C, (c + 1) * CHUNK_C)
        gate_rows = jnp.transpose(g_ref[rs, :] + gbias)[GI_LANE:GI_LANE + 2 * NH_C]
        f1, f2, f3 = _split3(_log_sigmoid(gate_rows))
        bc_rows = _dot(f1, tri_u) + _dot(f2, tri_u) + _dot(f3, tri_u)
        g_rows = gate_rows[0:NH_C] - bc_rows[NH_C:2 * NH_C]
        g_cols = jnp.transpose(jnp.concatenate([g_rows, jnp.zeros((CHUNK_C - NH_C, CHUNK_C), F32)], axis=0))
        for h in range(NH_C):
            ls = slice(h * DH_C, (h + 1) * DH_C)
            q = qc_sc[rs, ls]
            k = kc_sc[rs, ls]
            vext = jnp.concatenate([vt_ref[ls, rs], ones_rows], axis=0)
            g_col = g_cols[:, h:h + 1]
            g_row = g_rows[h:h + 1, :]
            bc_row = bc_rows[NH_C + h:NH_C + h + 1, :]
            b_last = bc_row[:, CHUNK_C - 1:CHUNK_C]
            m_prev = m_sc[h][:, 0:1]
            st = st_sc[h]
            m_loc = b_last + jnp.max(g_row, axis=1, keepdims=True)
            e_end = jnp.exp(b_last + g_row - m_loc)

            gmat = jnp.where(causal, g_col, -jnp.inf)
            mg = jnp.maximum(m_prev, jnp.max(gmat, axis=0, keepdims=True))
            s_t = _nt_dot(k, q) * jnp.exp(gmat - mg)
            e_int = jnp.exp(m_prev - mg)
            lhs = jnp.concatenate([s_t, jnp.transpose(q.astype(F32)) * e_int], axis=0).astype(BF16)
            rhs = jnp.concatenate([vext, st.astype(BF16)], axis=1)
            both = _dot(rhs, lhs)
            den = both[DH_C:DH_C + 1, :]
            hval = both[:DH_C] * (1.0 / jnp.maximum(jnp.abs(den), jnp.exp(-(bc_row + mg))))
            mu = jnp.mean(hval, axis=0, keepdims=True)
            var = jnp.mean(jnp.square(hval - mu), axis=0, keepdims=True)
            hn = jnp.transpose((hval - mu) * lax.rsqrt(var + LN_EPS))
            o_gate = o_ref[rs, ls].astype(F32)
            z_gate = z_ref[rs, ls].astype(F32)
            gated = hn * ng[:, ls] * _sigmoid(o_gate) * _silu(z_gate)
            y_ref[rs, ls] = gated.astype(BF16)

            loc = _dot((vext.astype(F32) * e_end).astype(BF16), k)
            m_new = jnp.maximum(b_last + m_prev, m_loc)
            st_sc[h] = jnp.exp(b_last + m_prev - m_new) * st + jnp.exp(m_loc - m_new) * loc
            m_sc[h] = jnp.broadcast_to(m_new, (1, LANE))


def _mlstm(main, vt, small, conv_w, conv_b, gbias, norm_g, layer, bsz, t_len):
    nsb = t_len // SEQ_BLK_C
    blk = lambda col: pl.BlockSpec((SEQ_BLK_C, D_C), lambda b, s: (b * nsb + s, col))
    full = lambda a: _layer_spec(a, layer, 2)
    return pl.pallas_call(
        _mlstm_kernel,
        grid=(bsz, nsb),
        in_specs=[blk(COL_CQ), blk(COL_CK),
                  pl.BlockSpec((D_C, SEQ_BLK_C), lambda b, s: (NSA_VT_W // D_C, b * nsb + s)),
                  blk(COL_CO), blk(COL_CZ),
                  pl.BlockSpec((SEQ_BLK_C, LANE), lambda b, s: (b * nsb + s, 0)),
                  full(conv_w), full(conv_b), full(gbias), full(norm_g)],
        out_specs=pl.BlockSpec((SEQ_BLK_C, D_C), lambda b, s: (b * nsb + s, 0)),
        out_shape=jax.ShapeDtypeStruct((bsz * t_len, D_C), BF16),
        scratch_shapes=[pltpu.VMEM((SEQ_BLK_C, D_C), BF16), pltpu.VMEM((SEQ_BLK_C, D_C), BF16),
                        pltpu.VMEM((2, SUBLANE, D_C), F32), pltpu.VMEM((NH_C, DVX, DH_C), F32),
                        pltpu.VMEM((NH_C, 1, LANE), F32)],
        compiler_params=_params("parallel", "arbitrary"),
        name="mlstm",
    )(main, main, vt, main, main, small, conv_w, conv_b, gbias, norm_g)


def _merge_kernel(x_ref, ya_ref, yc_ref, u_ref, v_ref, z_ref, g0_ref, g1_ref, g2_ref, p_ref,
                  sgg_ref, sgb_ref, sgw_ref, sgbt_ref,
                  wa_ref, wb_ref, wc_ref, wo_ref, wp_ref, wg_ref, lng_ref, lnb_ref, o_ref, *ob_ref):
    yb = _sgu_tile(u_ref, v_ref, z_ref, sgg_ref, sgb_ref, sgw_ref, sgbt_ref).astype(BF16)
    merged = (_sigmoid(g0_ref[...].astype(F32)) * _dot(ya_ref[...], wa_ref[...])
              + _sigmoid(g1_ref[...].astype(F32)) * _dot(yb, wb_ref[...])
              + _sigmoid(g2_ref[...].astype(F32)) * _dot(yc_ref[...], wc_ref[...]))
    r = ALPHA * x_ref[...] + _dot(merged.astype(BF16), wo_ref[...])
    r = r + _sigmoid(_dot(r.astype(BF16), wg_ref[...])) * _dot(p_ref[...].astype(BF16), wp_ref[...])
    mu = jnp.mean(r, axis=-1, keepdims=True)
    var = jnp.mean(jnp.square(r - mu), axis=-1, keepdims=True)
    y = (r - mu) * lax.rsqrt(var + LN_EPS) * lng_ref[...] + lnb_ref[...]
    o_ref[...] = y
    for ref in ob_ref:
        ref[...] = y.astype(BF16)


def _merge(x, ya, yc, main, p, sgg, sgb, sgw, sgbt, wa, wb, wc, wo, wp, wg, lng, lnb, layer, tm):
    n = x.shape[0]
    n_out = 1 if layer == DEPTH - 1 else 2
    row = lambda w: pl.BlockSpec((tm, w), lambda i: (i, 0))
    blk = lambda col: pl.BlockSpec((tm, D_B), lambda i: (i, col))
    gate = lambda j: pl.BlockSpec((tm, D_MODEL), lambda i: (i, j))
    full = lambda a: _layer_spec(a, layer, 1)
    return pl.pallas_call(
        _merge_kernel,
        grid=(n // tm,),
        in_specs=[row(D_MODEL), row(D_A), row(D_C), blk(COL_BU), blk(COL_BV), blk(COL_BZ),
                  gate(0), gate(1), gate(2), pl.BlockSpec((None, tm, PLE_DIM), lambda i: (layer, i, 0)),
                  full(sgg), full(sgb), full(sgw), full(sgbt),
                  full(wa), full(wb), full(wc), full(wo), full(wp), full(wg), full(lng), full(lnb)],
        out_specs=[row(D_MODEL)] * n_out,
        out_shape=[jax.ShapeDtypeStruct((n, D_MODEL), F32), jax.ShapeDtypeStruct((n, D_MODEL), BF16)][:n_out],
        compiler_params=_params("parallel"),
        name="merge",
    )(x, ya, yc, main, main, main, main, main, main, p, sgg, sgb, sgw, sgbt, wa, wb, wc, wo, wp, wg, lng, lnb)


def _regroup_kernel(wt_ref, main_ref, k_ref, v_ref, aux_ref):
    wt = wt_ref[...]
    cols = wt.shape[1]
    run = lambda first, last: wt[SEG[first][0]:SEG[last][1]]
    zeros = lambda n: jnp.zeros((n, cols), F32)
    main_ref[...] = jnp.concatenate([run('m_g', 'm_g'), run('a_q', 'a_q'), run('a_z', 'c_k'), run('c_o', 'c_z')],
                                    axis=0).astype(BF16)
    k_ref[...] = jnp.concatenate([run('a_ks', 'a_ks'), run('a_kw', 'a_kw')], axis=0).astype(BF16)
    v_rows = []
    for name in ('a_vs', 'a_vw'):
        for g in range(G_A):
            lo = SEG[name][0] + g * DK_A
            v_rows += [wt[lo:lo + DK_A], zeros(LANE - DK_A)]
    v_ref[...] = jnp.concatenate(v_rows + [run('c_v', 'c_v')], axis=0).astype(BF16)
    n_ag = SEG['a_g'][1] - SEG['a_g'][0]
    aux_ref[...] = jnp.concatenate([run('a_kc', 'a_vc'), run('a_g', 'a_g'), zeros(GI_LANE - n_ag), run('c_if', 'c_if'),
                                    zeros(LANE - GI_LANE - 2 * NH_C)], axis=0).astype(BF16)


def _regroup(w_in_t, cols):
    nl, n_in, d = w_in_t.shape
    blk = lambda rows: pl.BlockSpec((None, rows, cols), lambda l, c: (l, 0, c))
    return pl.pallas_call(
        _regroup_kernel,
        grid=(nl, d // cols),
        in_specs=[blk(n_in)],
        out_specs=[blk(MAIN_W), blk(K_W), blk(VT_W), blk(3 * LANE)],
        out_shape=[jax.ShapeDtypeStruct((nl, MAIN_W, d), BF16), jax.ShapeDtypeStruct((nl, K_W, d), BF16),
                   jax.ShapeDtypeStruct((nl, VT_W, d), BF16), jax.ShapeDtypeStruct((nl, 3 * LANE, d), BF16)],
        compiler_params=_params("parallel", "parallel"),
        name="regroup_w_in",
    )(w_in_t)


def _prepare(w_in, cmp_wv2, sg_ln_g, sg_ln_b, sg_b,
             ml_conv_b, ml_b_i, ml_b_f, ml_norm_g, w_br_a, w_br_b, w_br_c, w_out, ple_w, ple_gate, ln_g, ln_b):
    nl = w_in.shape[0]
    prm = {}
    prm['w_main'], prm['w_k'], prm['w_vt'], prm['w_aux'] = _regroup(jnp.swapaxes(w_in, 1, 2), 256)
    prm['wv2t'] = jnp.swapaxes(cmp_wv2, 1, 2)
    prm['sg_ln_g'], prm['sg_ln_b'] = sg_ln_g[:, None, :], sg_ln_b[:, None, :]
    prm['sg_b_t'] = jnp.swapaxes(sg_b, 1, 2)
    gbias = jnp.zeros((nl, 1, LANE), F32).at[:, 0, GI_LANE:GI_LANE + NH_C].set(ml_b_i)
    prm['gbias'] = gbias.at[:, 0, GF_LANE:GF_LANE + NH_C].set(ml_b_f)
    prm['conv_b'], prm['norm_g'] = ml_conv_b[:, None, :], ml_norm_g[:, None, :]
    for name, a in (('wa', w_br_a), ('wb', w_br_b), ('wc', w_br_c), ('wo', w_out), ('wp', ple_w), ('wg', ple_gate)):
        prm[name] = a.astype(BF16)
    prm['ln_g'], prm['ln_b'] = ln_g[:, None, :], ln_b[:, None, :]
    return prm


def _layer(i, x, xin, p, prm, cmp, sg_w, ml_conv_w, bsz, t_len):
    main = _matmul(xin, prm['w_main'], i, 2048, 1536, "proj_main")
    kk, vt, kc, vc, small = _tail_matmul(xin, prm['w_k'], prm['w_vt'], prm['w_aux'], i, 2048)

    pos_k, pos_v, wk1, wk2, wv1 = cmp
    kcmp, vcmp_t = _compress(kc, vc, pos_k, pos_v, wk1, wk2, wv1, prm['wv2t'], i, bsz, t_len)
    ya = _nsa(main, kk, vt, small, kcmp, vcmp_t, bsz, t_len)
    yc = _mlstm(main, vt, small, ml_conv_w, prm['conv_b'], prm['gbias'], prm['norm_g'], i, bsz, t_len)
    return _merge(x, ya, yc, main, p, prm['sg_ln_g'], prm['sg_ln_b'], sg_w, prm['sg_b_t'],
                  prm['wa'], prm['wb'], prm['wc'], prm['wo'], prm['wp'], prm['wg'], prm['ln_g'], prm['ln_b'], i, 512)


def kernel(x, p, w_in, cmp_pos_k, cmp_pos_v, cmp_wk1, cmp_wk2, cmp_wv1, cmp_wv2, sg_ln_g, sg_ln_b, sg_w, sg_b,
           ml_conv_w, ml_conv_b, ml_b_i, ml_b_f, ml_norm_g, w_br_a, w_br_b, w_br_c, w_out, ple_w, ple_gate,
           ln_g, ln_b):
    bsz, t_len, d = x.shape
    assert d == D_MODEL and t_len == N_CMP_PAD * STRIDE_CMP and t_len // L_SEL == N_BLK
    assert t_len % SEQ_BLK_C == 0 and t_len % CK == 0 and w_in.shape[0] == DEPTH
    prm = _prepare(w_in, cmp_wv2, sg_ln_g, sg_ln_b, sg_b,
                   ml_conv_b, ml_b_i, ml_b_f, ml_norm_g, w_br_a, w_br_b, w_br_c, w_out, ple_w, ple_gate, ln_g, ln_b)
    xf = x.reshape(bsz * t_len, d)
    xin = xf
    pf = p.reshape(DEPTH, bsz * t_len, PLE_DIM)
    for i in range(DEPTH):
        outs = _layer(i, xf, xin, pf, prm, (cmp_pos_k, cmp_pos_v, cmp_wk1, cmp_wk2, cmp_wv1), sg_w, ml_conv_w,
                      bsz, t_len)
        xf, xin = outs[0], outs[-1]
    return xf.reshape(bsz, t_len, d)
```

```python
import jax
import jax.numpy as jnp
from jax import lax
from jax.experimental import pallas as pl
from jax.experimental.pallas import tpu as pltpu

F32 = jnp.float32
BF16 = jnp.bfloat16

D_MODEL = 1024
PLE_DIM = 256
D_A = 512
NH_A = 8
G_A = 2
HPG_A = 4
DK_A = 64
L_CMP = 32
STRIDE_CMP = 16
CMP_HID = 128
L_SEL = 64
N_SEL = 8
N_BLK = 32
WINDOW = 256
QB = 256
CK = 256
N_CMP_PAD = 128
BIG = 1e9
D_B = 512
CHUNK_B = 128
G_B = 4
D_C = 512
NH_C = 4
DH_C = 128
CONV_W = 4
CHUNK_C = 128
SEQ_BLK_C = 1024
LN_EPS = 1e-5
DEPTH = 2
ALPHA = (2.0 * DEPTH) ** 0.25

LANE = 128
SUBLANE = 8
_SEG_SIZES = (('a_q', D_A), ('a_kc', G_A * DK_A), ('a_vc', G_A * DK_A), ('a_ks', G_A * DK_A), ('a_vs', G_A * DK_A),
              ('a_kw', G_A * DK_A), ('a_vw', G_A * DK_A), ('a_g', 3 * NH_A), ('a_z', D_A),
              ('b_u', D_B), ('b_v', D_B), ('b_z', D_B),
              ('c_q', D_C), ('c_k', D_C), ('c_v', D_C), ('c_if', 2 * NH_C), ('c_o', D_C), ('c_z', D_C),
              ('m_g', 3 * D_MODEL))
SEG = {}
for _name, _size in _SEG_SIZES:
    _lo = sum(s for _, s in _SEG_SIZES[:len(SEG)])
    SEG[_name] = (_lo, _lo + _size)
MAIN_W = 7680
COL_AQ, COL_AZ, COL_BU, COL_BV, COL_BZ, COL_CQ, COL_CK, COL_CO, COL_CZ = 6, 7, 8, 9, 10, 11, 12, 13, 14
K_W = 256
NSA_VT_W = 512
VT_W = NSA_VT_W + 512
DVX = 128 + 16
GI_LANE = 32
GF_LANE = 36
V7X_VMEM_BYTES = 64 * 1024 * 1024
VMEM_LIMIT = V7X_VMEM_BYTES * 13 // 16
PROJ_TM, PROJ_TN = 2048, 1536
TAIL_TM = 2048
MERGE_TM = 512
REGROUP_COLS = 256
NEG = -1e30
LOG2E = 1.4426950408889634


def _params(*sem):
    return pltpu.CompilerParams(dimension_semantics=sem, vmem_limit_bytes=VMEM_LIMIT)


def _nt_dot(a, b):
    return lax.dot_general(a, b, (((1,), (1,)), ((), ())), preferred_element_type=F32)


def _dot(a, b):
    return jnp.dot(a, b, preferred_element_type=F32)


def _split3(a):
    a1 = a.astype(BF16)
    r1 = a - a1.astype(F32)
    a2 = r1.astype(BF16)
    a3 = (r1 - a2.astype(F32)).astype(BF16)
    return a1, a2, a3


def _sigmoid(x):
    return 0.5 * jnp.tanh(0.5 * x) + 0.5


def _silu(x):
    return x * _sigmoid(x)


def _log_sigmoid(x):
    return jnp.minimum(x, 0.0) - jnp.log1p(jnp.exp(-jnp.abs(x)))


def _layer_spec(a, layer, grid_rank):
    zeros = (0,) * (a.ndim - 1)
    if grid_rank == 1:
        return pl.BlockSpec((None,) + a.shape[1:], lambda i: (layer,) + zeros)
    return pl.BlockSpec((None,) + a.shape[1:], lambda i, j: (layer,) + zeros)


def _mm_kernel(x_ref, wt_ref, o_ref):
    o_ref[...] = _nt_dot(x_ref[...].astype(BF16), wt_ref[...]).astype(o_ref.dtype)


def _matmul(x, wt, layer, tm, tn, name):
    m, k = x.shape
    n = wt.shape[1]
    return pl.pallas_call(
        _mm_kernel,
        grid=(m // tm, n // tn),
        in_specs=[pl.BlockSpec((tm, k), lambda i, j: (i, 0)),
                  pl.BlockSpec((None, tn, k), lambda i, j: (layer, j, 0))],
        out_specs=pl.BlockSpec((tm, tn), lambda i, j: (i, j)),
        out_shape=jax.ShapeDtypeStruct((m, n), BF16),
        compiler_params=_params("parallel", "arbitrary"),
        name=name,
    )(x, wt)


def _tail_kernel(x_ref, wkt_ref, wvt_ref, wauxt_ref, ok_ref, ovt_ref, okc_ref, ovc_ref, osm_ref):
    x = x_ref[...].astype(BF16)
    ok_ref[...] = _nt_dot(x, wkt_ref[...]).astype(BF16)
    ovt_ref[...] = _nt_dot(wvt_ref[...], x).astype(BF16)
    aux = _nt_dot(x, wauxt_ref[...])
    okc_ref[...] = aux[:, 0:LANE]
    ovc_ref[...] = aux[:, LANE:2 * LANE]
    osm_ref[...] = aux[:, 2 * LANE:3 * LANE]


def _tail_matmul(x, wk, wvt, waux, layer, tm):
    m, k = x.shape
    tok = lambda w: pl.BlockSpec((tm, w), lambda i: (i, 0))
    f32_out = jax.ShapeDtypeStruct((m, LANE), F32)
    return pl.pallas_call(
        _tail_kernel,
        grid=(m // tm,),
        in_specs=[tok(k), _layer_spec(wk, layer, 1), _layer_spec(wvt, layer, 1), _layer_spec(waux, layer, 1)],
        out_specs=[tok(K_W), pl.BlockSpec((VT_W, tm), lambda i: (0, i)), tok(LANE), tok(LANE), tok(LANE)],
        out_shape=[jax.ShapeDtypeStruct((m, K_W), BF16), jax.ShapeDtypeStruct((VT_W, m), BF16),
                   f32_out, f32_out, f32_out],
        compiler_params=_params("parallel"),
        name="proj_tail",
    )(x, wk, wvt, waux)


def _compress_kernel(zk_ref, zv_ref, pk_ref, pv_ref, wk1_ref, wk2_ref, wv1_ref, wv2t_ref, ko_ref, vot_ref):
    n = zk_ref.shape[0] // STRIDE_CMP

    def hidden(z_ref, p_ref, w1_ref):
        r0 = [jnp.zeros((n, CMP_HID), F32) for _ in range(G_A)]
        r1 = [jnp.zeros((n, CMP_HID), F32) for _ in range(G_A)]
        for l in range(STRIDE_CMP):
            zl = z_ref[pl.ds(l, n, stride=STRIDE_CMP), :]
            w_a = w1_ref[l * DK_A:(l + 1) * DK_A, :].astype(BF16)
            w_b = w1_ref[(STRIDE_CMP + l) * DK_A:(STRIDE_CMP + l + 1) * DK_A, :].astype(BF16)
            for g in range(G_A):
                zg = zl[:, g * DK_A:(g + 1) * DK_A]
                r0[g] = r0[g] + _dot((zg + p_ref[l:l + 1, :]).astype(BF16), w_a)
                r1[g] = r1[g] + _dot((zg + p_ref[STRIDE_CMP + l:STRIDE_CMP + l + 1, :]).astype(BF16), w_b)
        return [jax.nn.gelu(r0[g] + pltpu.roll(r1[g], n - 1, 0)).astype(BF16) for g in range(G_A)]

    hk = hidden(zk_ref, pk_ref, wk1_ref)
    wk2 = wk2_ref[...].astype(BF16)
    for g in range(G_A):
        ko_ref[0, g] = _dot(hk[g], wk2).astype(BF16)
    hv = hidden(zv_ref, pv_ref, wv1_ref)
    wv2t = wv2t_ref[...].astype(BF16)
    vot_ref[0] = jnp.concatenate([_nt_dot(wv2t, hv[g]) for g in range(G_A)], axis=0).astype(BF16)


def _compress(kc, vc, pk, pv, wk1, wk2, wv1, wv2t, layer, bsz, t_len):
    n = t_len // STRIDE_CMP
    zspec = pl.BlockSpec((t_len, LANE), lambda i: (i, 0))
    full = lambda a: _layer_spec(a, layer, 1)
    return pl.pallas_call(
        _compress_kernel,
        grid=(bsz,),
        in_specs=[zspec, zspec, full(pk), full(pv), full(wk1), full(wk2), full(wv1), full(wv2t)],
        out_specs=[pl.BlockSpec((1, G_A, n, DK_A), lambda i: (i, 0, 0, 0)),
                   pl.BlockSpec((1, G_A * DK_A, n), lambda i: (i, 0, 0))],
        out_shape=[jax.ShapeDtypeStruct((bsz, G_A, n, DK_A), BF16), jax.ShapeDtypeStruct((bsz, G_A * DK_A, n), BF16)],
        compiler_params=_params("parallel"),
        name="nsa_compress",
    )(kc, vc, pk, pv, wk1, wk2, wv1, wv2t)


def _nsa_kernel(q_ref, z_ref, g_ref, kc_ref, vct_ref, k_ref, vt_ref, o_ref,
                s_sc, acc_sc, ow_sc, bw_sc, sel_sc, idx_sc):
    qb = pl.program_id(1)
    t0 = qb * QB
    cols = HPG_A * QB
    n_qtile = QB // LANE
    n_wchunk = (WINDOW + QB) // LANE
    n_wslot = n_wchunk + 1
    k_i = lax.broadcasted_iota(jnp.int32, (LANE, 1), 0)
    t_i = lax.broadcasted_iota(jnp.int32, (1, QB), 1)
    tk = (t_i - k_i).astype(F32)
    ones_row = jnp.where(lax.broadcasted_iota(jnp.int32, (LANE, CK), 0) == DK_A, 1.0, 0.0).astype(BF16)
    slopes = [[LOG2E * 2.0 ** -(g * HPG_A + h + 1) for h in range(HPG_A)] for g in range(G_A)]
    hs = [slice(h * QB, (h + 1) * QB) for h in range(HPG_A)]

    def biased(s, bias_of_head):
        return jnp.concatenate([s[:, hs[h]] + bias_of_head(h) for h in range(HPG_A)], axis=1)

    @pl.when(qb == 0)
    def _():
        for g in range(G_A):
            for h in range(HPG_A):
                for j in range(n_wchunk):
                    dist = tk + float(WINDOW - j * LANE)
                    ok = (dist >= 0.0) & (dist < WINDOW)
                    bw_sc[g, h * n_wslot + j] = jnp.where(ok, -slopes[g][h] * dist, NEG)
                bw_sc[g, h * n_wslot + n_wchunk] = jnp.full((LANE, QB), NEG, F32)

    gsig = _sigmoid(g_ref[...])
    gates_t = jnp.concatenate([jnp.transpose(gsig[r * LANE:(r + 1) * LANE]) for r in range(n_qtile)], axis=1)

    jb = lax.broadcasted_iota(jnp.int32, (N_BLK, QB), 0)
    cur = (t0 + lax.broadcasted_iota(jnp.int32, (N_BLK, QB), 1)) >> 6
    forced = (jb == 0) | (jb == cur) | (jb == cur - 1)
    allowed = jb <= cur
    ov_j = lax.broadcasted_iota(jnp.int32, (N_BLK, N_CMP_PAD), 0) * L_SEL
    ov_n = lax.broadcasted_iota(jnp.int32, (N_BLK, N_CMP_PAD), 1) * STRIDE_CMP
    ov_t = jnp.where((ov_n < ov_j + L_SEL) & (ov_n + L_CMP > ov_j), 1.0, 0.0).astype(BF16)
    jb8 = lax.broadcasted_iota(jnp.int32, (SUBLANE, QB), 0)

    d_cmp = (t0 - (L_CMP - 1)).astype(F32) + (t_i - STRIDE_CMP * k_i).astype(F32)
    valid_cmp = (d_cmp >= 0.0) & (k_i < N_CMP_PAD - 1)

    gls = [slice(g * DK_A, (g + 1) * DK_A) for g in range(G_A)]
    q4s, o_cmps = [], []
    for g in range(G_A):
        q4 = jnp.concatenate(
            [q_ref[:, (g * HPG_A + h) * DK_A:(g * HPG_A + h + 1) * DK_A] for h in range(HPG_A)], axis=0)
        q4 = (q4.astype(F32) * (LOG2E * DK_A ** -0.5)).astype(BF16)
        q4s.append(q4)

        s = biased(_nt_dot(kc_ref[0, g], q4),
                   lambda h: jnp.where(valid_cmp, -slopes[g][h] * d_cmp, -jnp.inf))
        mx = jnp.max(s, axis=0, keepdims=True)
        mx = jnp.where(mx > -jnp.inf, mx, 0.0)
        e = jnp.exp2(s - mx)
        p = e * (1.0 / jnp.maximum(jnp.sum(e, axis=0, keepdims=True), 1e-30))
        o_cmps.append(_dot(vct_ref[0][gls[g], :], p.astype(BF16)))

        psum = p[:, hs[0]] + p[:, hs[1]] + p[:, hs[2]] + p[:, hs[3]]
        p1, p2, p3 = _split3(psum)
        imp = _dot(ov_t, p1) + _dot(ov_t, p2) + _dot(ov_t, p3)
        score = jnp.where(allowed, jnp.where(forced, BIG, imp), -BIG)
        tiles = [score[r * SUBLANE:(r + 1) * SUBLANE] for r in range(N_BLK // SUBLANE)]
        cnt = [jnp.zeros((SUBLANE, QB), F32) for _ in tiles]
        for j in range(N_BLK):
            sj = score[j:j + 1, :]
            for r, tile in enumerate(tiles):
                lo = r * SUBLANE
                if j >= lo + SUBLANE:
                    beats = sj > tile
                elif j < lo:
                    beats = sj >= tile
                else:
                    beats = (sj > tile) | ((sj >= tile) & (jb8 > j - lo))
                cnt[r] = cnt[r] + jnp.where(beats, 1.0, 0.0)
        sel_sc[g] = jnp.where((jnp.concatenate(cnt, axis=0) < N_SEL) & allowed, 1.0, 0.0)

        for r in range(n_qtile):
            q4_r = jnp.concatenate([q4[h * QB + r * LANE:h * QB + (r + 1) * LANE] for h in range(HPG_A)], axis=0)
            s_w, v_w = [], []
            for j in range(r, r + WINDOW // LANE + 1):
                c = qb * n_qtile - WINDOW // LANE + j
                k0 = pl.multiple_of(jnp.maximum(c, 0) * LANE, LANE)
                slot = jnp.where(c >= 0, j, n_wchunk)
                sj = _nt_dot(k_ref[pl.ds(k0, LANE), LANE + g * DK_A:LANE + (g + 1) * DK_A], q4_r)
                s_w.append(jnp.concatenate(
                    [sj[:, h * LANE:(h + 1) * LANE] + bw_sc[g, h * n_wslot + slot, :, r * LANE:(r + 1) * LANE]
                     for h in range(HPG_A)], axis=1))
                v_w.append(vt_ref[G_A * LANE + g * LANE:G_A * LANE + (g + 1) * LANE, pl.ds(k0, LANE)]
                           + ones_row[:, :LANE])
            mx = s_w[0]
            for sj in s_w[1:]:
                mx = jnp.maximum(mx, sj)
            mx = jnp.max(mx, axis=0, keepdims=True)
            o_win = _dot(v_w[0], jnp.exp2((s_w[0] - mx).astype(BF16)))
            for sj, vj in zip(s_w[1:], v_w[1:]):
                o_win = o_win + _dot(vj, jnp.exp2((sj - mx).astype(BF16)))
            ow_sc[g, r] = o_win

    chosen_any = jnp.maximum(sel_sc[0], sel_sc[1])
    blk_per_chunk = CK // L_SEL
    n_causal = ((qb + 1) * QB - 1) // CK + 1
    n_chunk = jnp.int32(0)
    for c in range(s_sc.shape[1]):
        need = (jnp.max(chosen_any[c * blk_per_chunk:(c + 1) * blk_per_chunk, :]) > 0.5) & (c < n_causal)

        @pl.when(need)
        def _():
            idx_sc[n_chunk] = c

        n_chunk = n_chunk + jnp.where(need, 1, 0)

    def score_chunk(i, m_runs):
        c = idx_sc[i]
        k0 = pl.multiple_of(c * CK, CK)
        out = []
        for g in range(G_A):
            sc = _nt_dot(k_ref[pl.ds(k0, CK), gls[g]], q4s[g])
            halves = []
            for u in range(CK // LANE):
                dist = tk + (t0 - k0 - u * LANE).astype(F32)
                blk0 = (CK // L_SEL) * c + (LANE // L_SEL) * u
                chosen = jnp.concatenate(
                    [jnp.broadcast_to(sel_sc[g, pl.ds(blk0 + b, 1), :], (L_SEL, QB))
                     for b in range(LANE // L_SEL)], axis=0)
                ok = (chosen > 0.5) & (dist >= 0.0)
                far = jnp.where(ok, dist, -NEG)
                halves.append(biased(sc[u * LANE:(u + 1) * LANE], lambda h: far * -slopes[g][h]))
            sc = jnp.concatenate(halves, axis=0)
            s_sc[g, i] = sc
            out.append(jnp.maximum(m_runs[g], jnp.max(sc.reshape(CK // SUBLANE, SUBLANE, cols), axis=0)))
        return tuple(out)

    m_runs = lax.fori_loop(0, n_chunk, score_chunk, (jnp.full((SUBLANE, cols), NEG, F32),) * G_A)
    m_rows = [jnp.max(m_runs[g], axis=0, keepdims=True) for g in range(G_A)]
    acc_sc[...] = jnp.zeros(acc_sc.shape, F32)

    def value_chunk(i, carry):
        k0 = pl.multiple_of(idx_sc[i] * CK, CK)
        for g in range(G_A):
            pc = jnp.exp2((s_sc[g, i] - m_rows[g]).astype(BF16))
            acc_sc[g] += _dot(vt_ref[g * LANE:(g + 1) * LANE, pl.ds(k0, CK)] + ones_row, pc)
        return carry

    lax.fori_loop(0, n_chunk, value_chunk, 0)

    pieces = []
    for g in range(G_A):
        o_sel = acc_sc[g]
        for h in range(HPG_A):
            col = g * HPG_A + h
            os_h = o_sel[:, hs[h]]
            ow_h = jnp.concatenate([ow_sc[g, r, :, h * LANE:(h + 1) * LANE] for r in range(n_qtile)], axis=1)
            w_sel = gates_t[NH_A + col:NH_A + col + 1, :] / jnp.maximum(os_h[DK_A:DK_A + 1, :], 1e-30)
            w_win = gates_t[2 * NH_A + col:2 * NH_A + col + 1, :] / jnp.maximum(ow_h[DK_A:DK_A + 1, :], 1e-30)
            pieces.append(gates_t[col:col + 1, :] * o_cmps[g][:, hs[h]] + w_sel * os_h[:DK_A] + w_win * ow_h[:DK_A])

    out_t = jnp.concatenate(pieces, axis=0)
    out = jnp.concatenate(
        [jnp.concatenate([jnp.transpose(out_t[i * LANE:(i + 1) * LANE, r * LANE:(r + 1) * LANE])
                          for i in range(D_A // LANE)], axis=1) for r in range(n_qtile)], axis=0)
    o_ref[...] = (out * _silu(z_ref[...].astype(F32))).astype(BF16)


def _nsa(main, kk, vt, small, kcmp, vcmp_t, bsz, t_len):
    nqb = t_len // QB
    cols = HPG_A * QB
    n_wslot = (WINDOW + QB) // LANE + 1
    return pl.pallas_call(
        _nsa_kernel,
        grid=(bsz, nqb),
        in_specs=[pl.BlockSpec((QB, D_A), lambda b, i: (b * nqb + i, COL_AQ)),
                  pl.BlockSpec((QB, D_A), lambda b, i: (b * nqb + i, COL_AZ)),
                  pl.BlockSpec((QB, LANE), lambda b, i: (b * nqb + i, 0)),
                  pl.BlockSpec((1, G_A, N_CMP_PAD, DK_A), lambda b, i: (b, 0, 0, 0)),
                  pl.BlockSpec((1, G_A * DK_A, N_CMP_PAD), lambda b, i: (b, 0, 0)),
                  pl.BlockSpec((t_len, K_W), lambda b, i: (b, 0)),
                  pl.BlockSpec((NSA_VT_W, t_len), lambda b, i: (0, b))],
        out_specs=pl.BlockSpec((QB, D_A), lambda b, i: (b * nqb + i, 0)),
        out_shape=jax.ShapeDtypeStruct((bsz * t_len, D_A), BF16),
        scratch_shapes=[pltpu.VMEM((G_A, t_len // CK, CK, cols), F32), pltpu.VMEM((G_A, LANE, cols), F32),
                        pltpu.VMEM((G_A, QB // LANE, LANE, HPG_A * LANE), F32),
                        pltpu.VMEM((G_A, HPG_A * n_wslot, LANE, QB), F32), pltpu.VMEM((G_A, N_BLK, QB), F32),
                        pltpu.SMEM((t_len // CK,), jnp.int32)],
        compiler_params=_params("parallel", "arbitrary"),
        name="nsa_attention",
    )(main, main, small, kcmp, vcmp_t, kk, vt)


def _sgu_tile(u_ref, v_ref, z_ref, lng_ref, lnb_ref, w_ref, b_ref):
    u = jax.nn.gelu(u_ref[...].astype(F32))
    v = jax.nn.gelu(v_ref[...].astype(F32))
    mu = jnp.mean(v, axis=-1, keepdims=True)
    var = jnp.mean(jnp.square(v - mu), axis=-1, keepdims=True)
    vn = ((v - mu) * lax.rsqrt(var + LN_EPS) * lng_ref[...] + lnb_ref[...]).astype(BF16)
    gate = u * _silu(z_ref[...].astype(F32))
    ti = lax.broadcasted_iota(jnp.int32, (CHUNK_B, CHUNK_B), 0)
    si = lax.broadcasted_iota(jnp.int32, (CHUNK_B, CHUNK_B), 1)
    ws = [jnp.where(si <= ti, w_ref[g], 0.0).astype(BF16) for g in range(G_B)]
    rows = []
    for c in range(u_ref.shape[0] // CHUNK_B):
        rs = slice(c * CHUNK_B, (c + 1) * CHUNK_B)
        rows.append(jnp.concatenate(
            [gate[rs, g * LANE:(g + 1) * LANE] * (_dot(ws[g], vn[rs, g * LANE:(g + 1) * LANE]) + b_ref[:, g:g + 1])
             for g in range(G_B)], axis=1))
    return jnp.concatenate(rows, axis=0)


def _mlstm_kernel(q_ref, k_ref, vt_ref, o_ref, z_ref, g_ref, cw_ref, cb_ref, gb_ref, ng_ref, y_ref,
                  qc_sc, kc_sc, halo_sc, st_sc, m_sc):
    sb = pl.program_id(1)
    blk = q_ref.shape[0]

    @pl.when(sb == 0)
    def _():
        halo_sc[...] = jnp.zeros(halo_sc.shape, F32)
        st_sc[...] = jnp.zeros(st_sc.shape, F32)
        m_sc[...] = jnp.zeros(m_sc.shape, F32)

    r8 = lax.broadcasted_iota(jnp.int32, (SUBLANE, 1), 0)

    def conv_silu(x_ref, which, dst_ref, scale):
        x = x_ref[...].astype(F32)
        prev = halo_sc[which]
        w = cw_ref[:, which * D_C:(which + 1) * D_C]
        acc = x * w[CONV_W - 1:CONV_W, :] + cb_ref[:, which * D_C:(which + 1) * D_C]
        for j in range(1, CONV_W):
            rolled = pltpu.roll(x, j, 0)
            head = jnp.where(r8 < j, pltpu.roll(prev, j, 0), rolled[0:SUBLANE])
            shifted = jnp.concatenate([head, rolled[SUBLANE:]], axis=0)
            acc = acc + shifted * w[CONV_W - 1 - j:CONV_W - j, :]
        halo_sc[which] = x[blk - SUBLANE:blk]
        dst_ref[...] = (_silu(acc) * scale).astype(BF16)

    conv_silu(q_ref, 0, qc_sc, 1.0)
    conv_silu(k_ref, 1, kc_sc, DH_C ** -0.5)

    si = lax.broadcasted_iota(jnp.int32, (CHUNK_C, CHUNK_C), 0)
    ti = lax.broadcasted_iota(jnp.int32, (CHUNK_C, CHUNK_C), 1)
    causal = si <= ti
    tri_u = jnp.where(causal, 1.0, 0.0).astype(BF16)
    ones_rows = jnp.where(lax.broadcasted_iota(jnp.int32, (DVX - DH_C, CHUNK_C), 0) == 0, 1.0, 0.0).astype(BF16)
    ng = ng_ref[...]
    gbias = gb_ref[...]

    for c in range(blk // CHUNK_C):
        rs = slice(c * CHUNK_C, (c + 1) * CHUNK_C)
        gate_rows = jnp.transpose(g_ref[rs, :] + gbias)[GI_LANE:GI_LANE + 2 * NH_C]
        f1, f2, f3 = _split3(_log_sigmoid(gate_rows))
        bc_rows = _dot(f1, tri_u) + _dot(f2, tri_u) + _dot(f3, tri_u)
        g_rows = gate_rows[0:NH_C] - bc_rows[NH_C:2 * NH_C]
        g_cols = jnp.transpose(jnp.concatenate([g_rows, jnp.zeros((CHUNK_C - NH_C, CHUNK_C), F32)], axis=0))
        for h in range(NH_C):
            ls = slice(h * DH_C, (h + 1) * DH_C)
            q = qc_sc[rs, ls]
            k = kc_sc[rs, ls]
            vext = jnp.concatenate([vt_ref[ls, rs], ones_rows], axis=0)
            g_col = g_cols[:, h:h + 1]
            g_row = g_rows[h:h + 1, :]
            bc_row = bc_rows[NH_C + h:NH_C + h + 1, :]
            b_last = bc_row[:, CHUNK_C - 1:CHUNK_C]
            m_prev = m_sc[h][:, 0:1]
            st = st_sc[h]
            m_loc = b_last + jnp.max(g_row, axis=1, keepdims=True)
            e_end = jnp.exp(b_last + g_row - m_loc)

            gmat = jnp.where(causal, g_col, -jnp.inf)
            mg = jnp.maximum(m_prev, jnp.max(gmat, axis=0, keepdims=True))
            s_t = _nt_dot(k, q) * jnp.exp(gmat - mg)
            e_int = jnp.exp(m_prev - mg)
            both = _dot(vext, s_t.astype(BF16)) + _nt_dot(st.astype(BF16), q) * e_int
            den = both[DH_C:DH_C + 1, :]
            hval = both[:DH_C] * (1.0 / jnp.maximum(jnp.abs(den), jnp.exp(-(bc_row + mg))))
            mu = jnp.mean(hval, axis=0, keepdims=True)
            var = jnp.mean(jnp.square(hval - mu), axis=0, keepdims=True)
            hn = jnp.transpose((hval - mu) * lax.rsqrt(var + LN_EPS))
            o_gate = o_ref[rs, ls].astype(F32)
            z_gate = z_ref[rs, ls].astype(F32)
            gated = hn * ng[:, ls] * _sigmoid(o_gate) * _silu(z_gate)
            y_ref[rs, ls] = gated.astype(BF16)

            loc = _dot((vext.astype(F32) * e_end).astype(BF16), k)
            m_new = jnp.maximum(b_last + m_prev, m_loc)
            st_sc[h] = jnp.exp(b_last + m_prev - m_new) * st + jnp.exp(m_loc - m_new) * loc
            m_sc[h] = jnp.broadcast_to(m_new, (1, LANE))


def _mlstm(main, vt, small, conv_w, conv_b, gbias, norm_g, layer, bsz, t_len):
    nsb = t_len // SEQ_BLK_C
    blk = lambda col: pl.BlockSpec((SEQ_BLK_C, D_C), lambda b, s: (b * nsb + s, col))
    full = lambda a: _layer_spec(a, layer, 2)
    return pl.pallas_call(
        _mlstm_kernel,
        grid=(bsz, nsb),
        in_specs=[blk(COL_CQ), blk(COL_CK),
                  pl.BlockSpec((D_C, SEQ_BLK_C), lambda b, s: (NSA_VT_W // D_C, b * nsb + s)),
                  blk(COL_CO), blk(COL_CZ),
                  pl.BlockSpec((SEQ_BLK_C, LANE), lambda b, s: (b * nsb + s, 0)),
                  full(conv_w), full(conv_b), full(gbias), full(norm_g)],
        out_specs=pl.BlockSpec((SEQ_BLK_C, D_C), lambda b, s: (b * nsb + s, 0)),
        out_shape=jax.ShapeDtypeStruct((bsz * t_len, D_C), BF16),
        scratch_shapes=[pltpu.VMEM((SEQ_BLK_C, D_C), BF16), pltpu.VMEM((SEQ_BLK_C, D_C), BF16),
                        pltpu.VMEM((2, SUBLANE, D_C), F32), pltpu.VMEM((NH_C, DVX, DH_C), F32),
                        pltpu.VMEM((NH_C, 1, LANE), F32)],
        compiler_params=_params("parallel", "arbitrary"),
        name="mlstm",
    )(main, main, vt, main, main, small, conv_w, conv_b, gbias, norm_g)


def _merge_kernel(x_ref, ya_ref, yc_ref, u_ref, v_ref, z_ref, g0_ref, g1_ref, g2_ref, p_ref,
                  sgg_ref, sgb_ref, sgw_ref, sgbt_ref,
                  wa_ref, wb_ref, wc_ref, wo_ref, wp_ref, wg_ref, lng_ref, lnb_ref, o_ref, *ob_ref):
    yb = _sgu_tile(u_ref, v_ref, z_ref, sgg_ref, sgb_ref, sgw_ref, sgbt_ref).astype(BF16)
    merged = (_sigmoid(g0_ref[...].astype(F32)) * _dot(ya_ref[...], wa_ref[...])
              + _sigmoid(g1_ref[...].astype(F32)) * _dot(yb, wb_ref[...])
              + _sigmoid(g2_ref[...].astype(F32)) * _dot(yc_ref[...], wc_ref[...]))
    r = ALPHA * x_ref[...] + _dot(merged.astype(BF16), wo_ref[...])
    r = r + _sigmoid(_dot(r.astype(BF16), wg_ref[...])) * _dot(p_ref[...].astype(BF16), wp_ref[...])
    mu = jnp.mean(r, axis=-1, keepdims=True)
    var = jnp.mean(jnp.square(r - mu), axis=-1, keepdims=True)
    y = (r - mu) * lax.rsqrt(var + LN_EPS) * lng_ref[...] + lnb_ref[...]
    o_ref[...] = y
    for ref in ob_ref:
        ref[...] = y.astype(BF16)


def _merge(x, ya, yc, main, p, sgg, sgb, sgw, sgbt, wa, wb, wc, wo, wp, wg, lng, lnb, layer, tm):
    n = x.shape[0]
    n_out = 1 if layer == DEPTH - 1 else 2
    row = lambda w: pl.BlockSpec((tm, w), lambda i: (i, 0))
    blk = lambda col: pl.BlockSpec((tm, D_B), lambda i: (i, col))
    gate = lambda j: pl.BlockSpec((tm, D_MODEL), lambda i: (i, j))
    full = lambda a: _layer_spec(a, layer, 1)
    return pl.pallas_call(
        _merge_kernel,
        grid=(n // tm,),
        in_specs=[row(D_MODEL), row(D_A), row(D_C), blk(COL_BU), blk(COL_BV), blk(COL_BZ),
                  gate(0), gate(1), gate(2), pl.BlockSpec((None, tm, PLE_DIM), lambda i: (layer, i, 0)),
                  full(sgg), full(sgb), full(sgw), full(sgbt),
                  full(wa), full(wb), full(wc), full(wo), full(wp), full(wg), full(lng), full(lnb)],
        out_specs=[row(D_MODEL)] * n_out,
        out_shape=[jax.ShapeDtypeStruct((n, D_MODEL), F32), jax.ShapeDtypeStruct((n, D_MODEL), BF16)][:n_out],
        compiler_params=_params("parallel"),
        name="merge",
    )(x, ya, yc, main, main, main, main, main, main, p, sgg, sgb, sgw, sgbt, wa, wb, wc, wo, wp, wg, lng, lnb)


def _regroup_kernel(wt_ref, main_ref, k_ref, v_ref, aux_ref):
    wt = wt_ref[...]
    cols = wt.shape[1]
    run = lambda first, last: wt[SEG[first][0]:SEG[last][1]]
    zeros = lambda n: jnp.zeros((n, cols), F32)
    main_ref[...] = jnp.concatenate([run('m_g', 'm_g'), run('a_q', 'a_q'), run('a_z', 'c_k'), run('c_o', 'c_z')],
                                    axis=0).astype(BF16)
    k_ref[...] = jnp.concatenate([run('a_ks', 'a_ks'), run('a_kw', 'a_kw')], axis=0).astype(BF16)
    v_rows = []
    for name in ('a_vs', 'a_vw'):
        for g in range(G_A):
            lo = SEG[name][0] + g * DK_A
            v_rows += [wt[lo:lo + DK_A], zeros(LANE - DK_A)]
    v_ref[...] = jnp.concatenate(v_rows + [run('c_v', 'c_v')], axis=0).astype(BF16)
    n_ag = SEG['a_g'][1] - SEG['a_g'][0]
    aux_ref[...] = jnp.concatenate([run('a_kc', 'a_vc'), run('a_g', 'a_g'), zeros(GI_LANE - n_ag), run('c_if', 'c_if'),
                                    zeros(LANE - GI_LANE - 2 * NH_C)], axis=0).astype(BF16)


def _regroup(w_in_t, cols):
    nl, n_in, d = w_in_t.shape
    blk = lambda rows: pl.BlockSpec((None, rows, cols), lambda l, c: (l, 0, c))
    return pl.pallas_call(
        _regroup_kernel,
        grid=(nl, d // cols),
        in_specs=[blk(n_in)],
        out_specs=[blk(MAIN_W), blk(K_W), blk(VT_W), blk(3 * LANE)],
        out_shape=[jax.ShapeDtypeStruct((nl, MAIN_W, d), BF16), jax.ShapeDtypeStruct((nl, K_W, d), BF16),
                   jax.ShapeDtypeStruct((nl, VT_W, d), BF16), jax.ShapeDtypeStruct((nl, 3 * LANE, d), BF16)],
        compiler_params=_params("parallel", "parallel"),
        name="regroup_w_in",
    )(w_in_t)


def _prepare(w_in, cmp_wv2, sg_ln_g, sg_ln_b, sg_b,
             ml_conv_b, ml_b_i, ml_b_f, ml_norm_g, w_br_a, w_br_b, w_br_c, w_out, ple_w, ple_gate, ln_g, ln_b):
    nl = w_in.shape[0]
    prm = {}
    prm['w_main'], prm['w_k'], prm['w_vt'], prm['w_aux'] = _regroup(jnp.swapaxes(w_in, 1, 2), REGROUP_COLS)
    prm['wv2t'] = jnp.swapaxes(cmp_wv2, 1, 2)
    prm['sg_ln_g'], prm['sg_ln_b'] = sg_ln_g[:, None, :], sg_ln_b[:, None, :]
    prm['sg_b_t'] = jnp.swapaxes(sg_b, 1, 2)
    gbias = jnp.zeros((nl, 1, LANE), F32).at[:, 0, GI_LANE:GI_LANE + NH_C].set(ml_b_i)
    prm['gbias'] = gbias.at[:, 0, GF_LANE:GF_LANE + NH_C].set(ml_b_f)
    prm['conv_b'], prm['norm_g'] = ml_conv_b[:, None, :], ml_norm_g[:, None, :]
    for name, a in (('wa', w_br_a), ('wb', w_br_b), ('wc', w_br_c), ('wo', w_out), ('wp', ple_w), ('wg', ple_gate)):
        prm[name] = a.astype(BF16)
    prm['ln_g'], prm['ln_b'] = ln_g[:, None, :], ln_b[:, None, :]
    return prm


def _layer(i, x, xin, p, prm, cmp, sg_w, ml_conv_w, bsz, t_len):
    main = _matmul(xin, prm['w_main'], i, PROJ_TM, PROJ_TN, "proj_main")
    kk, vt, kc, vc, small = _tail_matmul(xin, prm['w_k'], prm['w_vt'], prm['w_aux'], i, TAIL_TM)

    pos_k, pos_v, wk1, wk2, wv1 = cmp
    kcmp, vcmp_t = _compress(kc, vc, pos_k, pos_v, wk1, wk2, wv1, prm['wv2t'], i, bsz, t_len)
    ya = _nsa(main, kk, vt, small, kcmp, vcmp_t, bsz, t_len)
    yc = _mlstm(main, vt, small, ml_conv_w, prm['conv_b'], prm['gbias'], prm['norm_g'], i, bsz, t_len)
    return _merge(x, ya, yc, main, p, prm['sg_ln_g'], prm['sg_ln_b'], sg_w, prm['sg_b_t'],
                  prm['wa'], prm['wb'], prm['wc'], prm['wo'], prm['wp'], prm['wg'], prm['ln_g'], prm['ln_b'], i, MERGE_TM)


def kernel(x, p, w_in, cmp_pos_k, cmp_pos_v, cmp_wk1, cmp_wk2, cmp_wv1, cmp_wv2, sg_ln_g, sg_ln_b, sg_w, sg_b,
           ml_conv_w, ml_conv_b, ml_b_i, ml_b_f, ml_norm_g, w_br_a, w_br_b, w_br_c, w_out, ple_w, ple_gate,
           ln_g, ln_b):
    bsz, t_len, d = x.shape
    assert d == D_MODEL and t_len == N_CMP_PAD * STRIDE_CMP and t_len // L_SEL == N_BLK
    assert t_len % SEQ_BLK_C == 0 and t_len % CK == 0 and w_in.shape[0] == DEPTH
    prm = _prepare(w_in, cmp_wv2, sg_ln_g, sg_ln_b, sg_b,
                   ml_conv_b, ml_b_i, ml_b_f, ml_norm_g, w_br_a, w_br_b, w_br_c, w_out, ple_w, ple_gate, ln_g, ln_b)
    xf = x.reshape(bsz * t_len, d)
    xin = xf
    pf = p.reshape(DEPTH, bsz * t_len, PLE_DIM)
    for i in range(DEPTH):
        outs = _layer(i, xf, xin, pf, prm, (cmp_pos_k, cmp_pos_v, cmp_wk1, cmp_wk2, cmp_wv1), sg_w, ml_conv_w,
                      bsz, t_len)
        xf, xin = outs[0], outs[-1]
    return xf.reshape(bsz, t_len, d)
```

```python
import jax
import jax.numpy as jnp
from jax import lax
from jax.experimental import pallas as pl
from jax.experimental.pallas import tpu as pltpu

F32 = jnp.float32
BF16 = jnp.bfloat16

D_MODEL = 1024
PLE_DIM = 256
D_A = 512
NH_A = 8
G_A = 2
HPG_A = 4
DK_A = 64
L_CMP = 32
STRIDE_CMP = 16
CMP_HID = 128
L_SEL = 64
N_SEL = 8
N_BLK = 32
WINDOW = 256
QB = 256
CK = 256
N_CMP_PAD = 128
BIG = 1e9
D_B = 512
CHUNK_B = 128
G_B = 4
D_C = 512
NH_C = 4
DH_C = 128
CONV_W = 4
CHUNK_C = 128
SEQ_BLK_C = 1024
LN_EPS = 1e-5
DEPTH = 2
ALPHA = (2.0 * DEPTH) ** 0.25

LANE = 128
SUBLANE = 8
_SEG_SIZES = (('a_q', D_A), ('a_kc', G_A * DK_A), ('a_vc', G_A * DK_A), ('a_ks', G_A * DK_A), ('a_vs', G_A * DK_A),
              ('a_kw', G_A * DK_A), ('a_vw', G_A * DK_A), ('a_g', 3 * NH_A), ('a_z', D_A),
              ('b_u', D_B), ('b_v', D_B), ('b_z', D_B),
              ('c_q', D_C), ('c_k', D_C), ('c_v', D_C), ('c_if', 2 * NH_C), ('c_o', D_C), ('c_z', D_C),
              ('m_g', 3 * D_MODEL))
SEG = {}
for _name, _size in _SEG_SIZES:
    _lo = sum(s for _, s in _SEG_SIZES[:len(SEG)])
    SEG[_name] = (_lo, _lo + _size)
MAIN_W = 7680
COL_AQ, COL_AZ, COL_BU, COL_BV, COL_BZ, COL_CQ, COL_CK, COL_CO, COL_CZ = 6, 7, 8, 9, 10, 11, 12, 13, 14
K_W = 256
NSA_VT_W = 512
VT_W = NSA_VT_W + 512
DVX = 128 + 16
GI_LANE = 32
GF_LANE = 36
V7X_VMEM_BYTES = 64 * 1024 * 1024
VMEM_LIMIT = V7X_VMEM_BYTES * 13 // 16
PROJ_TM, PROJ_TN = 2048, 1536
TAIL_TM = 2048
MERGE_TM = 512
REGROUP_COLS = 256
NEG = -1e30
LOG2E = 1.4426950408889634


def _params(*sem):
    return pltpu.CompilerParams(dimension_semantics=sem, vmem_limit_bytes=VMEM_LIMIT)


def _nt_dot(a, b):
    return lax.dot_general(a, b, (((1,), (1,)), ((), ())), preferred_element_type=F32)


def _dot(a, b):
    return jnp.dot(a, b, preferred_element_type=F32)


def _split3(a):
    a1 = a.astype(BF16)
    r1 = a - a1.astype(F32)
    a2 = r1.astype(BF16)
    a3 = (r1 - a2.astype(F32)).astype(BF16)
    return a1, a2, a3


def _sigmoid(x):
    return 0.5 * jnp.tanh(0.5 * x) + 0.5


def _silu(x):
    return x * _sigmoid(x)


def _log_sigmoid(x):
    return jnp.minimum(x, 0.0) - jnp.log1p(jnp.exp(-jnp.abs(x)))


def _layer_spec(a, layer, grid_rank):
    zeros = (0,) * (a.ndim - 1)
    if grid_rank == 1:
        return pl.BlockSpec((None,) + a.shape[1:], lambda i: (layer,) + zeros)
    return pl.BlockSpec((None,) + a.shape[1:], lambda i, j: (layer,) + zeros)


def _mm_kernel(x_ref, wt_ref, o_ref):
    o_ref[...] = _nt_dot(x_ref[...].astype(BF16), wt_ref[...]).astype(o_ref.dtype)


def _matmul(x, wt, layer, tm, tn, name):
    m, k = x.shape
    n = wt.shape[1]
    return pl.pallas_call(
        _mm_kernel,
        grid=(m // tm, n // tn),
        in_specs=[pl.BlockSpec((tm, k), lambda i, j: (i, 0)),
                  pl.BlockSpec((None, tn, k), lambda i, j: (layer, j, 0))],
        out_specs=pl.BlockSpec((tm, tn), lambda i, j: (i, j)),
        out_shape=jax.ShapeDtypeStruct((m, n), BF16),
        compiler_params=_params("parallel", "arbitrary"),
        name=name,
    )(x, wt)


def _tail_kernel(x_ref, wkt_ref, wvt_ref, wauxt_ref, ok_ref, ovt_ref, okc_ref, ovc_ref, osm_ref):
    x = x_ref[...].astype(BF16)
    ok_ref[...] = _nt_dot(x, wkt_ref[...]).astype(BF16)
    ovt_ref[...] = _nt_dot(wvt_ref[...], x).astype(BF16)
    aux = _nt_dot(x, wauxt_ref[...])
    okc_ref[...] = aux[:, 0:LANE]
    ovc_ref[...] = aux[:, LANE:2 * LANE]
    osm_ref[...] = aux[:, 2 * LANE:3 * LANE]


def _tail_matmul(x, wk, wvt, waux, layer, tm):
    m, k = x.shape
    tok = lambda w: pl.BlockSpec((tm, w), lambda i: (i, 0))
    f32_out = jax.ShapeDtypeStruct((m, LANE), F32)
    return pl.pallas_call(
        _tail_kernel,
        grid=(m // tm,),
        in_specs=[tok(k), _layer_spec(wk, layer, 1), _layer_spec(wvt, layer, 1), _layer_spec(waux, layer, 1)],
        out_specs=[tok(K_W), pl.BlockSpec((VT_W, tm), lambda i: (0, i)), tok(LANE), tok(LANE), tok(LANE)],
        out_shape=[jax.ShapeDtypeStruct((m, K_W), BF16), jax.ShapeDtypeStruct((VT_W, m), BF16),
                   f32_out, f32_out, f32_out],
        compiler_params=_params("parallel"),
        name="proj_tail",
    )(x, wk, wvt, waux)


def _compress_kernel(zk_ref, zv_ref, pk_ref, pv_ref, wk1_ref, wk2_ref, wv1_ref, wv2t_ref, ko_ref, vot_ref):
    n = zk_ref.shape[0] // STRIDE_CMP

    def hidden(z_ref, p_ref, w1_ref):
        r0 = [jnp.zeros((n, CMP_HID), F32) for _ in range(G_A)]
        r1 = [jnp.zeros((n, CMP_HID), F32) for _ in range(G_A)]
        for l in range(STRIDE_CMP):
            zl = z_ref[pl.ds(l, n, stride=STRIDE_CMP), :]
            w_a = w1_ref[l * DK_A:(l + 1) * DK_A, :].astype(BF16)
            w_b = w1_ref[(STRIDE_CMP + l) * DK_A:(STRIDE_CMP + l + 1) * DK_A, :].astype(BF16)
            for g in range(G_A):
                zg = zl[:, g * DK_A:(g + 1) * DK_A]
                r0[g] = r0[g] + _dot((zg + p_ref[l:l + 1, :]).astype(BF16), w_a)
                r1[g] = r1[g] + _dot((zg + p_ref[STRIDE_CMP + l:STRIDE_CMP + l + 1, :]).astype(BF16), w_b)
        return [jax.nn.gelu(r0[g] + pltpu.roll(r1[g], n - 1, 0)).astype(BF16) for g in range(G_A)]

    hk = hidden(zk_ref, pk_ref, wk1_ref)
    wk2 = wk2_ref[...].astype(BF16)
    for g in range(G_A):
        ko_ref[0, g] = _dot(hk[g], wk2).astype(BF16)
    hv = hidden(zv_ref, pv_ref, wv1_ref)
    wv2t = wv2t_ref[...].astype(BF16)
    vot_ref[0] = jnp.concatenate([_nt_dot(wv2t, hv[g]) for g in range(G_A)], axis=0).astype(BF16)


def _compress(kc, vc, pk, pv, wk1, wk2, wv1, wv2t, layer, bsz, t_len):
    n = t_len // STRIDE_CMP
    zspec = pl.BlockSpec((t_len, LANE), lambda i: (i, 0))
    full = lambda a: _layer_spec(a, layer, 1)
    return pl.pallas_call(
        _compress_kernel,
        grid=(bsz,),
        in_specs=[zspec, zspec, full(pk), full(pv), full(wk1), full(wk2), full(wv1), full(wv2t)],
        out_specs=[pl.BlockSpec((1, G_A, n, DK_A), lambda i: (i, 0, 0, 0)),
                   pl.BlockSpec((1, G_A * DK_A, n), lambda i: (i, 0, 0))],
        out_shape=[jax.ShapeDtypeStruct((bsz, G_A, n, DK_A), BF16), jax.ShapeDtypeStruct((bsz, G_A * DK_A, n), BF16)],
        compiler_params=_params("parallel"),
        name="nsa_compress",
    )(kc, vc, pk, pv, wk1, wk2, wv1, wv2t)


def _nsa_kernel(q_ref, z_ref, g_ref, kc_ref, vct_ref, k_ref, vt_ref, o_ref,
                s_sc, acc_sc, ow_sc, bw_sc, sel_sc, idx_sc):
    qb = pl.program_id(1)
    t0 = qb * QB
    cols = HPG_A * QB
    n_qtile = QB // LANE
    n_wchunk = (WINDOW + QB) // LANE
    n_wslot = n_wchunk + 1
    k_i = lax.broadcasted_iota(jnp.int32, (LANE, 1), 0)
    t_i = lax.broadcasted_iota(jnp.int32, (1, QB), 1)
    tk = (t_i - k_i).astype(F32)
    ones_row = jnp.where(lax.broadcasted_iota(jnp.int32, (LANE, CK), 0) == DK_A, 1.0, 0.0).astype(BF16)
    slopes = [[LOG2E * 2.0 ** -(g * HPG_A + h + 1) for h in range(HPG_A)] for g in range(G_A)]
    hs = [slice(h * QB, (h + 1) * QB) for h in range(HPG_A)]

    def biased(s, bias_of_head):
        return jnp.concatenate([s[:, hs[h]] + bias_of_head(h) for h in range(HPG_A)], axis=1)

    @pl.when((pl.program_id(0) == 0) & (qb == 0))
    def _():
        for g in range(G_A):
            for h in range(HPG_A):
                for j in range(n_wchunk):
                    dist = tk + float(WINDOW - j * LANE)
                    ok = (dist >= 0.0) & (dist < WINDOW)
                    bw_sc[g, h * n_wslot + j] = jnp.where(ok, -slopes[g][h] * dist, NEG)
                bw_sc[g, h * n_wslot + n_wchunk] = jnp.full((LANE, QB), NEG, F32)

    gsig = _sigmoid(g_ref[...])
    gates_t = jnp.concatenate([jnp.transpose(gsig[r * LANE:(r + 1) * LANE]) for r in range(n_qtile)], axis=1)

    jb = lax.broadcasted_iota(jnp.int32, (N_BLK, QB), 0)
    cur = (t0 + lax.broadcasted_iota(jnp.int32, (N_BLK, QB), 1)) >> 6
    forced = (jb == 0) | (jb == cur) | (jb == cur - 1)
    allowed = jb <= cur
    ov_j = lax.broadcasted_iota(jnp.int32, (N_BLK, N_CMP_PAD), 0) * L_SEL
    ov_n = lax.broadcasted_iota(jnp.int32, (N_BLK, N_CMP_PAD), 1) * STRIDE_CMP
    ov_t = jnp.where((ov_n < ov_j + L_SEL) & (ov_n + L_CMP > ov_j), 1.0, 0.0).astype(BF16)
    jb8 = lax.broadcasted_iota(jnp.int32, (SUBLANE, QB), 0)

    d_cmp = (t0 - (L_CMP - 1)).astype(F32) + (t_i - STRIDE_CMP * k_i).astype(F32)
    valid_cmp = (d_cmp >= 0.0) & (k_i < N_CMP_PAD - 1)

    gls = [slice(g * DK_A, (g + 1) * DK_A) for g in range(G_A)]
    q4s, o_cmps = [], []
    for g in range(G_A):
        q4 = jnp.concatenate(
            [q_ref[:, (g * HPG_A + h) * DK_A:(g * HPG_A + h + 1) * DK_A] for h in range(HPG_A)], axis=0)
        q4 = (q4.astype(F32) * (LOG2E * DK_A ** -0.5)).astype(BF16)
        q4s.append(q4)

        s = biased(_nt_dot(kc_ref[0, g], q4),
                   lambda h: jnp.where(valid_cmp, -slopes[g][h] * d_cmp, -jnp.inf))
        mx = jnp.max(s, axis=0, keepdims=True)
        mx = jnp.where(mx > -jnp.inf, mx, 0.0)
        e = jnp.exp2(s - mx)
        p = e * (1.0 / jnp.maximum(jnp.sum(e, axis=0, keepdims=True), 1e-30))
        o_cmps.append(_dot(vct_ref[0][gls[g], :], p.astype(BF16)))

        psum = p[:, hs[0]] + p[:, hs[1]] + p[:, hs[2]] + p[:, hs[3]]
        p1, p2, p3 = _split3(psum)
        imp = _dot(ov_t, p1) + _dot(ov_t, p2) + _dot(ov_t, p3)
        score = jnp.where(allowed, jnp.where(forced, BIG, imp), -BIG)
        tiles = [score[r * SUBLANE:(r + 1) * SUBLANE] for r in range(N_BLK // SUBLANE)]
        cnt = [jnp.zeros((SUBLANE, QB), F32) for _ in tiles]
        for j in range(N_BLK):
            sj = score[j:j + 1, :]
            for r, tile in enumerate(tiles):
                lo = r * SUBLANE
                if j >= lo + SUBLANE:
                    beats = sj > tile
                elif j < lo:
                    beats = sj >= tile
                else:
                    beats = (sj > tile) | ((sj >= tile) & (jb8 > j - lo))
                cnt[r] = cnt[r] + jnp.where(beats, 1.0, 0.0)
        sel_sc[g] = jnp.where((jnp.concatenate(cnt, axis=0) < N_SEL) & allowed, 1.0, 0.0)

        for r in range(n_qtile):
            q4_r = jnp.concatenate([q4[h * QB + r * LANE:h * QB + (r + 1) * LANE] for h in range(HPG_A)], axis=0)
            s_w, v_w = [], []
            for j in range(r, r + WINDOW // LANE + 1):
                c = qb * n_qtile - WINDOW // LANE + j
                k0 = pl.multiple_of(jnp.maximum(c, 0) * LANE, LANE)
                slot = jnp.where(c >= 0, j, n_wchunk)
                sj = _nt_dot(k_ref[pl.ds(k0, LANE), LANE + g * DK_A:LANE + (g + 1) * DK_A], q4_r)
                s_w.append(jnp.concatenate(
                    [sj[:, h * LANE:(h + 1) * LANE] + bw_sc[g, h * n_wslot + slot, :, r * LANE:(r + 1) * LANE]
                     for h in range(HPG_A)], axis=1))
                v_w.append(vt_ref[G_A * LANE + g * LANE:G_A * LANE + (g + 1) * LANE, pl.ds(k0, LANE)]
                           + ones_row[:, :LANE])
            mx = s_w[0]
            for sj in s_w[1:]:
                mx = jnp.maximum(mx, sj)
            mx = jnp.max(mx, axis=0, keepdims=True)
            o_win = _dot(v_w[0], jnp.exp2((s_w[0] - mx).astype(BF16)))
            for sj, vj in zip(s_w[1:], v_w[1:]):
                o_win = o_win + _dot(vj, jnp.exp2((sj - mx).astype(BF16)))
            ow_sc[g, r] = o_win

    chosen_any = jnp.maximum(sel_sc[0], sel_sc[1])
    blk_per_chunk = CK // L_SEL
    n_causal = ((qb + 1) * QB - 1) // CK + 1
    n_chunk = jnp.int32(0)
    for c in range(s_sc.shape[1]):
        need = (jnp.max(chosen_any[c * blk_per_chunk:(c + 1) * blk_per_chunk, :]) > 0.5) & (c < n_causal)
        idx_sc[n_chunk] = c
        n_chunk = n_chunk + jnp.where(need, 1, 0)

    def score_chunk(i, m_runs):
        c = idx_sc[i]
        k0 = pl.multiple_of(c * CK, CK)
        out = []
        for g in range(G_A):
            sc = _nt_dot(k_ref[pl.ds(k0, CK), gls[g]], q4s[g])
            halves = []
            for u in range(CK // LANE):
                dist = tk + (t0 - k0 - u * LANE).astype(F32)
                blk0 = (CK // L_SEL) * c + (LANE // L_SEL) * u
                chosen = jnp.concatenate(
                    [jnp.broadcast_to(sel_sc[g, pl.ds(blk0 + b, 1), :], (L_SEL, QB))
                     for b in range(LANE // L_SEL)], axis=0)
                ok = (chosen > 0.5) & (dist >= 0.0)
                far = jnp.where(ok, dist, -NEG)
                halves.append(biased(sc[u * LANE:(u + 1) * LANE], lambda h: far * -slopes[g][h]))
            sc = jnp.concatenate(halves, axis=0)
            s_sc[g, i] = sc
            out.append(jnp.maximum(m_runs[g], jnp.max(sc.reshape(CK // SUBLANE, SUBLANE, cols), axis=0)))
        return tuple(out)

    m_runs = lax.fori_loop(0, n_chunk, score_chunk, (jnp.full((SUBLANE, cols), NEG, F32),) * G_A)
    m_rows = [jnp.max(m_runs[g], axis=0, keepdims=True) for g in range(G_A)]
    acc_sc[...] = jnp.zeros(acc_sc.shape, F32)

    def value_chunk(i, carry):
        k0 = pl.multiple_of(idx_sc[i] * CK, CK)
        for g in range(G_A):
            pc = jnp.exp2((s_sc[g, i] - m_rows[g]).astype(BF16))
            acc_sc[g] += _dot(vt_ref[g * LANE:(g + 1) * LANE, pl.ds(k0, CK)] + ones_row, pc)
        return carry

    lax.fori_loop(0, n_chunk, value_chunk, 0)

    pieces = []
    for g in range(G_A):
        o_sel = acc_sc[g]
        for h in range(HPG_A):
            col = g * HPG_A + h
            os_h = o_sel[:, hs[h]]
            ow_h = jnp.concatenate([ow_sc[g, r, :, h * LANE:(h + 1) * LANE] for r in range(n_qtile)], axis=1)
            w_sel = gates_t[NH_A + col:NH_A + col + 1, :] / jnp.maximum(os_h[DK_A:DK_A + 1, :], 1e-30)
            w_win = gates_t[2 * NH_A + col:2 * NH_A + col + 1, :] / jnp.maximum(ow_h[DK_A:DK_A + 1, :], 1e-30)
            pieces.append(gates_t[col:col + 1, :] * o_cmps[g][:, hs[h]] + w_sel * os_h[:DK_A] + w_win * ow_h[:DK_A])

    out_t = jnp.concatenate(pieces, axis=0)
    out = jnp.concatenate(
        [jnp.concatenate([jnp.transpose(out_t[i * LANE:(i + 1) * LANE, r * LANE:(r + 1) * LANE])
                          for i in range(D_A // LANE)], axis=1) for r in range(n_qtile)], axis=0)
    o_ref[...] = (out * _silu(z_ref[...].astype(F32))).astype(BF16)


def _nsa(main, kk, vt, small, kcmp, vcmp_t, bsz, t_len):
    nqb = t_len // QB
    cols = HPG_A * QB
    n_wslot = (WINDOW + QB) // LANE + 1
    return pl.pallas_call(
        _nsa_kernel,
        grid=(bsz, nqb),
        in_specs=[pl.BlockSpec((QB, D_A), lambda b, i: (b * nqb + i, COL_AQ)),
                  pl.BlockSpec((QB, D_A), lambda b, i: (b * nqb + i, COL_AZ)),
                  pl.BlockSpec((QB, LANE), lambda b, i: (b * nqb + i, 0)),
                  pl.BlockSpec((1, G_A, N_CMP_PAD, DK_A), lambda b, i: (b, 0, 0, 0)),
                  pl.BlockSpec((1, G_A * DK_A, N_CMP_PAD), lambda b, i: (b, 0, 0)),
                  pl.BlockSpec((t_len, K_W), lambda b, i: (b, 0)),
                  pl.BlockSpec((NSA_VT_W, t_len), lambda b, i: (0, b))],
        out_specs=pl.BlockSpec((QB, D_A), lambda b, i: (b * nqb + i, 0)),
        out_shape=jax.ShapeDtypeStruct((bsz * t_len, D_A), BF16),
        scratch_shapes=[pltpu.VMEM((G_A, t_len // CK, CK, cols), F32), pltpu.VMEM((G_A, LANE, cols), F32),
                        pltpu.VMEM((G_A, QB // LANE, LANE, HPG_A * LANE), F32),
                        pltpu.VMEM((G_A, HPG_A * n_wslot, LANE, QB), F32), pltpu.VMEM((G_A, N_BLK, QB), F32),
                        pltpu.SMEM((t_len // CK + 1,), jnp.int32)],
        compiler_params=_params("arbitrary", "arbitrary"),
        name="nsa_attention",
    )(main, main, small, kcmp, vcmp_t, kk, vt)


def _sgu_tile(u_ref, v_ref, z_ref, lng_ref, lnb_ref, w_ref, b_ref):
    u = jax.nn.gelu(u_ref[...].astype(F32))
    v = jax.nn.gelu(v_ref[...].astype(F32))
    mu = jnp.mean(v, axis=-1, keepdims=True)
    var = jnp.mean(jnp.square(v - mu), axis=-1, keepdims=True)
    vn = ((v - mu) * lax.rsqrt(var + LN_EPS) * lng_ref[...] + lnb_ref[...]).astype(BF16)
    gate = u * _silu(z_ref[...].astype(F32))
    ti = lax.broadcasted_iota(jnp.int32, (CHUNK_B, CHUNK_B), 0)
    si = lax.broadcasted_iota(jnp.int32, (CHUNK_B, CHUNK_B), 1)
    ws = [jnp.where(si <= ti, w_ref[g], 0.0).astype(BF16) for g in range(G_B)]
    rows = []
    for c in range(u_ref.shape[0] // CHUNK_B):
        rs = slice(c * CHUNK_B, (c + 1) * CHUNK_B)
        rows.append(jnp.concatenate(
            [gate[rs, g * LANE:(g + 1) * LANE] * (_dot(ws[g], vn[rs, g * LANE:(g + 1) * LANE]) + b_ref[:, g:g + 1])
             for g in range(G_B)], axis=1))
    return jnp.concatenate(rows, axis=0)


def _mlstm_kernel(q_ref, k_ref, vt_ref, o_ref, z_ref, g_ref, cw_ref, cb_ref, gb_ref, ng_ref, y_ref,
                  qc_sc, kc_sc, halo_sc, st_sc, m_sc):
    sb = pl.program_id(1)
    blk = q_ref.shape[0]

    @pl.when(sb == 0)
    def _():
        halo_sc[...] = jnp.zeros(halo_sc.shape, F32)
        st_sc[...] = jnp.zeros(st_sc.shape, F32)
        m_sc[...] = jnp.zeros(m_sc.shape, F32)

    r8 = lax.broadcasted_iota(jnp.int32, (SUBLANE, 1), 0)

    def conv_silu(x_ref, which, dst_ref, scale):
        x = x_ref[...].astype(F32)
        prev = halo_sc[which]
        w = cw_ref[:, which * D_C:(which + 1) * D_C]
        acc = x * w[CONV_W - 1:CONV_W, :] + cb_ref[:, which * D_C:(which + 1) * D_C]
        for j in range(1, CONV_W):
            rolled = pltpu.roll(x, j, 0)
            head = jnp.where(r8 < j, pltpu.roll(prev, j, 0), rolled[0:SUBLANE])
            shifted = jnp.concatenate([head, rolled[SUBLANE:]], axis=0)
            acc = acc + shifted * w[CONV_W - 1 - j:CONV_W - j, :]
        halo_sc[which] = x[blk - SUBLANE:blk]
        dst_ref[...] = (_silu(acc) * scale).astype(BF16)

    conv_silu(q_ref, 0, qc_sc, 1.0)
    conv_silu(k_ref, 1, kc_sc, DH_C ** -0.5)

    si = lax.broadcasted_iota(jnp.int32, (CHUNK_C, CHUNK_C), 0)
    ti = lax.broadcasted_iota(jnp.int32, (CHUNK_C, CHUNK_C), 1)
    causal = si <= ti
    tri_u = jnp.where(causal, 1.0, 0.0).astype(BF16)
    ones_rows = jnp.where(lax.broadcasted_iota(jnp.int32, (DVX - DH_C, CHUNK_C), 0) == 0, 1.0, 0.0).astype(BF16)
    ng = ng_ref[...]
    gbias = gb_ref[...]

    for c in range(blk // CHUNK_C):
        rs = slice(c * CHUNK_C, (c + 1) * CHUNK_C)
        gate_rows = jnp.transpose(g_ref[rs, :] + gbias)[GI_LANE:GI_LANE + 2 * NH_C]
        f1, f2, f3 = _split3(_log_sigmoid(gate_rows))
        bc_rows = _dot(f1, tri_u) + _dot(f2, tri_u) + _dot(f3, tri_u)
        g_rows = gate_rows[0:NH_C] - bc_rows[NH_C:2 * NH_C]
        g_cols = jnp.transpose(jnp.concatenate([g_rows, jnp.zeros((CHUNK_C - NH_C, CHUNK_C), F32)], axis=0))
        for h in range(NH_C):
            ls = slice(h * DH_C, (h + 1) * DH_C)
            q = qc_sc[rs, ls]
            k = kc_sc[rs, ls]
            vext = jnp.concatenate([vt_ref[ls, rs], ones_rows], axis=0)
            g_col = g_cols[:, h:h + 1]
            g_row = g_rows[h:h + 1, :]
            bc_row = bc_rows[NH_C + h:NH_C + h + 1, :]
            b_last = bc_row[:, CHUNK_C - 1:CHUNK_C]
            m_prev = m_sc[h][:, 0:1]
            st = st_sc[h]
            m_loc = b_last + jnp.max(g_row, axis=1, keepdims=True)
            e_end = jnp.exp(b_last + g_row - m_loc)

            gmat = jnp.where(causal, g_col, -jnp.inf)
            mg = jnp.maximum(m_prev, jnp.max(gmat, axis=0, keepdims=True))
            s_t = _nt_dot(k, q) * jnp.exp(gmat - mg)
            e_int = jnp.exp(m_prev - mg)
            both = _dot(vext, s_t.astype(BF16)) + _nt_dot(st.astype(BF16), q) * e_int
            den = both[DH_C:DH_C + 1, :]
            hval = both[:DH_C] * (1.0 / jnp.maximum(jnp.abs(den), jnp.exp(-(bc_row + mg))))
            mu = jnp.mean(hval, axis=0, keepdims=True)
            var = jnp.mean(jnp.square(hval - mu), axis=0, keepdims=True)
            hn = jnp.transpose((hval - mu) * lax.rsqrt(var + LN_EPS))
            o_gate = o_ref[rs, ls].astype(F32)
            z_gate = z_ref[rs, ls].astype(F32)
            gated = hn * ng[:, ls] * _sigmoid(o_gate) * _silu(z_gate)
            y_ref[rs, ls] = gated.astype(BF16)

            loc = _dot((vext.astype(F32) * e_end).astype(BF16), k)
            m_new = jnp.maximum(b_last + m_prev, m_loc)
            st_sc[h] = jnp.exp(b_last + m_prev - m_new) * st + jnp.exp(m_loc - m_new) * loc
            m_sc[h] = jnp.broadcast_to(m_new, (1, LANE))


def _mlstm(main, vt, small, conv_w, conv_b, gbias, norm_g, layer, bsz, t_len):
    nsb = t_len // SEQ_BLK_C
    blk = lambda col: pl.BlockSpec((SEQ_BLK_C, D_C), lambda b, s: (b * nsb + s, col))
    full = lambda a: _layer_spec(a, layer, 2)
    return pl.pallas_call(
        _mlstm_kernel,
        grid=(bsz, nsb),
        in_specs=[blk(COL_CQ), blk(COL_CK),
                  pl.BlockSpec((D_C, SEQ_BLK_C), lambda b, s: (NSA_VT_W // D_C, b * nsb + s)),
                  blk(COL_CO), blk(COL_CZ),
                  pl.BlockSpec((SEQ_BLK_C, LANE), lambda b, s: (b * nsb + s, 0)),
                  full(conv_w), full(conv_b), full(gbias), full(norm_g)],
        out_specs=pl.BlockSpec((SEQ_BLK_C, D_C), lambda b, s: (b * nsb + s, 0)),
        out_shape=jax.ShapeDtypeStruct((bsz * t_len, D_C), BF16),
        scratch_shapes=[pltpu.VMEM((SEQ_BLK_C, D_C), BF16), pltpu.VMEM((SEQ_BLK_C, D_C), BF16),
                        pltpu.VMEM((2, SUBLANE, D_C), F32), pltpu.VMEM((NH_C, DVX, DH_C), F32),
                        pltpu.VMEM((NH_C, 1, LANE), F32)],
        compiler_params=_params("parallel", "arbitrary"),
        name="mlstm",
    )(main, main, vt, main, main, small, conv_w, conv_b, gbias, norm_g)


def _merge_kernel(x_ref, ya_ref, yc_ref, u_ref, v_ref, z_ref, g0_ref, g1_ref, g2_ref, p_ref,
                  sgg_ref, sgb_ref, sgw_ref, sgbt_ref,
                  wa_ref, wb_ref, wc_ref, wo_ref, wp_ref, wg_ref, lng_ref, lnb_ref, o_ref, *ob_ref):
    yb = _sgu_tile(u_ref, v_ref, z_ref, sgg_ref, sgb_ref, sgw_ref, sgbt_ref).astype(BF16)
    merged = (_sigmoid(g0_ref[...].astype(F32)) * _dot(ya_ref[...], wa_ref[...])
              + _sigmoid(g1_ref[...].astype(F32)) * _dot(yb, wb_ref[...])
              + _sigmoid(g2_ref[...].astype(F32)) * _dot(yc_ref[...], wc_ref[...]))
    r = ALPHA * x_ref[...] + _dot(merged.astype(BF16), wo_ref[...])
    r = r + _sigmoid(_dot(r.astype(BF16), wg_ref[...])) * _dot(p_ref[...].astype(BF16), wp_ref[...])
    mu = jnp.mean(r, axis=-1, keepdims=True)
    var = jnp.mean(jnp.square(r - mu), axis=-1, keepdims=True)
    y = (r - mu) * lax.rsqrt(var + LN_EPS) * lng_ref[...] + lnb_ref[...]
    o_ref[...] = y
    for ref in ob_ref:
        ref[...] = y.astype(BF16)


def _merge(x, ya, yc, main, p, sgg, sgb, sgw, sgbt, wa, wb, wc, wo, wp, wg, lng, lnb, layer, tm):
    n = x.shape[0]
    n_out = 1 if layer == DEPTH - 1 else 2
    row = lambda w: pl.BlockSpec((tm, w), lambda i: (i, 0))
    blk = lambda col: pl.BlockSpec((tm, D_B), lambda i: (i, col))
    gate = lambda j: pl.BlockSpec((tm, D_MODEL), lambda i: (i, j))
    full = lambda a: _layer_spec(a, layer, 1)
    return pl.pallas_call(
        _merge_kernel,
        grid=(n // tm,),
        in_specs=[row(D_MODEL), row(D_A), row(D_C), blk(COL_BU), blk(COL_BV), blk(COL_BZ),
                  gate(0), gate(1), gate(2), pl.BlockSpec((None, tm, PLE_DIM), lambda i: (layer, i, 0)),
                  full(sgg), full(sgb), full(sgw), full(sgbt),
                  full(wa), full(wb), full(wc), full(wo), full(wp), full(wg), full(lng), full(lnb)],
        out_specs=[row(D_MODEL)] * n_out,
        out_shape=[jax.ShapeDtypeStruct((n, D_MODEL), F32), jax.ShapeDtypeStruct((n, D_MODEL), BF16)][:n_out],
        compiler_params=_params("parallel"),
        name="merge",
    )(x, ya, yc, main, main, main, main, main, main, p, sgg, sgb, sgw, sgbt, wa, wb, wc, wo, wp, wg, lng, lnb)


def _regroup_kernel(wt_ref, main_ref, k_ref, v_ref, aux_ref):
    wt = wt_ref[...]
    cols = wt.shape[1]
    run = lambda first, last: wt[SEG[first][0]:SEG[last][1]]
    zeros = lambda n: jnp.zeros((n, cols), F32)
    main_ref[...] = jnp.concatenate([run('m_g', 'm_g'), run('a_q', 'a_q'), run('a_z', 'c_k'), run('c_o', 'c_z')],
                                    axis=0).astype(BF16)
    k_ref[...] = jnp.concatenate([run('a_ks', 'a_ks'), run('a_kw', 'a_kw')], axis=0).astype(BF16)
    v_rows = []
    for name in ('a_vs', 'a_vw'):
        for g in range(G_A):
            lo = SEG[name][0] + g * DK_A
            v_rows += [wt[lo:lo + DK_A], zeros(LANE - DK_A)]
    v_ref[...] = jnp.concatenate(v_rows + [run('c_v', 'c_v')], axis=0).astype(BF16)
    n_ag = SEG['a_g'][1] - SEG['a_g'][0]
    aux_ref[...] = jnp.concatenate([run('a_kc', 'a_vc'), run('a_g', 'a_g'), zeros(GI_LANE - n_ag), run('c_if', 'c_if'),
                                    zeros(LANE - GI_LANE - 2 * NH_C)], axis=0).astype(BF16)


def _regroup(w_in_t, cols):
    nl, n_in, d = w_in_t.shape
    blk = lambda rows: pl.BlockSpec((None, rows, cols), lambda l, c: (l, 0, c))
    return pl.pallas_call(
        _regroup_kernel,
        grid=(nl, d // cols),
        in_specs=[blk(n_in)],
        out_specs=[blk(MAIN_W), blk(K_W), blk(VT_W), blk(3 * LANE)],
        out_shape=[jax.ShapeDtypeStruct((nl, MAIN_W, d), BF16), jax.ShapeDtypeStruct((nl, K_W, d), BF16),
                   jax.ShapeDtypeStruct((nl, VT_W, d), BF16), jax.ShapeDtypeStruct((nl, 3 * LANE, d), BF16)],
        compiler_params=_params("parallel", "parallel"),
        name="regroup_w_in",
    )(w_in_t)


def _prepare(w_in, cmp_wv2, sg_ln_g, sg_ln_b, sg_b,
             ml_conv_b, ml_b_i, ml_b_f, ml_norm_g, w_br_a, w_br_b, w_br_c, w_out, ple_w, ple_gate, ln_g, ln_b):
    nl = w_in.shape[0]
    prm = {}
    prm['w_main'], prm['w_k'], prm['w_vt'], prm['w_aux'] = _regroup(jnp.swapaxes(w_in, 1, 2), REGROUP_COLS)
    prm['wv2t'] = jnp.swapaxes(cmp_wv2, 1, 2)
    prm['sg_ln_g'], prm['sg_ln_b'] = sg_ln_g[:, None, :], sg_ln_b[:, None, :]
    prm['sg_b_t'] = jnp.swapaxes(sg_b, 1, 2)
    gbias = jnp.zeros((nl, 1, LANE), F32).at[:, 0, GI_LANE:GI_LANE + NH_C].set(ml_b_i)
    prm['gbias'] = gbias.at[:, 0, GF_LANE:GF_LANE + NH_C].set(ml_b_f)
    prm['conv_b'], prm['norm_g'] = ml_conv_b[:, None, :], ml_norm_g[:, None, :]
    for name, a in (('wa', w_br_a), ('wb', w_br_b), ('wc', w_br_c), ('wo', w_out), ('wp', ple_w), ('wg', ple_gate)):
        prm[name] = a.astype(BF16)
    prm['ln_g'], prm['ln_b'] = ln_g[:, None, :], ln_b[:, None, :]
    return prm


def _layer(i, x, xin, p, prm, cmp, sg_w, ml_conv_w, bsz, t_len):
    main = _matmul(xin, prm['w_main'], i, PROJ_TM, PROJ_TN, "proj_main")
    kk, vt, kc, vc, small = _tail_matmul(xin, prm['w_k'], prm['w_vt'], prm['w_aux'], i, TAIL_TM)

    pos_k, pos_v, wk1, wk2, wv1 = cmp
    kcmp, vcmp_t = _compress(kc, vc, pos_k, pos_v, wk1, wk2, wv1, prm['wv2t'], i, bsz, t_len)
    ya = _nsa(main, kk, vt, small, kcmp, vcmp_t, bsz, t_len)
    yc = _mlstm(main, vt, small, ml_conv_w, prm['conv_b'], prm['gbias'], prm['norm_g'], i, bsz, t_len)
    return _merge(x, ya, yc, main, p, prm['sg_ln_g'], prm['sg_ln_b'], sg_w, prm['sg_b_t'],
                  prm['wa'], prm['wb'], prm['wc'], prm['wo'], prm['wp'], prm['wg'], prm['ln_g'], prm['ln_b'], i, MERGE_TM)


def kernel(x, p, w_in, cmp_pos_k, cmp_pos_v, cmp_wk1, cmp_wk2, cmp_wv1, cmp_wv2, sg_ln_g, sg_ln_b, sg_w, sg_b,
           ml_conv_w, ml_conv_b, ml_b_i, ml_b_f, ml_norm_g, w_br_a, w_br_b, w_br_c, w_out, ple_w, ple_gate,
           ln_g, ln_b):
    bsz, t_len, d = x.shape
    assert d == D_MODEL and t_len == N_CMP_PAD * STRIDE_CMP and t_len // L_SEL == N_BLK
    assert t_len % SEQ_BLK_C == 0 and t_len % CK == 0 and w_in.shape[0] == DEPTH
    prm = _prepare(w_in, cmp_wv2, sg_ln_g, sg_ln_b, sg_b,
                   ml_conv_b, ml_b_i, ml_b_f, ml_norm_g, w_br_a, w_br_b, w_br_c, w_out, ple_w, ple_gate, ln_g, ln_b)
    xf = x.reshape(bsz * t_len, d)
    xin = xf
    pf = p.reshape(DEPTH, bsz * t_len, PLE_DIM)
    for i in range(DEPTH):
        outs = _layer(i, xf, xin, pf, prm, (cmp_pos_k, cmp_pos_v, cmp_wk1, cmp_wk2, cmp_wv1), sg_w, ml_conv_w,
                      bsz, t_len)
        xf, xin = outs[0], outs[-1]
    return xf.reshape(bsz, t_len, d)
```

```python
import jax
import jax.numpy as jnp
from jax import lax
from jax.experimental import pallas as pl
from jax.experimental.pallas import tpu as pltpu

F32 = jnp.float32
BF16 = jnp.bfloat16

D_MODEL = 1024
PLE_DIM = 256
D_A = 512
NH_A = 8
G_A = 2
HPG_A = 4
DK_A = 64
L_CMP = 32
STRIDE_CMP = 16
CMP_HID = 128
L_SEL = 64
N_SEL = 8
N_BLK = 32
WINDOW = 256
QB = 256
CK = 256
N_CMP_PAD = 128
BIG = 1e9
D_B = 512
CHUNK_B = 128
G_B = 4
D_C = 512
NH_C = 4
DH_C = 128
CONV_W = 4
CHUNK_C = 128
SEQ_BLK_C = 1024
LN_EPS = 1e-5
DEPTH = 2
ALPHA = (2.0 * DEPTH) ** 0.25

LANE = 128
SUBLANE = 8
_SEG_SIZES = (('a_q', D_A), ('a_kc', G_A * DK_A), ('a_vc', G_A * DK_A), ('a_ks', G_A * DK_A), ('a_vs', G_A * DK_A),
              ('a_kw', G_A * DK_A), ('a_vw', G_A * DK_A), ('a_g', 3 * NH_A), ('a_z', D_A),
              ('b_u', D_B), ('b_v', D_B), ('b_z', D_B),
              ('c_q', D_C), ('c_k', D_C), ('c_v', D_C), ('c_if', 2 * NH_C), ('c_o', D_C), ('c_z', D_C),
              ('m_g', 3 * D_MODEL))
SEG = {}
for _name, _size in _SEG_SIZES:
    _lo = sum(s for _, s in _SEG_SIZES[:len(SEG)])
    SEG[_name] = (_lo, _lo + _size)
MAIN_W = 7680
COL_AQ, COL_AZ, COL_BU, COL_BV, COL_BZ, COL_CQ, COL_CK, COL_CO, COL_CZ = 6, 7, 8, 9, 10, 11, 12, 13, 14
K_W = 256
NSA_VT_W = 512
VT_W = NSA_VT_W + 512
DVX = 128 + 16
GI_LANE = 32
GF_LANE = 36
V7X_VMEM_BYTES = 64 * 1024 * 1024
VMEM_LIMIT = V7X_VMEM_BYTES * 13 // 16
PROJ_TM, PROJ_TN = 2048, 1536
TAIL_TM = 2048
MERGE_TM = 512
REGROUP_COLS = 256
NEG = -1e30
LOG2E = 1.4426950408889634


def _params(*sem):
    return pltpu.CompilerParams(dimension_semantics=sem, vmem_limit_bytes=VMEM_LIMIT)


def _nt_dot(a, b):
    return lax.dot_general(a, b, (((1,), (1,)), ((), ())), preferred_element_type=F32)


def _dot(a, b):
    return jnp.dot(a, b, preferred_element_type=F32)


def _split3(a):
    a1 = a.astype(BF16)
    r1 = a - a1.astype(F32)
    a2 = r1.astype(BF16)
    a3 = (r1 - a2.astype(F32)).astype(BF16)
    return a1, a2, a3


def _sigmoid(x):
    return 0.5 * jnp.tanh(0.5 * x) + 0.5


def _silu(x):
    return x * _sigmoid(x)


def _log_sigmoid(x):
    return jnp.minimum(x, 0.0) - jnp.log1p(jnp.exp(-jnp.abs(x)))


def _layer_spec(a, layer, grid_rank):
    zeros = (0,) * (a.ndim - 1)
    if grid_rank == 1:
        return pl.BlockSpec((None,) + a.shape[1:], lambda i: (layer,) + zeros)
    return pl.BlockSpec((None,) + a.shape[1:], lambda i, j: (layer,) + zeros)


def _mm_kernel(x_ref, wt_ref, o_ref):
    o_ref[...] = _nt_dot(x_ref[...].astype(BF16), wt_ref[...]).astype(o_ref.dtype)


def _matmul(x, wt, layer, tm, tn, name):
    m, k = x.shape
    n = wt.shape[1]
    return pl.pallas_call(
        _mm_kernel,
        grid=(m // tm, n // tn),
        in_specs=[pl.BlockSpec((tm, k), lambda i, j: (i, 0)),
                  pl.BlockSpec((None, tn, k), lambda i, j: (layer, j, 0))],
        out_specs=pl.BlockSpec((tm, tn), lambda i, j: (i, j)),
        out_shape=jax.ShapeDtypeStruct((m, n), BF16),
        compiler_params=_params("parallel", "arbitrary"),
        name=name,
    )(x, wt)


def _tail_kernel(x_ref, wkt_ref, wvt_ref, wauxt_ref, ok_ref, ovt_ref, okc_ref, ovc_ref, osm_ref):
    x = x_ref[...].astype(BF16)
    ok_ref[...] = _nt_dot(x, wkt_ref[...]).astype(BF16)
    ovt_ref[...] = _nt_dot(wvt_ref[...], x).astype(BF16)
    aux = _nt_dot(x, wauxt_ref[...])
    okc_ref[...] = aux[:, 0:LANE]
    ovc_ref[...] = aux[:, LANE:2 * LANE]
    osm_ref[...] = aux[:, 2 * LANE:3 * LANE]


def _tail_matmul(x, wk, wvt, waux, layer, tm):
    m, k = x.shape
    tok = lambda w: pl.BlockSpec((tm, w), lambda i: (i, 0))
    f32_out = jax.ShapeDtypeStruct((m, LANE), F32)
    return pl.pallas_call(
        _tail_kernel,
        grid=(m // tm,),
        in_specs=[tok(k), _layer_spec(wk, layer, 1), _layer_spec(wvt, layer, 1), _layer_spec(waux, layer, 1)],
        out_specs=[tok(K_W), pl.BlockSpec((VT_W, tm), lambda i: (0, i)), tok(LANE), tok(LANE), tok(LANE)],
        out_shape=[jax.ShapeDtypeStruct((m, K_W), BF16), jax.ShapeDtypeStruct((VT_W, m), BF16),
                   f32_out, f32_out, f32_out],
        compiler_params=_params("parallel"),
        name="proj_tail",
    )(x, wk, wvt, waux)


def _compress_kernel(zk_ref, zv_ref, pk_ref, pv_ref, wk1_ref, wk2_ref, wv1_ref, wv2t_ref, ko_ref, vot_ref):
    n = zk_ref.shape[0] // STRIDE_CMP

    def hidden(z_ref, p_ref, w1_ref):
        r0 = [jnp.zeros((n, CMP_HID), F32) for _ in range(G_A)]
        r1 = [jnp.zeros((n, CMP_HID), F32) for _ in range(G_A)]
        for l in range(STRIDE_CMP):
            zl = z_ref[pl.ds(l, n, stride=STRIDE_CMP), :]
            w_a = w1_ref[l * DK_A:(l + 1) * DK_A, :].astype(BF16)
            w_b = w1_ref[(STRIDE_CMP + l) * DK_A:(STRIDE_CMP + l + 1) * DK_A, :].astype(BF16)
            for g in range(G_A):
                zg = zl[:, g * DK_A:(g + 1) * DK_A]
                r0[g] = r0[g] + _dot((zg + p_ref[l:l + 1, :]).astype(BF16), w_a)
                r1[g] = r1[g] + _dot((zg + p_ref[STRIDE_CMP + l:STRIDE_CMP + l + 1, :]).astype(BF16), w_b)
        return [jax.nn.gelu(r0[g] + pltpu.roll(r1[g], n - 1, 0)).astype(BF16) for g in range(G_A)]

    hk = hidden(zk_ref, pk_ref, wk1_ref)
    wk2 = wk2_ref[...].astype(BF16)
    for g in range(G_A):
        ko_ref[0, g] = _dot(hk[g], wk2).astype(BF16)
    hv = hidden(zv_ref, pv_ref, wv1_ref)
    wv2t = wv2t_ref[...].astype(BF16)
    vot_ref[0] = jnp.concatenate([_nt_dot(wv2t, hv[g]) for g in range(G_A)], axis=0).astype(BF16)


def _compress(kc, vc, pk, pv, wk1, wk2, wv1, wv2t, layer, bsz, t_len):
    n = t_len // STRIDE_CMP
    zspec = pl.BlockSpec((t_len, LANE), lambda i: (i, 0))
    full = lambda a: _layer_spec(a, layer, 1)
    return pl.pallas_call(
        _compress_kernel,
        grid=(bsz,),
        in_specs=[zspec, zspec, full(pk), full(pv), full(wk1), full(wk2), full(wv1), full(wv2t)],
        out_specs=[pl.BlockSpec((1, G_A, n, DK_A), lambda i: (i, 0, 0, 0)),
                   pl.BlockSpec((1, G_A * DK_A, n), lambda i: (i, 0, 0))],
        out_shape=[jax.ShapeDtypeStruct((bsz, G_A, n, DK_A), BF16), jax.ShapeDtypeStruct((bsz, G_A * DK_A, n), BF16)],
        compiler_params=_params("parallel"),
        name="nsa_compress",
    )(kc, vc, pk, pv, wk1, wk2, wv1, wv2t)


def _nsa_kernel(q_ref, z_ref, g_ref, kc_ref, vct_ref, k_ref, vt_ref, o_ref,
                s_sc, acc_sc, ow_sc, bw_sc, sel_sc, idx_sc):
    qb = pl.program_id(1)
    t0 = qb * QB
    cols = HPG_A * QB
    n_qtile = QB // LANE
    n_wchunk = (WINDOW + QB) // LANE
    n_wslot = n_wchunk + 1
    k_i = lax.broadcasted_iota(jnp.int32, (LANE, 1), 0)
    t_i = lax.broadcasted_iota(jnp.int32, (1, QB), 1)
    tk = (t_i - k_i).astype(F32)
    ones_row = jnp.where(lax.broadcasted_iota(jnp.int32, (LANE, CK), 0) == DK_A, 1.0, 0.0).astype(BF16)
    slopes = [[LOG2E * 2.0 ** -(g * HPG_A + h + 1) for h in range(HPG_A)] for g in range(G_A)]
    hs = [slice(h * QB, (h + 1) * QB) for h in range(HPG_A)]

    def biased(s, bias_of_head):
        return jnp.concatenate([s[:, hs[h]] + bias_of_head(h) for h in range(HPG_A)], axis=1)

    @pl.when((pl.program_id(0) == 0) & (qb == 0))
    def _():
        for g in range(G_A):
            for h in range(HPG_A):
                for j in range(n_wchunk):
                    dist = tk + float(WINDOW - j * LANE)
                    ok = (dist >= 0.0) & (dist < WINDOW)
                    bw_sc[g, h * n_wslot + j] = jnp.where(ok, -slopes[g][h] * dist, NEG)
                bw_sc[g, h * n_wslot + n_wchunk] = jnp.full((LANE, QB), NEG, F32)

    gsig = _sigmoid(g_ref[...])
    gates_t = jnp.concatenate([jnp.transpose(gsig[r * LANE:(r + 1) * LANE]) for r in range(n_qtile)], axis=1)

    jb = lax.broadcasted_iota(jnp.int32, (N_BLK, QB), 0)
    cur = (t0 + lax.broadcasted_iota(jnp.int32, (N_BLK, QB), 1)) >> 6
    forced = (jb == 0) | (jb == cur) | (jb == cur - 1)
    allowed = jb <= cur
    ov_j = lax.broadcasted_iota(jnp.int32, (N_BLK, N_CMP_PAD), 0) * L_SEL
    ov_n = lax.broadcasted_iota(jnp.int32, (N_BLK, N_CMP_PAD), 1) * STRIDE_CMP
    ov_t = jnp.where((ov_n < ov_j + L_SEL) & (ov_n + L_CMP > ov_j), 1.0, 0.0).astype(BF16)
    jb8 = lax.broadcasted_iota(jnp.int32, (SUBLANE, QB), 0)

    d_cmp = (t0 - (L_CMP - 1)).astype(F32) + (t_i - STRIDE_CMP * k_i).astype(F32)
    valid_cmp = (d_cmp >= 0.0) & (k_i < N_CMP_PAD - 1)

    gls = [slice(g * DK_A, (g + 1) * DK_A) for g in range(G_A)]
    q4s, o_cmps = [], []
    for g in range(G_A):
        q4 = jnp.concatenate(
            [q_ref[:, (g * HPG_A + h) * DK_A:(g * HPG_A + h + 1) * DK_A] for h in range(HPG_A)], axis=0)
        q4 = (q4.astype(F32) * (LOG2E * DK_A ** -0.5)).astype(BF16)
        q4s.append(q4)

        s = biased(_nt_dot(kc_ref[0, g], q4),
                   lambda h: jnp.where(valid_cmp, -slopes[g][h] * d_cmp, -jnp.inf))
        mx = jnp.max(s, axis=0, keepdims=True)
        mx = jnp.where(mx > -jnp.inf, mx, 0.0)
        e = jnp.exp2(s - mx)
        p = e * (1.0 / jnp.maximum(jnp.sum(e, axis=0, keepdims=True), 1e-30))
        o_cmps.append(_dot(vct_ref[0][gls[g], :], p.astype(BF16)))

        psum = p[:, hs[0]] + p[:, hs[1]] + p[:, hs[2]] + p[:, hs[3]]
        p1, p2, p3 = _split3(psum)
        imp = _dot(ov_t, p1) + _dot(ov_t, p2) + _dot(ov_t, p3)
        score = jnp.where(allowed, jnp.where(forced, BIG, imp), -BIG)
        tiles = [score[r * SUBLANE:(r + 1) * SUBLANE] for r in range(N_BLK // SUBLANE)]
        cnt = [jnp.zeros((SUBLANE, QB), F32) for _ in tiles]
        for j in range(N_BLK):
            sj = score[j:j + 1, :]
            for r, tile in enumerate(tiles):
                lo = r * SUBLANE
                if j >= lo + SUBLANE:
                    beats = sj > tile
                elif j < lo:
                    beats = sj >= tile
                else:
                    beats = (sj > tile) | ((sj >= tile) & (jb8 > j - lo))
                cnt[r] = cnt[r] + jnp.where(beats, 1.0, 0.0)
        sel_sc[g] = jnp.where((jnp.concatenate(cnt, axis=0) < N_SEL) & allowed, 1.0, 0.0)

        for r in range(n_qtile):
            q4_r = jnp.concatenate([q4[h * QB + r * LANE:h * QB + (r + 1) * LANE] for h in range(HPG_A)], axis=0)
            s_w, v_w = [], []
            for j in range(r, r + WINDOW // LANE + 1):
                c = qb * n_qtile - WINDOW // LANE + j
                k0 = pl.multiple_of(jnp.maximum(c, 0) * LANE, LANE)
                slot = jnp.where(c >= 0, j, n_wchunk)
                sj = _nt_dot(k_ref[pl.ds(k0, LANE), LANE + g * DK_A:LANE + (g + 1) * DK_A], q4_r)
                s_w.append(jnp.concatenate(
                    [sj[:, h * LANE:(h + 1) * LANE] + bw_sc[g, h * n_wslot + slot, :, r * LANE:(r + 1) * LANE]
                     for h in range(HPG_A)], axis=1))
                v_w.append(vt_ref[G_A * LANE + g * LANE:G_A * LANE + (g + 1) * LANE, pl.ds(k0, LANE)]
                           + ones_row[:, :LANE])
            mx = s_w[0]
            for sj in s_w[1:]:
                mx = jnp.maximum(mx, sj)
            mx = jnp.max(mx, axis=0, keepdims=True)
            o_win = _dot(v_w[0], jnp.exp2((s_w[0] - mx).astype(BF16)))
            for sj, vj in zip(s_w[1:], v_w[1:]):
                o_win = o_win + _dot(vj, jnp.exp2((sj - mx).astype(BF16)))
            ow_sc[g, r] = o_win

    chosen_any = jnp.maximum(sel_sc[0], sel_sc[1])
    blk_per_chunk = CK // L_SEL
    n_all = s_sc.shape[1]
    per_chunk = jnp.concatenate(
        [jnp.max(chosen_any[c * blk_per_chunk:(c + 1) * blk_per_chunk, :], axis=0, keepdims=True)
         for c in range(n_all)], axis=0)
    bit = jnp.exp2(lax.broadcasted_iota(jnp.int32, (n_all, 1), 0).astype(F32))
    mask = jnp.sum(jnp.max(per_chunk, axis=1, keepdims=True) * bit).astype(jnp.int32)
    n_causal = ((qb + 1) * QB - 1) // CK + 1
    n_chunk = jnp.int32(0)
    for c in range(n_all):
        need = (((mask >> c) & 1) == 1) & (c < n_causal)
        idx_sc[n_chunk] = c
        n_chunk = n_chunk + jnp.where(need, 1, 0)

    def score_chunk(i, m_runs):
        c = idx_sc[i]
        k0 = pl.multiple_of(c * CK, CK)
        out = []
        for g in range(G_A):
            sc = _nt_dot(k_ref[pl.ds(k0, CK), gls[g]], q4s[g])
            halves = []
            for u in range(CK // LANE):
                dist = tk + (t0 - k0 - u * LANE).astype(F32)
                blk0 = (CK // L_SEL) * c + (LANE // L_SEL) * u
                chosen = jnp.concatenate(
                    [jnp.broadcast_to(sel_sc[g, pl.ds(blk0 + b, 1), :], (L_SEL, QB))
                     for b in range(LANE // L_SEL)], axis=0)
                ok = (chosen > 0.5) & (dist >= 0.0)
                far = jnp.where(ok, dist, -NEG)
                halves.append(biased(sc[u * LANE:(u + 1) * LANE], lambda h: far * -slopes[g][h]))
            sc = jnp.concatenate(halves, axis=0)
            s_sc[g, i] = sc
            out.append(jnp.maximum(m_runs[g], jnp.max(sc.reshape(CK // SUBLANE, SUBLANE, cols), axis=0)))
        return tuple(out)

    m_runs = lax.fori_loop(0, n_chunk, score_chunk, (jnp.full((SUBLANE, cols), NEG, F32),) * G_A)
    m_rows = [jnp.max(m_runs[g], axis=0, keepdims=True) for g in range(G_A)]
    acc_sc[...] = jnp.zeros(acc_sc.shape, F32)

    def value_chunk(i, carry):
        k0 = pl.multiple_of(idx_sc[i] * CK, CK)
        for g in range(G_A):
            pc = jnp.exp2((s_sc[g, i] - m_rows[g]).astype(BF16))
            acc_sc[g] += _dot(vt_ref[g * LANE:(g + 1) * LANE, pl.ds(k0, CK)] + ones_row, pc)
        return carry

    lax.fori_loop(0, n_chunk, value_chunk, 0)

    pieces = []
    for g in range(G_A):
        o_sel = acc_sc[g]
        for h in range(HPG_A):
            col = g * HPG_A + h
            os_h = o_sel[:, hs[h]]
            ow_h = jnp.concatenate([ow_sc[g, r, :, h * LANE:(h + 1) * LANE] for r in range(n_qtile)], axis=1)
            w_sel = gates_t[NH_A + col:NH_A + col + 1, :] / jnp.maximum(os_h[DK_A:DK_A + 1, :], 1e-30)
            w_win = gates_t[2 * NH_A + col:2 * NH_A + col + 1, :] / jnp.maximum(ow_h[DK_A:DK_A + 1, :], 1e-30)
            pieces.append(gates_t[col:col + 1, :] * o_cmps[g][:, hs[h]] + w_sel * os_h[:DK_A] + w_win * ow_h[:DK_A])

    out_t = jnp.concatenate(pieces, axis=0)
    out = jnp.concatenate(
        [jnp.concatenate([jnp.transpose(out_t[i * LANE:(i + 1) * LANE, r * LANE:(r + 1) * LANE])
                          for i in range(D_A // LANE)], axis=1) for r in range(n_qtile)], axis=0)
    o_ref[...] = (out * _silu(z_ref[...].astype(F32))).astype(BF16)


def _nsa(main, kk, vt, small, kcmp, vcmp_t, bsz, t_len):
    nqb = t_len // QB
    cols = HPG_A * QB
    n_wslot = (WINDOW + QB) // LANE + 1
    return pl.pallas_call(
        _nsa_kernel,
        grid=(bsz, nqb),
        in_specs=[pl.BlockSpec((QB, D_A), lambda b, i: (b * nqb + i, COL_AQ)),
                  pl.BlockSpec((QB, D_A), lambda b, i: (b * nqb + i, COL_AZ)),
                  pl.BlockSpec((QB, LANE), lambda b, i: (b * nqb + i, 0)),
                  pl.BlockSpec((1, G_A, N_CMP_PAD, DK_A), lambda b, i: (b, 0, 0, 0)),
                  pl.BlockSpec((1, G_A * DK_A, N_CMP_PAD), lambda b, i: (b, 0, 0)),
                  pl.BlockSpec((t_len, K_W), lambda b, i: (b, 0)),
                  pl.BlockSpec((NSA_VT_W, t_len), lambda b, i: (0, b))],
        out_specs=pl.BlockSpec((QB, D_A), lambda b, i: (b * nqb + i, 0)),
        out_shape=jax.ShapeDtypeStruct((bsz * t_len, D_A), BF16),
        scratch_shapes=[pltpu.VMEM((G_A, t_len // CK, CK, cols), F32), pltpu.VMEM((G_A, LANE, cols), F32),
                        pltpu.VMEM((G_A, QB // LANE, LANE, HPG_A * LANE), F32),
                        pltpu.VMEM((G_A, HPG_A * n_wslot, LANE, QB), F32), pltpu.VMEM((G_A, N_BLK, QB), F32),
                        pltpu.SMEM((t_len // CK + 1,), jnp.int32)],
        compiler_params=_params("arbitrary", "arbitrary"),
        name="nsa_attention",
    )(main, main, small, kcmp, vcmp_t, kk, vt)


def _sgu_tile(u_ref, v_ref, z_ref, lng_ref, lnb_ref, w_ref, b_ref):
    u = jax.nn.gelu(u_ref[...].astype(F32))
    v = jax.nn.gelu(v_ref[...].astype(F32))
    mu = jnp.mean(v, axis=-1, keepdims=True)
    var = jnp.mean(jnp.square(v - mu), axis=-1, keepdims=True)
    vn = ((v - mu) * lax.rsqrt(var + LN_EPS) * lng_ref[...] + lnb_ref[...]).astype(BF16)
    gate = u * _silu(z_ref[...].astype(F32))
    ti = lax.broadcasted_iota(jnp.int32, (CHUNK_B, CHUNK_B), 0)
    si = lax.broadcasted_iota(jnp.int32, (CHUNK_B, CHUNK_B), 1)
    ws = [jnp.where(si <= ti, w_ref[g], 0.0).astype(BF16) for g in range(G_B)]
    rows = []
    for c in range(u_ref.shape[0] // CHUNK_B):
        rs = slice(c * CHUNK_B, (c + 1) * CHUNK_B)
        rows.append(jnp.concatenate(
            [gate[rs, g * LANE:(g + 1) * LANE] * (_dot(ws[g], vn[rs, g * LANE:(g + 1) * LANE]) + b_ref[:, g:g + 1])
             for g in range(G_B)], axis=1))
    return jnp.concatenate(rows, axis=0)


def _mlstm_kernel(q_ref, k_ref, vt_ref, o_ref, z_ref, g_ref, cw_ref, cb_ref, gb_ref, ng_ref, y_ref,
                  qc_sc, kc_sc, halo_sc, st_sc, m_sc):
    sb = pl.program_id(1)
    blk = q_ref.shape[0]

    @pl.when(sb == 0)
    def _():
        halo_sc[...] = jnp.zeros(halo_sc.shape, F32)
        st_sc[...] = jnp.zeros(st_sc.shape, F32)
        m_sc[...] = jnp.zeros(m_sc.shape, F32)

    r8 = lax.broadcasted_iota(jnp.int32, (SUBLANE, 1), 0)

    def conv_silu(x_ref, which, dst_ref, scale):
        x = x_ref[...].astype(F32)
        prev = halo_sc[which]
        w = cw_ref[:, which * D_C:(which + 1) * D_C]
        acc = x * w[CONV_W - 1:CONV_W, :] + cb_ref[:, which * D_C:(which + 1) * D_C]
        for j in range(1, CONV_W):
            rolled = pltpu.roll(x, j, 0)
            head = jnp.where(r8 < j, pltpu.roll(prev, j, 0), rolled[0:SUBLANE])
            shifted = jnp.concatenate([head, rolled[SUBLANE:]], axis=0)
            acc = acc + shifted * w[CONV_W - 1 - j:CONV_W - j, :]
        halo_sc[which] = x[blk - SUBLANE:blk]
        dst_ref[...] = (_silu(acc) * scale).astype(BF16)

    conv_silu(q_ref, 0, qc_sc, 1.0)
    conv_silu(k_ref, 1, kc_sc, DH_C ** -0.5)

    si = lax.broadcasted_iota(jnp.int32, (CHUNK_C, CHUNK_C), 0)
    ti = lax.broadcasted_iota(jnp.int32, (CHUNK_C, CHUNK_C), 1)
    causal = si <= ti
    tri_u = jnp.where(causal, 1.0, 0.0).astype(BF16)
    ones_rows = jnp.where(lax.broadcasted_iota(jnp.int32, (DVX - DH_C, CHUNK_C), 0) == 0, 1.0, 0.0).astype(BF16)
    ng = ng_ref[...]
    gbias = gb_ref[...]

    for c in range(blk // CHUNK_C):
        rs = slice(c * CHUNK_C, (c + 1) * CHUNK_C)
        gate_rows = jnp.transpose(g_ref[rs, :] + gbias)[GI_LANE:GI_LANE + 2 * NH_C]
        f1, f2, f3 = _split3(_log_sigmoid(gate_rows))
        bc_rows = _dot(f1, tri_u) + _dot(f2, tri_u) + _dot(f3, tri_u)
        g_rows = gate_rows[0:NH_C] - bc_rows[NH_C:2 * NH_C]
        g_cols = jnp.transpose(jnp.concatenate([g_rows, jnp.zeros((CHUNK_C - NH_C, CHUNK_C), F32)], axis=0))
        for h in range(NH_C):
            ls = slice(h * DH_C, (h + 1) * DH_C)
            q = qc_sc[rs, ls]
            k = kc_sc[rs, ls]
            vext = jnp.concatenate([vt_ref[ls, rs], ones_rows], axis=0)
            g_col = g_cols[:, h:h + 1]
            g_row = g_rows[h:h + 1, :]
            bc_row = bc_rows[NH_C + h:NH_C + h + 1, :]
            b_last = bc_row[:, CHUNK_C - 1:CHUNK_C]
            m_prev = m_sc[h][:, 0:1]
            st = st_sc[h]
            m_loc = b_last + jnp.max(g_row, axis=1, keepdims=True)
            e_end = jnp.exp(b_last + g_row - m_loc)

            gmat = jnp.where(causal, g_col, -jnp.inf)
            mg = jnp.maximum(m_prev, jnp.max(gmat, axis=0, keepdims=True))
            s_t = _nt_dot(k, q) * jnp.exp(gmat - mg)
            e_int = jnp.exp(m_prev - mg)
            both = _dot(vext, s_t.astype(BF16)) + _nt_dot(st.astype(BF16), q) * e_int
            den = both[DH_C:DH_C + 1, :]
            hval = both[:DH_C] * (1.0 / jnp.maximum(jnp.abs(den), jnp.exp(-(bc_row + mg))))
            mu = jnp.mean(hval, axis=0, keepdims=True)
            var = jnp.mean(jnp.square(hval - mu), axis=0, keepdims=True)
            hn = jnp.transpose((hval - mu) * lax.rsqrt(var + LN_EPS))
            o_gate = o_ref[rs, ls].astype(F32)
            z_gate = z_ref[rs, ls].astype(F32)
            gated = hn * ng[:, ls] * _sigmoid(o_gate) * _silu(z_gate)
            y_ref[rs, ls] = gated.astype(BF16)

            loc = _dot((vext.astype(F32) * e_end).astype(BF16), k)
            m_new = jnp.maximum(b_last + m_prev, m_loc)
            st_sc[h] = jnp.exp(b_last + m_prev - m_new) * st + jnp.exp(m_loc - m_new) * loc
            m_sc[h] = jnp.broadcast_to(m_new, (1, LANE))


def _mlstm(main, vt, small, conv_w, conv_b, gbias, norm_g, layer, bsz, t_len):
    nsb = t_len // SEQ_BLK_C
    blk = lambda col: pl.BlockSpec((SEQ_BLK_C, D_C), lambda b, s: (b * nsb + s, col))
    full = lambda a: _layer_spec(a, layer, 2)
    return pl.pallas_call(
        _mlstm_kernel,
        grid=(bsz, nsb),
        in_specs=[blk(COL_CQ), blk(COL_CK),
                  pl.BlockSpec((D_C, SEQ_BLK_C), lambda b, s: (NSA_VT_W // D_C, b * nsb + s)),
                  blk(COL_CO), blk(COL_CZ),
                  pl.BlockSpec((SEQ_BLK_C, LANE), lambda b, s: (b * nsb + s, 0)),
                  full(conv_w), full(conv_b), full(gbias), full(norm_g)],
        out_specs=pl.BlockSpec((SEQ_BLK_C, D_C), lambda b, s: (b * nsb + s, 0)),
        out_shape=jax.ShapeDtypeStruct((bsz * t_len, D_C), BF16),
        scratch_shapes=[pltpu.VMEM((SEQ_BLK_C, D_C), BF16), pltpu.VMEM((SEQ_BLK_C, D_C), BF16),
                        pltpu.VMEM((2, SUBLANE, D_C), F32), pltpu.VMEM((NH_C, DVX, DH_C), F32),
                        pltpu.VMEM((NH_C, 1, LANE), F32)],
        compiler_params=_params("parallel", "arbitrary"),
        name="mlstm",
    )(main, main, vt, main, main, small, conv_w, conv_b, gbias, norm_g)


def _merge_kernel(x_ref, ya_ref, yc_ref, u_ref, v_ref, z_ref, g0_ref, g1_ref, g2_ref, p_ref,
                  sgg_ref, sgb_ref, sgw_ref, sgbt_ref,
                  wa_ref, wb_ref, wc_ref, wo_ref, wp_ref, wg_ref, lng_ref, lnb_ref, o_ref, *ob_ref):
    yb = _sgu_tile(u_ref, v_ref, z_ref, sgg_ref, sgb_ref, sgw_ref, sgbt_ref).astype(BF16)
    merged = (_sigmoid(g0_ref[...].astype(F32)) * _dot(ya_ref[...], wa_ref[...])
              + _sigmoid(g1_ref[...].astype(F32)) * _dot(yb, wb_ref[...])
              + _sigmoid(g2_ref[...].astype(F32)) * _dot(yc_ref[...], wc_ref[...]))
    r = ALPHA * x_ref[...] + _dot(merged.astype(BF16), wo_ref[...])
    r = r + _sigmoid(_dot(r.astype(BF16), wg_ref[...])) * _dot(p_ref[...].astype(BF16), wp_ref[...])
    mu = jnp.mean(r, axis=-1, keepdims=True)
    var = jnp.mean(jnp.square(r - mu), axis=-1, keepdims=True)
    y = (r - mu) * lax.rsqrt(var + LN_EPS) * lng_ref[...] + lnb_ref[...]
    o_ref[...] = y
    for ref in ob_ref:
        ref[...] = y.astype(BF16)


def _merge(x, ya, yc, main, p, sgg, sgb, sgw, sgbt, wa, wb, wc, wo, wp, wg, lng, lnb, layer, tm):
    n = x.shape[0]
    n_out = 1 if layer == DEPTH - 1 else 2
    row = lambda w: pl.BlockSpec((tm, w), lambda i: (i, 0))
    blk = lambda col: pl.BlockSpec((tm, D_B), lambda i: (i, col))
    gate = lambda j: pl.BlockSpec((tm, D_MODEL), lambda i: (i, j))
    full = lambda a: _layer_spec(a, layer, 1)
    return pl.pallas_call(
        _merge_kernel,
        grid=(n // tm,),
        in_specs=[row(D_MODEL), row(D_A), row(D_C), blk(COL_BU), blk(COL_BV), blk(COL_BZ),
                  gate(0), gate(1), gate(2), pl.BlockSpec((None, tm, PLE_DIM), lambda i: (layer, i, 0)),
                  full(sgg), full(sgb), full(sgw), full(sgbt),
                  full(wa), full(wb), full(wc), full(wo), full(wp), full(wg), full(lng), full(lnb)],
        out_specs=[row(D_MODEL)] * n_out,
        out_shape=[jax.ShapeDtypeStruct((n, D_MODEL), F32), jax.ShapeDtypeStruct((n, D_MODEL), BF16)][:n_out],
        compiler_params=_params("parallel"),
        name="merge",
    )(x, ya, yc, main, main, main, main, main, main, p, sgg, sgb, sgw, sgbt, wa, wb, wc, wo, wp, wg, lng, lnb)


def _regroup_kernel(wt_ref, main_ref, k_ref, v_ref, aux_ref):
    wt = wt_ref[...]
    cols = wt.shape[1]
    run = lambda first, last: wt[SEG[first][0]:SEG[last][1]]
    zeros = lambda n: jnp.zeros((n, cols), F32)
    main_ref[...] = jnp.concatenate([run('m_g', 'm_g'), run('a_q', 'a_q'), run('a_z', 'c_k'), run('c_o', 'c_z')],
                                    axis=0).astype(BF16)
    k_ref[...] = jnp.concatenate([run('a_ks', 'a_ks'), run('a_kw', 'a_kw')], axis=0).astype(BF16)
    v_rows = []
    for name in ('a_vs', 'a_vw'):
        for g in range(G_A):
            lo = SEG[name][0] + g * DK_A
            v_rows += [wt[lo:lo + DK_A], zeros(LANE - DK_A)]
    v_ref[...] = jnp.concatenate(v_rows + [run('c_v', 'c_v')], axis=0).astype(BF16)
    n_ag = SEG['a_g'][1] - SEG['a_g'][0]
    aux_ref[...] = jnp.concatenate([run('a_kc', 'a_vc'), run('a_g', 'a_g'), zeros(GI_LANE - n_ag), run('c_if', 'c_if'),
                                    zeros(LANE - GI_LANE - 2 * NH_C)], axis=0).astype(BF16)


def _regroup(w_in_t, cols):
    nl, n_in, d = w_in_t.shape
    blk = lambda rows: pl.BlockSpec((None, rows, cols), lambda l, c: (l, 0, c))
    return pl.pallas_call(
        _regroup_kernel,
        grid=(nl, d // cols),
        in_specs=[blk(n_in)],
        out_specs=[blk(MAIN_W), blk(K_W), blk(VT_W), blk(3 * LANE)],
        out_shape=[jax.ShapeDtypeStruct((nl, MAIN_W, d), BF16), jax.ShapeDtypeStruct((nl, K_W, d), BF16),
                   jax.ShapeDtypeStruct((nl, VT_W, d), BF16), jax.ShapeDtypeStruct((nl, 3 * LANE, d), BF16)],
        compiler_params=_params("parallel", "parallel"),
        name="regroup_w_in",
    )(w_in_t)


def _prepare(w_in, cmp_wv2, sg_ln_g, sg_ln_b, sg_b,
             ml_conv_b, ml_b_i, ml_b_f, ml_norm_g, w_br_a, w_br_b, w_br_c, w_out, ple_w, ple_gate, ln_g, ln_b):
    nl = w_in.shape[0]
    prm = {}
    prm['w_main'], prm['w_k'], prm['w_vt'], prm['w_aux'] = _regroup(jnp.swapaxes(w_in, 1, 2), REGROUP_COLS)
    prm['wv2t'] = jnp.swapaxes(cmp_wv2, 1, 2)
    prm['sg_ln_g'], prm['sg_ln_b'] = sg_ln_g[:, None, :], sg_ln_b[:, None, :]
    prm['sg_b_t'] = jnp.swapaxes(sg_b, 1, 2)
    gbias = jnp.zeros((nl, 1, LANE), F32).at[:, 0, GI_LANE:GI_LANE + NH_C].set(ml_b_i)
    prm['gbias'] = gbias.at[:, 0, GF_LANE:GF_LANE + NH_C].set(ml_b_f)
    prm['conv_b'], prm['norm_g'] = ml_conv_b[:, None, :], ml_norm_g[:, None, :]
    for name, a in (('wa', w_br_a), ('wb', w_br_b), ('wc', w_br_c), ('wo', w_out), ('wp', ple_w), ('wg', ple_gate)):
        prm[name] = a.astype(BF16)
    prm['ln_g'], prm['ln_b'] = ln_g[:, None, :], ln_b[:, None, :]
    return prm


def _layer(i, x, xin, p, prm, cmp, sg_w, ml_conv_w, bsz, t_len):
    main = _matmul(xin, prm['w_main'], i, PROJ_TM, PROJ_TN, "proj_main")
    kk, vt, kc, vc, small = _tail_matmul(xin, prm['w_k'], prm['w_vt'], prm['w_aux'], i, TAIL_TM)

    pos_k, pos_v, wk1, wk2, wv1 = cmp
    kcmp, vcmp_t = _compress(kc, vc, pos_k, pos_v, wk1, wk2, wv1, prm['wv2t'], i, bsz, t_len)
    ya = _nsa(main, kk, vt, small, kcmp, vcmp_t, bsz, t_len)
    yc = _mlstm(main, vt, small, ml_conv_w, prm['conv_b'], prm['gbias'], prm['norm_g'], i, bsz, t_len)
    return _merge(x, ya, yc, main, p, prm['sg_ln_g'], prm['sg_ln_b'], sg_w, prm['sg_b_t'],
                  prm['wa'], prm['wb'], prm['wc'], prm['wo'], prm['wp'], prm['wg'], prm['ln_g'], prm['ln_b'], i, MERGE_TM)


def kernel(x, p, w_in, cmp_pos_k, cmp_pos_v, cmp_wk1, cmp_wk2, cmp_wv1, cmp_wv2, sg_ln_g, sg_ln_b, sg_w, sg_b,
           ml_conv_w, ml_conv_b, ml_b_i, ml_b_f, ml_norm_g, w_br_a, w_br_b, w_br_c, w_out, ple_w, ple_gate,
           ln_g, ln_b):
    bsz, t_len, d = x.shape
    assert d == D_MODEL and t_len == N_CMP_PAD * STRIDE_CMP and t_len // L_SEL == N_BLK
    assert t_len % SEQ_BLK_C == 0 and t_len % CK == 0 and w_in.shape[0] == DEPTH
    prm = _prepare(w_in, cmp_wv2, sg_ln_g, sg_ln_b, sg_b,
                   ml_conv_b, ml_b_i, ml_b_f, ml_norm_g, w_br_a, w_br_b, w_br_c, w_out, ple_w, ple_gate, ln_g, ln_b)
    xf = x.reshape(bsz * t_len, d)
    xin = xf
    pf = p.reshape(DEPTH, bsz * t_len, PLE_DIM)
    for i in range(DEPTH):
        outs = _layer(i, xf, xin, pf, prm, (cmp_pos_k, cmp_pos_v, cmp_wk1, cmp_wk2, cmp_wv1), sg_w, ml_conv_w,
                      bsz, t_len)
        xf, xin = outs[0], outs[-1]
    return xf.reshape(bsz, t_len, d)
```

```python
import jax
import jax.numpy as jnp
from jax import lax
from jax.experimental import pallas as pl
from jax.experimental.pallas import tpu as pltpu

F32 = jnp.float32
BF16 = jnp.bfloat16

D_MODEL = 1024
PLE_DIM = 256
D_A = 512
NH_A = 8
G_A = 2
HPG_A = 4
DK_A = 64
L_CMP = 32
STRIDE_CMP = 16
CMP_HID = 128
L_SEL = 64
N_SEL = 8
N_BLK = 32
WINDOW = 256
QB = 256
CK = 256
N_CMP_PAD = 128
BIG = 1e9
D_B = 512
CHUNK_B = 128
G_B = 4
D_C = 512
NH_C = 4
DH_C = 128
CONV_W = 4
CHUNK_C = 128
SEQ_BLK_C = 1024
LN_EPS = 1e-5
DEPTH = 2
ALPHA = (2.0 * DEPTH) ** 0.25

LANE = 128
SUBLANE = 8
_SEG_SIZES = (('a_q', D_A), ('a_kc', G_A * DK_A), ('a_vc', G_A * DK_A), ('a_ks', G_A * DK_A), ('a_vs', G_A * DK_A),
              ('a_kw', G_A * DK_A), ('a_vw', G_A * DK_A), ('a_g', 3 * NH_A), ('a_z', D_A),
              ('b_u', D_B), ('b_v', D_B), ('b_z', D_B),
              ('c_q', D_C), ('c_k', D_C), ('c_v', D_C), ('c_if', 2 * NH_C), ('c_o', D_C), ('c_z', D_C),
              ('m_g', 3 * D_MODEL))
SEG = {}
for _name, _size in _SEG_SIZES:
    _lo = sum(s for _, s in _SEG_SIZES[:len(SEG)])
    SEG[_name] = (_lo, _lo + _size)
MAIN_W = 7680
COL_AQ, COL_AZ, COL_BU, COL_BV, COL_BZ, COL_CQ, COL_CK, COL_CO, COL_CZ = 6, 7, 8, 9, 10, 11, 12, 13, 14
K_W = 256
NSA_VT_W = 512
VT_W = NSA_VT_W + 512
DVX = 128 + 16
GI_LANE = 32
GF_LANE = 36
V7X_VMEM_BYTES = 64 * 1024 * 1024
VMEM_LIMIT = V7X_VMEM_BYTES * 13 // 16
PROJ_TM, PROJ_TN = 2048, 1536
TAIL_TM = 2048
MERGE_TM = 512
REGROUP_COLS = 256
NEG = -1e30
LOG2E = 1.4426950408889634


def _params(*sem):
    return pltpu.CompilerParams(dimension_semantics=sem, vmem_limit_bytes=VMEM_LIMIT)


def _nt_dot(a, b):
    return lax.dot_general(a, b, (((1,), (1,)), ((), ())), preferred_element_type=F32)


def _dot(a, b):
    return jnp.dot(a, b, preferred_element_type=F32)


def _split3(a):
    a1 = a.astype(BF16)
    r1 = a - a1.astype(F32)
    a2 = r1.astype(BF16)
    a3 = (r1 - a2.astype(F32)).astype(BF16)
    return a1, a2, a3


def _sigmoid(x):
    return 0.5 * jnp.tanh(0.5 * x) + 0.5


def _silu(x):
    return x * _sigmoid(x)


def _log_sigmoid(x):
    return jnp.minimum(x, 0.0) - jnp.log1p(jnp.exp(-jnp.abs(x)))


def _layer_spec(a, layer, grid_rank):
    zeros = (0,) * (a.ndim - 1)
    if grid_rank == 1:
        return pl.BlockSpec((None,) + a.shape[1:], lambda i: (layer,) + zeros)
    return pl.BlockSpec((None,) + a.shape[1:], lambda i, j: (layer,) + zeros)


def _mm_kernel(x_ref, wt_ref, o_ref):
    o_ref[...] = _nt_dot(x_ref[...].astype(BF16), wt_ref[...]).astype(o_ref.dtype)


def _matmul(x, wt, layer, tm, tn, name):
    m, k = x.shape
    n = wt.shape[1]
    return pl.pallas_call(
        _mm_kernel,
        grid=(m // tm, n // tn),
        in_specs=[pl.BlockSpec((tm, k), lambda i, j: (i, 0)),
                  pl.BlockSpec((None, tn, k), lambda i, j: (layer, j, 0))],
        out_specs=pl.BlockSpec((tm, tn), lambda i, j: (i, j)),
        out_shape=jax.ShapeDtypeStruct((m, n), BF16),
        compiler_params=_params("parallel", "arbitrary"),
        name=name,
    )(x, wt)


def _tail_kernel(x_ref, wkt_ref, wvt_ref, wauxt_ref, ok_ref, ovt_ref, okc_ref, ovc_ref, osm_ref):
    x = x_ref[...].astype(BF16)
    ok_ref[...] = _nt_dot(x, wkt_ref[...]).astype(BF16)
    ovt_ref[...] = _nt_dot(wvt_ref[...], x).astype(BF16)
    aux = _nt_dot(x, wauxt_ref[...])
    okc_ref[...] = aux[:, 0:LANE]
    ovc_ref[...] = aux[:, LANE:2 * LANE]
    osm_ref[...] = aux[:, 2 * LANE:3 * LANE]


def _tail_matmul(x, wk, wvt, waux, layer, tm):
    m, k = x.shape
    tok = lambda w: pl.BlockSpec((tm, w), lambda i: (i, 0))
    f32_out = jax.ShapeDtypeStruct((m, LANE), F32)
    return pl.pallas_call(
        _tail_kernel,
        grid=(m // tm,),
        in_specs=[tok(k), _layer_spec(wk, layer, 1), _layer_spec(wvt, layer, 1), _layer_spec(waux, layer, 1)],
        out_specs=[tok(K_W), pl.BlockSpec((VT_W, tm), lambda i: (0, i)), tok(LANE), tok(LANE), tok(LANE)],
        out_shape=[jax.ShapeDtypeStruct((m, K_W), BF16), jax.ShapeDtypeStruct((VT_W, m), BF16),
                   f32_out, f32_out, f32_out],
        compiler_params=_params("parallel"),
        name="proj_tail",
    )(x, wk, wvt, waux)


def _compress_kernel(zk_ref, zv_ref, pk_ref, pv_ref, wk1_ref, wk2_ref, wv1_ref, wv2t_ref, ko_ref, vot_ref):
    n = zk_ref.shape[0] // STRIDE_CMP

    def hidden(z_ref, p_ref, w1_ref):
        r0 = [jnp.zeros((n, CMP_HID), F32) for _ in range(G_A)]
        r1 = [jnp.zeros((n, CMP_HID), F32) for _ in range(G_A)]
        for l in range(STRIDE_CMP):
            zl = z_ref[pl.ds(l, n, stride=STRIDE_CMP), :]
            w_a = w1_ref[l * DK_A:(l + 1) * DK_A, :].astype(BF16)
            w_b = w1_ref[(STRIDE_CMP + l) * DK_A:(STRIDE_CMP + l + 1) * DK_A, :].astype(BF16)
            for g in range(G_A):
                zg = zl[:, g * DK_A:(g + 1) * DK_A]
                r0[g] = r0[g] + _dot((zg + p_ref[l:l + 1, :]).astype(BF16), w_a)
                r1[g] = r1[g] + _dot((zg + p_ref[STRIDE_CMP + l:STRIDE_CMP + l + 1, :]).astype(BF16), w_b)
        return [jax.nn.gelu(r0[g] + pltpu.roll(r1[g], n - 1, 0)).astype(BF16) for g in range(G_A)]

    hk = hidden(zk_ref, pk_ref, wk1_ref)
    wk2 = wk2_ref[...].astype(BF16)
    for g in range(G_A):
        ko_ref[0, g] = _dot(hk[g], wk2).astype(BF16)
    hv = hidden(zv_ref, pv_ref, wv1_ref)
    wv2t = wv2t_ref[...].astype(BF16)
    vot_ref[0] = jnp.concatenate([_nt_dot(wv2t, hv[g]) for g in range(G_A)], axis=0).astype(BF16)


def _compress(kc, vc, pk, pv, wk1, wk2, wv1, wv2t, layer, bsz, t_len):
    n = t_len // STRIDE_CMP
    zspec = pl.BlockSpec((t_len, LANE), lambda i: (i, 0))
    full = lambda a: _layer_spec(a, layer, 1)
    return pl.pallas_call(
        _compress_kernel,
        grid=(bsz,),
        in_specs=[zspec, zspec, full(pk), full(pv), full(wk1), full(wk2), full(wv1), full(wv2t)],
        out_specs=[pl.BlockSpec((1, G_A, n, DK_A), lambda i: (i, 0, 0, 0)),
                   pl.BlockSpec((1, G_A * DK_A, n), lambda i: (i, 0, 0))],
        out_shape=[jax.ShapeDtypeStruct((bsz, G_A, n, DK_A), BF16), jax.ShapeDtypeStruct((bsz, G_A * DK_A, n), BF16)],
        compiler_params=_params("parallel"),
        name="nsa_compress",
    )(kc, vc, pk, pv, wk1, wk2, wv1, wv2t)


def _nsa_kernel(q_ref, z_ref, g_ref, kc_ref, vct_ref, k_ref, vt_ref, o_ref,
                s_sc, acc_sc, ow_sc, bw_sc, sel_sc, idx_sc):
    qb = pl.program_id(1)
    t0 = qb * QB
    cols = HPG_A * QB
    n_qtile = QB // LANE
    n_wchunk = (WINDOW + QB) // LANE
    n_wslot = n_wchunk + 1
    k_i = lax.broadcasted_iota(jnp.int32, (LANE, 1), 0)
    t_i = lax.broadcasted_iota(jnp.int32, (1, QB), 1)
    tk = (t_i - k_i).astype(F32)
    ones_row = jnp.where(lax.broadcasted_iota(jnp.int32, (LANE, CK), 0) == DK_A, 1.0, 0.0).astype(BF16)
    slopes = [[LOG2E * 2.0 ** -(g * HPG_A + h + 1) for h in range(HPG_A)] for g in range(G_A)]
    hs = [slice(h * QB, (h + 1) * QB) for h in range(HPG_A)]

    def biased(s, bias_of_head):
        return jnp.concatenate([s[:, hs[h]] + bias_of_head(h) for h in range(HPG_A)], axis=1)

    @pl.when((pl.program_id(0) == 0) & (qb == 0))
    def _():
        for g in range(G_A):
            for h in range(HPG_A):
                for j in range(n_wchunk):
                    dist = tk + float(WINDOW - j * LANE)
                    ok = (dist >= 0.0) & (dist < WINDOW)
                    bw_sc[g, h * n_wslot + j] = jnp.where(ok, -slopes[g][h] * dist, NEG)
                bw_sc[g, h * n_wslot + n_wchunk] = jnp.full((LANE, QB), NEG, F32)

    gsig = _sigmoid(g_ref[...])
    gates_t = jnp.concatenate([jnp.transpose(gsig[r * LANE:(r + 1) * LANE]) for r in range(n_qtile)], axis=1)

    jb = lax.broadcasted_iota(jnp.int32, (N_BLK, QB), 0)
    cur = (t0 + lax.broadcasted_iota(jnp.int32, (N_BLK, QB), 1)) >> 6
    forced = (jb == 0) | (jb == cur) | (jb == cur - 1)
    allowed = jb <= cur
    ov_j = lax.broadcasted_iota(jnp.int32, (N_BLK, N_CMP_PAD), 0) * L_SEL
    ov_n = lax.broadcasted_iota(jnp.int32, (N_BLK, N_CMP_PAD), 1) * STRIDE_CMP
    ov_t = jnp.where((ov_n < ov_j + L_SEL) & (ov_n + L_CMP > ov_j), 1.0, 0.0).astype(BF16)
    jb8 = lax.broadcasted_iota(jnp.int32, (SUBLANE, QB), 0)

    d_cmp = (t0 - (L_CMP - 1)).astype(F32) + (t_i - STRIDE_CMP * k_i).astype(F32)
    valid_cmp = (d_cmp >= 0.0) & (k_i < N_CMP_PAD - 1)

    gls = [slice(g * DK_A, (g + 1) * DK_A) for g in range(G_A)]
    q4s, o_cmps = [], []
    for g in range(G_A):
        q4 = jnp.concatenate(
            [q_ref[:, (g * HPG_A + h) * DK_A:(g * HPG_A + h + 1) * DK_A] for h in range(HPG_A)], axis=0)
        q4 = (q4.astype(F32) * (LOG2E * DK_A ** -0.5)).astype(BF16)
        q4s.append(q4)

        s = biased(_nt_dot(kc_ref[0, g], q4),
                   lambda h: jnp.where(valid_cmp, -slopes[g][h] * d_cmp, -jnp.inf))
        mx = jnp.max(s, axis=0, keepdims=True)
        mx = jnp.where(mx > -jnp.inf, mx, 0.0)
        e = jnp.exp2(s - mx)
        p = e * (1.0 / jnp.maximum(jnp.sum(e, axis=0, keepdims=True), 1e-30))
        o_cmps.append(_dot(vct_ref[0][gls[g], :], p.astype(BF16)))

        psum = p[:, hs[0]] + p[:, hs[1]] + p[:, hs[2]] + p[:, hs[3]]
        p1, p2, p3 = _split3(psum)
        imp = _dot(ov_t, p1) + _dot(ov_t, p2) + _dot(ov_t, p3)
        score = jnp.where(allowed, jnp.where(forced, BIG, imp), -BIG)
        tiles = [score[r * SUBLANE:(r + 1) * SUBLANE] for r in range(N_BLK // SUBLANE)]
        cnt = [jnp.zeros((SUBLANE, QB), F32) for _ in tiles]
        for j in range(N_BLK):
            sj = score[j:j + 1, :]
            for r, tile in enumerate(tiles):
                lo = r * SUBLANE
                if j >= lo + SUBLANE:
                    beats = sj > tile
                elif j < lo:
                    beats = sj >= tile
                else:
                    beats = (sj > tile) | ((sj >= tile) & (jb8 > j - lo))
                cnt[r] = cnt[r] + jnp.where(beats, 1.0, 0.0)
        sel_sc[g] = jnp.where((jnp.concatenate(cnt, axis=0) < N_SEL) & allowed, 1.0, 0.0)

        for r in range(n_qtile):
            q4_r = jnp.concatenate([q4[h * QB + r * LANE:h * QB + (r + 1) * LANE] for h in range(HPG_A)], axis=0)
            s_w, v_w = [], []
            for j in range(r, r + WINDOW // LANE + 1):
                c = qb * n_qtile - WINDOW // LANE + j
                k0 = pl.multiple_of(jnp.maximum(c, 0) * LANE, LANE)
                slot = jnp.where(c >= 0, j, n_wchunk)
                sj = _nt_dot(k_ref[pl.ds(k0, LANE), LANE + g * DK_A:LANE + (g + 1) * DK_A], q4_r)
                s_w.append(jnp.concatenate(
                    [sj[:, h * LANE:(h + 1) * LANE] + bw_sc[g, h * n_wslot + slot, :, r * LANE:(r + 1) * LANE]
                     for h in range(HPG_A)], axis=1))
                v_w.append(vt_ref[G_A * LANE + g * LANE:G_A * LANE + (g + 1) * LANE, pl.ds(k0, LANE)]
                           + ones_row[:, :LANE])
            mx = s_w[0]
            for sj in s_w[1:]:
                mx = jnp.maximum(mx, sj)
            mx = jnp.max(mx, axis=0, keepdims=True)
            o_win = _dot(v_w[0], jnp.exp2((s_w[0] - mx).astype(BF16)))
            for sj, vj in zip(s_w[1:], v_w[1:]):
                o_win = o_win + _dot(vj, jnp.exp2((sj - mx).astype(BF16)))
            ow_sc[g, r] = o_win

    chosen_any = jnp.maximum(sel_sc[0], sel_sc[1])
    blk_per_chunk = CK // L_SEL
    n_all = s_sc.shape[1]
    per_chunk = jnp.concatenate(
        [jnp.max(chosen_any[c * blk_per_chunk:(c + 1) * blk_per_chunk, :], axis=0, keepdims=True)
         for c in range(n_all)], axis=0)
    bit = jnp.exp2(lax.broadcasted_iota(jnp.int32, (n_all, 1), 0).astype(F32))
    mask = jnp.sum(jnp.max(per_chunk, axis=1, keepdims=True) * bit).astype(jnp.int32)
    n_causal = ((qb + 1) * QB - 1) // CK + 1
    n_chunk = jnp.int32(0)
    for c in range(n_all):
        need = (((mask >> c) & 1) == 1) & (c < n_causal)
        idx_sc[n_chunk] = c
        n_chunk = n_chunk + jnp.where(need, 1, 0)

    def score_chunk(i, m_runs):
        c = idx_sc[i]
        k0 = pl.multiple_of(c * CK, CK)
        out = []
        for g in range(G_A):
            sc = _nt_dot(k_ref[pl.ds(k0, CK), gls[g]], q4s[g])
            halves = []
            for u in range(CK // LANE):
                dist = tk + (t0 - k0 - u * LANE).astype(F32)
                blk0 = (CK // L_SEL) * c + (LANE // L_SEL) * u
                chosen = jnp.concatenate(
                    [jnp.broadcast_to(sel_sc[g, pl.ds(blk0 + b, 1), :], (L_SEL, QB))
                     for b in range(LANE // L_SEL)], axis=0)
                ok = (chosen > 0.5) & (dist >= 0.0)
                far = jnp.where(ok, dist, -NEG)
                halves.append(biased(sc[u * LANE:(u + 1) * LANE], lambda h: far * -slopes[g][h]))
            sc = jnp.concatenate(halves, axis=0)
            s_sc[g, i] = sc
            out.append(jnp.maximum(m_runs[g], jnp.max(sc.reshape(CK // SUBLANE, SUBLANE, cols), axis=0)))
        return tuple(out)

    m_runs = lax.fori_loop(0, n_chunk, score_chunk, (jnp.full((SUBLANE, cols), NEG, F32),) * G_A)
    m_rows = [jnp.max(m_runs[g], axis=0, keepdims=True) for g in range(G_A)]
    acc_sc[...] = jnp.zeros(acc_sc.shape, F32)

    def value_chunk(i, carry):
        k0 = pl.multiple_of(idx_sc[i] * CK, CK)
        for g in range(G_A):
            pc = jnp.exp2((s_sc[g, i] - m_rows[g]).astype(BF16))
            acc_sc[g] += _dot(vt_ref[g * LANE:(g + 1) * LANE, pl.ds(k0, CK)] + ones_row, pc)
        return carry

    lax.fori_loop(0, n_chunk, value_chunk, 0)

    pieces = []
    for g in range(G_A):
        o_sel = acc_sc[g]
        for h in range(HPG_A):
            col = g * HPG_A + h
            os_h = o_sel[:, hs[h]]
            ow_h = jnp.concatenate([ow_sc[g, r, :, h * LANE:(h + 1) * LANE] for r in range(n_qtile)], axis=1)
            w_sel = gates_t[NH_A + col:NH_A + col + 1, :] / jnp.maximum(os_h[DK_A:DK_A + 1, :], 1e-30)
            w_win = gates_t[2 * NH_A + col:2 * NH_A + col + 1, :] / jnp.maximum(ow_h[DK_A:DK_A + 1, :], 1e-30)
            pieces.append(gates_t[col:col + 1, :] * o_cmps[g][:, hs[h]] + w_sel * os_h[:DK_A] + w_win * ow_h[:DK_A])

    out_t = jnp.concatenate(pieces, axis=0).astype(BF16)
    out = jnp.concatenate(
        [jnp.concatenate([jnp.transpose(out_t[i * LANE:(i + 1) * LANE, r * LANE:(r + 1) * LANE])
                          for i in range(D_A // LANE)], axis=1) for r in range(n_qtile)], axis=0)
    o_ref[...] = (out.astype(F32) * _silu(z_ref[...].astype(F32))).astype(BF16)


def _nsa(main, kk, vt, small, kcmp, vcmp_t, bsz, t_len):
    nqb = t_len // QB
    cols = HPG_A * QB
    n_wslot = (WINDOW + QB) // LANE + 1
    return pl.pallas_call(
        _nsa_kernel,
        grid=(bsz, nqb),
        in_specs=[pl.BlockSpec((QB, D_A), lambda b, i: (b * nqb + i, COL_AQ)),
                  pl.BlockSpec((QB, D_A), lambda b, i: (b * nqb + i, COL_AZ)),
                  pl.BlockSpec((QB, LANE), lambda b, i: (b * nqb + i, 0)),
                  pl.BlockSpec((1, G_A, N_CMP_PAD, DK_A), lambda b, i: (b, 0, 0, 0)),
                  pl.BlockSpec((1, G_A * DK_A, N_CMP_PAD), lambda b, i: (b, 0, 0)),
                  pl.BlockSpec((t_len, K_W), lambda b, i: (b, 0)),
                  pl.BlockSpec((NSA_VT_W, t_len), lambda b, i: (0, b))],
        out_specs=pl.BlockSpec((QB, D_A), lambda b, i: (b * nqb + i, 0)),
        out_shape=jax.ShapeDtypeStruct((bsz * t_len, D_A), BF16),
        scratch_shapes=[pltpu.VMEM((G_A, t_len // CK, CK, cols), F32), pltpu.VMEM((G_A, LANE, cols), F32),
                        pltpu.VMEM((G_A, QB // LANE, LANE, HPG_A * LANE), F32),
                        pltpu.VMEM((G_A, HPG_A * n_wslot, LANE, QB), F32), pltpu.VMEM((G_A, N_BLK, QB), F32),
                        pltpu.SMEM((t_len // CK + 1,), jnp.int32)],
        compiler_params=_params("arbitrary", "arbitrary"),
        name="nsa_attention",
    )(main, main, small, kcmp, vcmp_t, kk, vt)


def _sgu_tile(u_ref, v_ref, z_ref, lng_ref, lnb_ref, w_ref, b_ref):
    u = jax.nn.gelu(u_ref[...].astype(F32))
    v = jax.nn.gelu(v_ref[...].astype(F32))
    mu = jnp.mean(v, axis=-1, keepdims=True)
    var = jnp.mean(jnp.square(v - mu), axis=-1, keepdims=True)
    vn = ((v - mu) * lax.rsqrt(var + LN_EPS) * lng_ref[...] + lnb_ref[...]).astype(BF16)
    gate = u * _silu(z_ref[...].astype(F32))
    ti = lax.broadcasted_iota(jnp.int32, (CHUNK_B, CHUNK_B), 0)
    si = lax.broadcasted_iota(jnp.int32, (CHUNK_B, CHUNK_B), 1)
    ws = [jnp.where(si <= ti, w_ref[g], 0.0).astype(BF16) for g in range(G_B)]
    rows = []
    for c in range(u_ref.shape[0] // CHUNK_B):
        rs = slice(c * CHUNK_B, (c + 1) * CHUNK_B)
        rows.append(jnp.concatenate(
            [gate[rs, g * LANE:(g + 1) * LANE] * (_dot(ws[g], vn[rs, g * LANE:(g + 1) * LANE]) + b_ref[:, g:g + 1])
             for g in range(G_B)], axis=1))
    return jnp.concatenate(rows, axis=0)


def _mlstm_kernel(q_ref, k_ref, vt_ref, o_ref, z_ref, g_ref, cw_ref, cb_ref, gb_ref, ng_ref, y_ref,
                  qc_sc, kc_sc, halo_sc, st_sc, m_sc):
    sb = pl.program_id(1)
    blk = q_ref.shape[0]

    @pl.when(sb == 0)
    def _():
        halo_sc[...] = jnp.zeros(halo_sc.shape, F32)
        st_sc[...] = jnp.zeros(st_sc.shape, F32)
        m_sc[...] = jnp.zeros(m_sc.shape, F32)

    r8 = lax.broadcasted_iota(jnp.int32, (SUBLANE, 1), 0)

    def conv_silu(x_ref, which, dst_ref, scale):
        x = x_ref[...].astype(F32)
        prev = halo_sc[which]
        w = cw_ref[:, which * D_C:(which + 1) * D_C]
        acc = x * w[CONV_W - 1:CONV_W, :] + cb_ref[:, which * D_C:(which + 1) * D_C]
        for j in range(1, CONV_W):
            rolled = pltpu.roll(x, j, 0)
            head = jnp.where(r8 < j, pltpu.roll(prev, j, 0), rolled[0:SUBLANE])
            shifted = jnp.concatenate([head, rolled[SUBLANE:]], axis=0)
            acc = acc + shifted * w[CONV_W - 1 - j:CONV_W - j, :]
        halo_sc[which] = x[blk - SUBLANE:blk]
        dst_ref[...] = (_silu(acc) * scale).astype(BF16)

    conv_silu(q_ref, 0, qc_sc, 1.0)
    conv_silu(k_ref, 1, kc_sc, DH_C ** -0.5)

    si = lax.broadcasted_iota(jnp.int32, (CHUNK_C, CHUNK_C), 0)
    ti = lax.broadcasted_iota(jnp.int32, (CHUNK_C, CHUNK_C), 1)
    causal = si <= ti
    tri_u = jnp.where(causal, 1.0, 0.0).astype(BF16)
    ones_rows = jnp.where(lax.broadcasted_iota(jnp.int32, (DVX - DH_C, CHUNK_C), 0) == 0, 1.0, 0.0).astype(BF16)
    ng = ng_ref[...]
    gbias = gb_ref[...]

    for c in range(blk // CHUNK_C):
        rs = slice(c * CHUNK_C, (c + 1) * CHUNK_C)
        gate_rows = jnp.transpose(g_ref[rs, :] + gbias)[GI_LANE:GI_LANE + 2 * NH_C]
        f1, f2, f3 = _split3(_log_sigmoid(gate_rows))
        bc_rows = _dot(f1, tri_u) + _dot(f2, tri_u) + _dot(f3, tri_u)
        g_rows = gate_rows[0:NH_C] - bc_rows[NH_C:2 * NH_C]
        g_cols = jnp.transpose(jnp.concatenate([g_rows, jnp.zeros((CHUNK_C - NH_C, CHUNK_C), F32)], axis=0))
        for h in range(NH_C):
            ls = slice(h * DH_C, (h + 1) * DH_C)
            q = qc_sc[rs, ls]
            k = kc_sc[rs, ls]
            vext = jnp.concatenate([vt_ref[ls, rs], ones_rows], axis=0)
            g_col = g_cols[:, h:h + 1]
            g_row = g_rows[h:h + 1, :]
            bc_row = bc_rows[NH_C + h:NH_C + h + 1, :]
            b_last = bc_row[:, CHUNK_C - 1:CHUNK_C]
            m_prev = m_sc[h][:, 0:1]
            st = st_sc[h]
            m_loc = b_last + jnp.max(g_row, axis=1, keepdims=True)
            e_end = jnp.exp(b_last + g_row - m_loc)

            gmat = jnp.where(causal, g_col, -jnp.inf)
            mg = jnp.maximum(m_prev, jnp.max(gmat, axis=0, keepdims=True))
            s_t = _nt_dot(k, q) * jnp.exp(gmat - mg)
            e_int = jnp.exp(m_prev - mg)
            both = _dot(vext, s_t.astype(BF16)) + _nt_dot(st.astype(BF16), q) * e_int
            den = both[DH_C:DH_C + 1, :]
            hval = both[:DH_C] * (1.0 / jnp.maximum(jnp.abs(den), jnp.exp(-(bc_row + mg))))
            mu = jnp.mean(hval, axis=0, keepdims=True)
            var = jnp.mean(jnp.square(hval - mu), axis=0, keepdims=True)
            hn = jnp.transpose(((hval - mu) * lax.rsqrt(var + LN_EPS)).astype(BF16)).astype(F32)
            o_gate = o_ref[rs, ls].astype(F32)
            z_gate = z_ref[rs, ls].astype(F32)
            gated = hn * ng[:, ls] * _sigmoid(o_gate) * _silu(z_gate)
            y_ref[rs, ls] = gated.astype(BF16)

            loc = _dot((vext.astype(F32) * e_end).astype(BF16), k)
            m_new = jnp.maximum(b_last + m_prev, m_loc)
            st_sc[h] = jnp.exp(b_last + m_prev - m_new) * st + jnp.exp(m_loc - m_new) * loc
            m_sc[h] = jnp.broadcast_to(m_new, (1, LANE))


def _mlstm(main, vt, small, conv_w, conv_b, gbias, norm_g, layer, bsz, t_len):
    nsb = t_len // SEQ_BLK_C
    blk = lambda col: pl.BlockSpec((SEQ_BLK_C, D_C), lambda b, s: (b * nsb + s, col))
    full = lambda a: _layer_spec(a, layer, 2)
    return pl.pallas_call(
        _mlstm_kernel,
        grid=(bsz, nsb),
        in_specs=[blk(COL_CQ), blk(COL_CK),
                  pl.BlockSpec((D_C, SEQ_BLK_C), lambda b, s: (NSA_VT_W // D_C, b * nsb + s)),
                  blk(COL_CO), blk(COL_CZ),
                  pl.BlockSpec((SEQ_BLK_C, LANE), lambda b, s: (b * nsb + s, 0)),
                  full(conv_w), full(conv_b), full(gbias), full(norm_g)],
        out_specs=pl.BlockSpec((SEQ_BLK_C, D_C), lambda b, s: (b * nsb + s, 0)),
        out_shape=jax.ShapeDtypeStruct((bsz * t_len, D_C), BF16),
        scratch_shapes=[pltpu.VMEM((SEQ_BLK_C, D_C), BF16), pltpu.VMEM((SEQ_BLK_C, D_C), BF16),
                        pltpu.VMEM((2, SUBLANE, D_C), F32), pltpu.VMEM((NH_C, DVX, DH_C), F32),
                        pltpu.VMEM((NH_C, 1, LANE), F32)],
        compiler_params=_params("parallel", "arbitrary"),
        name="mlstm",
    )(main, main, vt, main, main, small, conv_w, conv_b, gbias, norm_g)


def _merge_kernel(x_ref, ya_ref, yc_ref, u_ref, v_ref, z_ref, g0_ref, g1_ref, g2_ref, p_ref,
                  sgg_ref, sgb_ref, sgw_ref, sgbt_ref,
                  wa_ref, wb_ref, wc_ref, wo_ref, wp_ref, wg_ref, lng_ref, lnb_ref, o_ref, *ob_ref):
    yb = _sgu_tile(u_ref, v_ref, z_ref, sgg_ref, sgb_ref, sgw_ref, sgbt_ref).astype(BF16)
    merged = (_sigmoid(g0_ref[...].astype(F32)) * _dot(ya_ref[...], wa_ref[...])
              + _sigmoid(g1_ref[...].astype(F32)) * _dot(yb, wb_ref[...])
              + _sigmoid(g2_ref[...].astype(F32)) * _dot(yc_ref[...], wc_ref[...]))
    r = ALPHA * x_ref[...] + _dot(merged.astype(BF16), wo_ref[...])
    r = r + _sigmoid(_dot(r.astype(BF16), wg_ref[...])) * _dot(p_ref[...].astype(BF16), wp_ref[...])
    mu = jnp.mean(r, axis=-1, keepdims=True)
    var = jnp.mean(jnp.square(r - mu), axis=-1, keepdims=True)
    y = (r - mu) * lax.rsqrt(var + LN_EPS) * lng_ref[...] + lnb_ref[...]
    o_ref[...] = y
    for ref in ob_ref:
        ref[...] = y.astype(BF16)


def _merge(x, ya, yc, main, p, sgg, sgb, sgw, sgbt, wa, wb, wc, wo, wp, wg, lng, lnb, layer, tm):
    n = x.shape[0]
    n_out = 1 if layer == DEPTH - 1 else 2
    row = lambda w: pl.BlockSpec((tm, w), lambda i: (i, 0))
    blk = lambda col: pl.BlockSpec((tm, D_B), lambda i: (i, col))
    gate = lambda j: pl.BlockSpec((tm, D_MODEL), lambda i: (i, j))
    full = lambda a: _layer_spec(a, layer, 1)
    return pl.pallas_call(
        _merge_kernel,
        grid=(n // tm,),
        in_specs=[row(D_MODEL), row(D_A), row(D_C), blk(COL_BU), blk(COL_BV), blk(COL_BZ),
                  gate(0), gate(1), gate(2), pl.BlockSpec((None, tm, PLE_DIM), lambda i: (layer, i, 0)),
                  full(sgg), full(sgb), full(sgw), full(sgbt),
                  full(wa), full(wb), full(wc), full(wo), full(wp), full(wg), full(lng), full(lnb)],
        out_specs=[row(D_MODEL)] * n_out,
        out_shape=[jax.ShapeDtypeStruct((n, D_MODEL), F32), jax.ShapeDtypeStruct((n, D_MODEL), BF16)][:n_out],
        compiler_params=_params("parallel"),
        name="merge",
    )(x, ya, yc, main, main, main, main, main, main, p, sgg, sgb, sgw, sgbt, wa, wb, wc, wo, wp, wg, lng, lnb)


def _regroup_kernel(wt_ref, main_ref, k_ref, v_ref, aux_ref):
    wt = wt_ref[...]
    cols = wt.shape[1]
    run = lambda first, last: wt[SEG[first][0]:SEG[last][1]]
    zeros = lambda n: jnp.zeros((n, cols), F32)
    main_ref[...] = jnp.concatenate([run('m_g', 'm_g'), run('a_q', 'a_q'), run('a_z', 'c_k'), run('c_o', 'c_z')],
                                    axis=0).astype(BF16)
    k_ref[...] = jnp.concatenate([run('a_ks', 'a_ks'), run('a_kw', 'a_kw')], axis=0).astype(BF16)
    v_rows = []
    for name in ('a_vs', 'a_vw'):
        for g in range(G_A):
            lo = SEG[name][0] + g * DK_A
            v_rows += [wt[lo:lo + DK_A], zeros(LANE - DK_A)]
    v_ref[...] = jnp.concatenate(v_rows + [run('c_v', 'c_v')], axis=0).astype(BF16)
    n_ag = SEG['a_g'][1] - SEG['a_g'][0]
    aux_ref[...] = jnp.concatenate([run('a_kc', 'a_vc'), run('a_g', 'a_g'), zeros(GI_LANE - n_ag), run('c_if', 'c_if'),
                                    zeros(LANE - GI_LANE - 2 * NH_C)], axis=0).astype(BF16)


def _regroup(w_in_t, cols):
    nl, n_in, d = w_in_t.shape
    blk = lambda rows: pl.BlockSpec((None, rows, cols), lambda l, c: (l, 0, c))
    return pl.pallas_call(
        _regroup_kernel,
        grid=(nl, d // cols),
        in_specs=[blk(n_in)],
        out_specs=[blk(MAIN_W), blk(K_W), blk(VT_W), blk(3 * LANE)],
        out_shape=[jax.ShapeDtypeStruct((nl, MAIN_W, d), BF16), jax.ShapeDtypeStruct((nl, K_W, d), BF16),
                   jax.ShapeDtypeStruct((nl, VT_W, d), BF16), jax.ShapeDtypeStruct((nl, 3 * LANE, d), BF16)],
        compiler_params=_params("parallel", "parallel"),
        name="regroup_w_in",
    )(w_in_t)


def _prepare(w_in, cmp_wv2, sg_ln_g, sg_ln_b, sg_b,
             ml_conv_b, ml_b_i, ml_b_f, ml_norm_g, w_br_a, w_br_b, w_br_c, w_out, ple_w, ple_gate, ln_g, ln_b):
    nl = w_in.shape[0]
    prm = {}
    prm['w_main'], prm['w_k'], prm['w_vt'], prm['w_aux'] = _regroup(jnp.swapaxes(w_in, 1, 2), REGROUP_COLS)
    prm['wv2t'] = jnp.swapaxes(cmp_wv2, 1, 2)
    prm['sg_ln_g'], prm['sg_ln_b'] = sg_ln_g[:, None, :], sg_ln_b[:, None, :]
    prm['sg_b_t'] = jnp.swapaxes(sg_b, 1, 2)
    gbias = jnp.zeros((nl, 1, LANE), F32).at[:, 0, GI_LANE:GI_LANE + NH_C].set(ml_b_i)
    prm['gbias'] = gbias.at[:, 0, GF_LANE:GF_LANE + NH_C].set(ml_b_f)
    prm['conv_b'], prm['norm_g'] = ml_conv_b[:, None, :], ml_norm_g[:, None, :]
    for name, a in (('wa', w_br_a), ('wb', w_br_b), ('wc', w_br_c), ('wo', w_out), ('wp', ple_w), ('wg', ple_gate)):
        prm[name] = a.astype(BF16)
    prm['ln_g'], prm['ln_b'] = ln_g[:, None, :], ln_b[:, None, :]
    return prm


def _layer(i, x, xin, p, prm, cmp, sg_w, ml_conv_w, bsz, t_len):
    main = _matmul(xin, prm['w_main'], i, PROJ_TM, PROJ_TN, "proj_main")
    kk, vt, kc, vc, small = _tail_matmul(xin, prm['w_k'], prm['w_vt'], prm['w_aux'], i, TAIL_TM)

    pos_k, pos_v, wk1, wk2, wv1 = cmp
    kcmp, vcmp_t = _compress(kc, vc, pos_k, pos_v, wk1, wk2, wv1, prm['wv2t'], i, bsz, t_len)
    ya = _nsa(main, kk, vt, small, kcmp, vcmp_t, bsz, t_len)
    yc = _mlstm(main, vt, small, ml_conv_w, prm['conv_b'], prm['gbias'], prm['norm_g'], i, bsz, t_len)
    return _merge(x, ya, yc, main, p, prm['sg_ln_g'], prm['sg_ln_b'], sg_w, prm['sg_b_t'],
                  prm['wa'], prm['wb'], prm['wc'], prm['wo'], prm['wp'], prm['wg'], prm['ln_g'], prm['ln_b'], i, MERGE_TM)


def kernel(x, p, w_in, cmp_pos_k, cmp_pos_v, cmp_wk1, cmp_wk2, cmp_wv1, cmp_wv2, sg_ln_g, sg_ln_b, sg_w, sg_b,
           ml_conv_w, ml_conv_b, ml_b_i, ml_b_f, ml_norm_g, w_br_a, w_br_b, w_br_c, w_out, ple_w, ple_gate,
           ln_g, ln_b):
    bsz, t_len, d = x.shape
    assert d == D_MODEL and t_len == N_CMP_PAD * STRIDE_CMP and t_len // L_SEL == N_BLK
    assert t_len % SEQ_BLK_C == 0 and t_len % CK == 0 and w_in.shape[0] == DEPTH
    prm = _prepare(w_in, cmp_wv2, sg_ln_g, sg_ln_b, sg_b,
                   ml_conv_b, ml_b_i, ml_b_f, ml_norm_g, w_br_a, w_br_b, w_br_c, w_out, ple_w, ple_gate, ln_g, ln_b)
    xf = x.reshape(bsz * t_len, d)
    xin = xf
    pf = p.reshape(DEPTH, bsz * t_len, PLE_DIM)
    for i in range(DEPTH):
        outs = _layer(i, xf, xin, pf, prm, (cmp_pos_k, cmp_pos_v, cmp_wk1, cmp_wk2, cmp_wv1), sg_w, ml_conv_w,
                      bsz, t_len)
        xf, xin = outs[0], outs[-1]
    return xf.reshape(bsz, t_len, d)
```

```python
import jax
import jax.numpy as jnp
from jax import lax
from jax.experimental import pallas as pl
from jax.experimental.pallas import tpu as pltpu

F32 = jnp.float32
BF16 = jnp.bfloat16

D_MODEL = 1024
PLE_DIM = 256
D_A = 512
NH_A = 8
G_A = 2
HPG_A = 4
DK_A = 64
L_CMP = 32
STRIDE_CMP = 16
CMP_HID = 128
L_SEL = 64
N_SEL = 8
N_BLK = 32
WINDOW = 256
QB = 256
CK = 256
N_CMP_PAD = 128
BIG = 1e9
D_B = 512
CHUNK_B = 128
G_B = 4
D_C = 512
NH_C = 4
DH_C = 128
CONV_W = 4
CHUNK_C = 128
SEQ_BLK_C = 1024
LN_EPS = 1e-5
DEPTH = 2
ALPHA = (2.0 * DEPTH) ** 0.25

LANE = 128
SUBLANE = 8
_SEG_SIZES = (('a_q', D_A), ('a_kc', G_A * DK_A), ('a_vc', G_A * DK_A), ('a_ks', G_A * DK_A), ('a_vs', G_A * DK_A),
              ('a_kw', G_A * DK_A), ('a_vw', G_A * DK_A), ('a_g', 3 * NH_A), ('a_z', D_A),
              ('b_u', D_B), ('b_v', D_B), ('b_z', D_B),
              ('c_q', D_C), ('c_k', D_C), ('c_v', D_C), ('c_if', 2 * NH_C), ('c_o', D_C), ('c_z', D_C),
              ('m_g', 3 * D_MODEL))
SEG = {}
for _name, _size in _SEG_SIZES:
    _lo = sum(s for _, s in _SEG_SIZES[:len(SEG)])
    SEG[_name] = (_lo, _lo + _size)
MAIN_W = 7680
COL_AQ, COL_AZ, COL_BU, COL_BV, COL_BZ, COL_CQ, COL_CK, COL_CO, COL_CZ = 6, 7, 8, 9, 10, 11, 12, 13, 14
K_W = 256
NSA_VT_W = 512
VT_W = NSA_VT_W + 512
DVX = 128 + 16
GI_LANE = 32
GF_LANE = 36
V7X_VMEM_BYTES = 64 * 1024 * 1024
VMEM_LIMIT = V7X_VMEM_BYTES * 14 // 16
PROJ_TM, PROJ_TN = 2048, 2560
TAIL_TM = 2048
MERGE_TM = 512
REGROUP_COLS = 256
NEG = -1e30
LOG2E = 1.4426950408889634


def _params(*sem):
    return pltpu.CompilerParams(dimension_semantics=sem, vmem_limit_bytes=VMEM_LIMIT)


def _nt_dot(a, b):
    return lax.dot_general(a, b, (((1,), (1,)), ((), ())), preferred_element_type=F32)


def _dot(a, b):
    return jnp.dot(a, b, preferred_element_type=F32)


def _split3(a):
    a1 = a.astype(BF16)
    r1 = a - a1.astype(F32)
    a2 = r1.astype(BF16)
    a3 = (r1 - a2.astype(F32)).astype(BF16)
    return a1, a2, a3


def _sigmoid(x):
    return 0.5 * jnp.tanh(0.5 * x) + 0.5


def _silu(x):
    return x * _sigmoid(x)


def _log_sigmoid(x):
    return jnp.minimum(x, 0.0) - jnp.log1p(jnp.exp(-jnp.abs(x)))


def _layer_spec(a, layer, grid_rank):
    zeros = (0,) * (a.ndim - 1)
    if grid_rank == 1:
        return pl.BlockSpec((None,) + a.shape[1:], lambda i: (layer,) + zeros)
    return pl.BlockSpec((None,) + a.shape[1:], lambda i, j: (layer,) + zeros)


def _mm_kernel(x_ref, wt_ref, o_ref):
    o_ref[...] = _nt_dot(x_ref[...].astype(BF16), wt_ref[...]).astype(o_ref.dtype)


def _matmul(x, wt, layer, tm, tn, name):
    m, k = x.shape
    n = wt.shape[1]
    return pl.pallas_call(
        _mm_kernel,
        grid=(m // tm, n // tn),
        in_specs=[pl.BlockSpec((tm, k), lambda i, j: (i, 0)),
                  pl.BlockSpec((None, tn, k), lambda i, j: (layer, j, 0))],
        out_specs=pl.BlockSpec((tm, tn), lambda i, j: (i, j)),
        out_shape=jax.ShapeDtypeStruct((m, n), BF16),
        compiler_params=_params("parallel", "arbitrary"),
        name=name,
    )(x, wt)


def _tail_kernel(x_ref, wkt_ref, wvt_ref, wauxt_ref, ok_ref, ovt_ref, okc_ref, ovc_ref, osm_ref):
    x = x_ref[...].astype(BF16)
    ok_ref[...] = _nt_dot(x, wkt_ref[...]).astype(BF16)
    ovt_ref[...] = _nt_dot(wvt_ref[...], x).astype(BF16)
    aux = _nt_dot(x, wauxt_ref[...])
    okc_ref[...] = aux[:, 0:LANE]
    ovc_ref[...] = aux[:, LANE:2 * LANE]
    osm_ref[...] = aux[:, 2 * LANE:3 * LANE]


def _tail_matmul(x, wk, wvt, waux, layer, tm):
    m, k = x.shape
    tok = lambda w: pl.BlockSpec((tm, w), lambda i: (i, 0))
    f32_out = jax.ShapeDtypeStruct((m, LANE), F32)
    return pl.pallas_call(
        _tail_kernel,
        grid=(m // tm,),
        in_specs=[tok(k), _layer_spec(wk, layer, 1), _layer_spec(wvt, layer, 1), _layer_spec(waux, layer, 1)],
        out_specs=[tok(K_W), pl.BlockSpec((VT_W, tm), lambda i: (0, i)), tok(LANE), tok(LANE), tok(LANE)],
        out_shape=[jax.ShapeDtypeStruct((m, K_W), BF16), jax.ShapeDtypeStruct((VT_W, m), BF16),
                   f32_out, f32_out, f32_out],
        compiler_params=_params("parallel"),
        name="proj_tail",
    )(x, wk, wvt, waux)


def _compress_kernel(zk_ref, zv_ref, pk_ref, pv_ref, wk1_ref, wk2_ref, wv1_ref, wv2t_ref, ko_ref, vot_ref):
    n = zk_ref.shape[0] // STRIDE_CMP

    def hidden(z_ref, p_ref, w1_ref):
        r0 = [jnp.zeros((n, CMP_HID), F32) for _ in range(G_A)]
        r1 = [jnp.zeros((n, CMP_HID), F32) for _ in range(G_A)]
        for l in range(STRIDE_CMP):
            zl = z_ref[pl.ds(l, n, stride=STRIDE_CMP), :]
            w_a = w1_ref[l * DK_A:(l + 1) * DK_A, :].astype(BF16)
            w_b = w1_ref[(STRIDE_CMP + l) * DK_A:(STRIDE_CMP + l + 1) * DK_A, :].astype(BF16)
            for g in range(G_A):
                zg = zl[:, g * DK_A:(g + 1) * DK_A]
                r0[g] = r0[g] + _dot((zg + p_ref[l:l + 1, :]).astype(BF16), w_a)
                r1[g] = r1[g] + _dot((zg + p_ref[STRIDE_CMP + l:STRIDE_CMP + l + 1, :]).astype(BF16), w_b)
        return [jax.nn.gelu(r0[g] + pltpu.roll(r1[g], n - 1, 0)).astype(BF16) for g in range(G_A)]

    hk = hidden(zk_ref, pk_ref, wk1_ref)
    wk2 = wk2_ref[...].astype(BF16)
    for g in range(G_A):
        ko_ref[0, g] = _dot(hk[g], wk2).astype(BF16)
    hv = hidden(zv_ref, pv_ref, wv1_ref)
    wv2t = wv2t_ref[...].astype(BF16)
    vot_ref[0] = jnp.concatenate([_nt_dot(wv2t, hv[g]) for g in range(G_A)], axis=0).astype(BF16)


def _compress(kc, vc, pk, pv, wk1, wk2, wv1, wv2t, layer, bsz, t_len):
    n = t_len // STRIDE_CMP
    zspec = pl.BlockSpec((t_len, LANE), lambda i: (i, 0))
    full = lambda a: _layer_spec(a, layer, 1)
    return pl.pallas_call(
        _compress_kernel,
        grid=(bsz,),
        in_specs=[zspec, zspec, full(pk), full(pv), full(wk1), full(wk2), full(wv1), full(wv2t)],
        out_specs=[pl.BlockSpec((1, G_A, n, DK_A), lambda i: (i, 0, 0, 0)),
                   pl.BlockSpec((1, G_A * DK_A, n), lambda i: (i, 0, 0))],
        out_shape=[jax.ShapeDtypeStruct((bsz, G_A, n, DK_A), BF16), jax.ShapeDtypeStruct((bsz, G_A * DK_A, n), BF16)],
        compiler_params=_params("parallel"),
        name="nsa_compress",
    )(kc, vc, pk, pv, wk1, wk2, wv1, wv2t)


def _nsa_kernel(q_ref, z_ref, g_ref, kc_ref, vct_ref, k_ref, vt_ref, o_ref,
                s_sc, acc_sc, ow_sc, bw_sc, sel_sc, idx_sc):
    qb = pl.program_id(1)
    t0 = qb * QB
    cols = HPG_A * QB
    n_qtile = QB // LANE
    n_wchunk = (WINDOW + QB) // LANE
    n_wslot = n_wchunk + 1
    k_i = lax.broadcasted_iota(jnp.int32, (LANE, 1), 0)
    t_i = lax.broadcasted_iota(jnp.int32, (1, QB), 1)
    tk = (t_i - k_i).astype(F32)
    ones_row = jnp.where(lax.broadcasted_iota(jnp.int32, (LANE, CK), 0) == DK_A, 1.0, 0.0).astype(BF16)
    slopes = [[LOG2E * 2.0 ** -(g * HPG_A + h + 1) for h in range(HPG_A)] for g in range(G_A)]
    hs = [slice(h * QB, (h + 1) * QB) for h in range(HPG_A)]

    def biased(s, bias_of_head):
        return jnp.concatenate([s[:, hs[h]] + bias_of_head(h) for h in range(HPG_A)], axis=1)

    @pl.when((pl.program_id(0) == 0) & (qb == 0))
    def _():
        for g in range(G_A):
            for h in range(HPG_A):
                for j in range(n_wchunk):
                    dist = tk + float(WINDOW - j * LANE)
                    ok = (dist >= 0.0) & (dist < WINDOW)
                    bw_sc[g, h * n_wslot + j] = jnp.where(ok, -slopes[g][h] * dist, NEG)
                bw_sc[g, h * n_wslot + n_wchunk] = jnp.full((LANE, QB), NEG, F32)

    gsig = _sigmoid(g_ref[...])
    gates_t = jnp.concatenate([jnp.transpose(gsig[r * LANE:(r + 1) * LANE]) for r in range(n_qtile)], axis=1)

    jb = lax.broadcasted_iota(jnp.int32, (N_BLK, QB), 0)
    cur = (t0 + lax.broadcasted_iota(jnp.int32, (N_BLK, QB), 1)) >> 6
    forced = (jb == 0) | (jb == cur) | (jb == cur - 1)
    allowed = jb <= cur
    ov_j = lax.broadcasted_iota(jnp.int32, (N_BLK, N_CMP_PAD), 0) * L_SEL
    ov_n = lax.broadcasted_iota(jnp.int32, (N_BLK, N_CMP_PAD), 1) * STRIDE_CMP
    ov_t = jnp.where((ov_n < ov_j + L_SEL) & (ov_n + L_CMP > ov_j), 1.0, 0.0).astype(BF16)
    jb8 = lax.broadcasted_iota(jnp.int32, (SUBLANE, QB), 0)

    d_cmp = (t0 - (L_CMP - 1)).astype(F32) + (t_i - STRIDE_CMP * k_i).astype(F32)
    valid_cmp = (d_cmp >= 0.0) & (k_i < N_CMP_PAD - 1)

    gls = [slice(g * DK_A, (g + 1) * DK_A) for g in range(G_A)]
    q4s, o_cmps = [], []
    for g in range(G_A):
        q4 = jnp.concatenate(
            [q_ref[:, (g * HPG_A + h) * DK_A:(g * HPG_A + h + 1) * DK_A] for h in range(HPG_A)], axis=0)
        q4 = (q4.astype(F32) * (LOG2E * DK_A ** -0.5)).astype(BF16)
        q4s.append(q4)

        s = biased(_nt_dot(kc_ref[0, g], q4),
                   lambda h: jnp.where(valid_cmp, -slopes[g][h] * d_cmp, -jnp.inf))
        mx = jnp.max(s, axis=0, keepdims=True)
        mx = jnp.where(mx > -jnp.inf, mx, 0.0)
        e = jnp.exp2(s - mx)
        p = e * (1.0 / jnp.maximum(jnp.sum(e, axis=0, keepdims=True), 1e-30))
        o_cmps.append(_dot(vct_ref[0][gls[g], :], p.astype(BF16)))

        psum = p[:, hs[0]] + p[:, hs[1]] + p[:, hs[2]] + p[:, hs[3]]
        p1, p2, p3 = _split3(psum)
        imp = _dot(ov_t, p1) + _dot(ov_t, p2) + _dot(ov_t, p3)
        score = jnp.where(allowed, jnp.where(forced, BIG, imp), -BIG)
        tiles = [score[r * SUBLANE:(r + 1) * SUBLANE] for r in range(N_BLK // SUBLANE)]
        cnt = [jnp.zeros((SUBLANE, QB), F32) for _ in tiles]
        for j in range(N_BLK):
            sj = score[j:j + 1, :]
            for r, tile in enumerate(tiles):
                lo = r * SUBLANE
                if j >= lo + SUBLANE:
                    beats = sj > tile
                elif j < lo:
                    beats = sj >= tile
                else:
                    beats = (sj > tile) | ((sj >= tile) & (jb8 > j - lo))
                cnt[r] = cnt[r] + jnp.where(beats, 1.0, 0.0)
        sel_sc[g] = jnp.where((jnp.concatenate(cnt, axis=0) < N_SEL) & allowed, 1.0, 0.0)

        for r in range(n_qtile):
            q4_r = jnp.concatenate([q4[h * QB + r * LANE:h * QB + (r + 1) * LANE] for h in range(HPG_A)], axis=0)
            s_w, v_w = [], []
            for j in range(r, r + WINDOW // LANE + 1):
                c = qb * n_qtile - WINDOW // LANE + j
                k0 = pl.multiple_of(jnp.maximum(c, 0) * LANE, LANE)
                slot = jnp.where(c >= 0, j, n_wchunk)
                sj = _nt_dot(k_ref[pl.ds(k0, LANE), LANE + g * DK_A:LANE + (g + 1) * DK_A], q4_r)
                s_w.append(jnp.concatenate(
                    [sj[:, h * LANE:(h + 1) * LANE] + bw_sc[g, h * n_wslot + slot, :, r * LANE:(r + 1) * LANE]
                     for h in range(HPG_A)], axis=1))
                v_w.append(vt_ref[G_A * LANE + g * LANE:G_A * LANE + (g + 1) * LANE, pl.ds(k0, LANE)]
                           + ones_row[:, :LANE])
            mx = s_w[0]
            for sj in s_w[1:]:
                mx = jnp.maximum(mx, sj)
            mx = jnp.max(mx, axis=0, keepdims=True)
            o_win = _dot(v_w[0], jnp.exp2((s_w[0] - mx).astype(BF16)))
            for sj, vj in zip(s_w[1:], v_w[1:]):
                o_win = o_win + _dot(vj, jnp.exp2((sj - mx).astype(BF16)))
            ow_sc[g, r] = o_win

    chosen_any = jnp.maximum(sel_sc[0], sel_sc[1])
    blk_per_chunk = CK // L_SEL
    n_all = s_sc.shape[1]
    per_chunk = jnp.concatenate(
        [jnp.max(chosen_any[c * blk_per_chunk:(c + 1) * blk_per_chunk, :], axis=0, keepdims=True)
         for c in range(n_all)], axis=0)
    bit = jnp.exp2(lax.broadcasted_iota(jnp.int32, (n_all, 1), 0).astype(F32))
    mask = jnp.sum(jnp.max(per_chunk, axis=1, keepdims=True) * bit).astype(jnp.int32)
    n_causal = ((qb + 1) * QB - 1) // CK + 1
    n_chunk = jnp.int32(0)
    for c in range(n_all):
        need = (((mask >> c) & 1) == 1) & (c < n_causal)
        idx_sc[n_chunk] = c
        n_chunk = n_chunk + jnp.where(need, 1, 0)

    def score_chunk(i, m_runs):
        c = idx_sc[i]
        k0 = pl.multiple_of(c * CK, CK)
        out = []
        for g in range(G_A):
            sc = _nt_dot(k_ref[pl.ds(k0, CK), gls[g]], q4s[g])
            halves = []
            for u in range(CK // LANE):
                dist = tk + (t0 - k0 - u * LANE).astype(F32)
                blk0 = (CK // L_SEL) * c + (LANE // L_SEL) * u
                chosen = jnp.concatenate(
                    [jnp.broadcast_to(sel_sc[g, pl.ds(blk0 + b, 1), :], (L_SEL, QB))
                     for b in range(LANE // L_SEL)], axis=0)
                ok = (chosen > 0.5) & (dist >= 0.0)
                far = jnp.where(ok, dist, -NEG)
                halves.append(biased(sc[u * LANE:(u + 1) * LANE], lambda h: far * -slopes[g][h]))
            sc = jnp.concatenate(halves, axis=0)
            s_sc[g, i] = sc
            out.append(jnp.maximum(m_runs[g], jnp.max(sc.reshape(CK // SUBLANE, SUBLANE, cols), axis=0)))
        return tuple(out)

    m_runs = lax.fori_loop(0, n_chunk, score_chunk, (jnp.full((SUBLANE, cols), NEG, F32),) * G_A)
    m_rows = [jnp.max(m_runs[g], axis=0, keepdims=True) for g in range(G_A)]
    acc_sc[...] = jnp.zeros(acc_sc.shape, F32)

    def value_chunk(i, carry):
        k0 = pl.multiple_of(idx_sc[i] * CK, CK)
        for g in range(G_A):
            pc = jnp.exp2((s_sc[g, i] - m_rows[g]).astype(BF16))
            acc_sc[g] += _dot(vt_ref[g * LANE:(g + 1) * LANE, pl.ds(k0, CK)] + ones_row, pc)
        return carry

    lax.fori_loop(0, n_chunk, value_chunk, 0)

    pieces = []
    for g in range(G_A):
        o_sel = acc_sc[g]
        for h in range(HPG_A):
            col = g * HPG_A + h
            os_h = o_sel[:, hs[h]]
            ow_h = jnp.concatenate([ow_sc[g, r, :, h * LANE:(h + 1) * LANE] for r in range(n_qtile)], axis=1)
            w_sel = gates_t[NH_A + col:NH_A + col + 1, :] / jnp.maximum(os_h[DK_A:DK_A + 1, :], 1e-30)
            w_win = gates_t[2 * NH_A + col:2 * NH_A + col + 1, :] / jnp.maximum(ow_h[DK_A:DK_A + 1, :], 1e-30)
            pieces.append(gates_t[col:col + 1, :] * o_cmps[g][:, hs[h]] + w_sel * os_h[:DK_A] + w_win * ow_h[:DK_A])

    out_t = jnp.concatenate(pieces, axis=0)
    out = jnp.concatenate(
        [jnp.concatenate([jnp.transpose(out_t[i * LANE:(i + 1) * LANE, r * LANE:(r + 1) * LANE])
                          for i in range(D_A // LANE)], axis=1) for r in range(n_qtile)], axis=0)
    o_ref[...] = (out * _silu(z_ref[...].astype(F32))).astype(BF16)


def _nsa(main, kk, vt, small, kcmp, vcmp_t, bsz, t_len):
    nqb = t_len // QB
    cols = HPG_A * QB
    n_wslot = (WINDOW + QB) // LANE + 1
    return pl.pallas_call(
        _nsa_kernel,
        grid=(bsz, nqb),
        in_specs=[pl.BlockSpec((QB, D_A), lambda b, i: (b * nqb + i, COL_AQ)),
                  pl.BlockSpec((QB, D_A), lambda b, i: (b * nqb + i, COL_AZ)),
                  pl.BlockSpec((QB, LANE), lambda b, i: (b * nqb + i, 0)),
                  pl.BlockSpec((1, G_A, N_CMP_PAD, DK_A), lambda b, i: (b, 0, 0, 0)),
                  pl.BlockSpec((1, G_A * DK_A, N_CMP_PAD), lambda b, i: (b, 0, 0)),
                  pl.BlockSpec((t_len, K_W), lambda b, i: (b, 0)),
                  pl.BlockSpec((NSA_VT_W, t_len), lambda b, i: (0, b))],
        out_specs=pl.BlockSpec((QB, D_A), lambda b, i: (b * nqb + i, 0)),
        out_shape=jax.ShapeDtypeStruct((bsz * t_len, D_A), BF16),
        scratch_shapes=[pltpu.VMEM((G_A, t_len // CK, CK, cols), F32), pltpu.VMEM((G_A, LANE, cols), F32),
                        pltpu.VMEM((G_A, QB // LANE, LANE, HPG_A * LANE), F32),
                        pltpu.VMEM((G_A, HPG_A * n_wslot, LANE, QB), F32), pltpu.VMEM((G_A, N_BLK, QB), F32),
                        pltpu.SMEM((t_len // CK + 1,), jnp.int32)],
        compiler_params=_params("arbitrary", "arbitrary"),
        name="nsa_attention",
    )(main, main, small, kcmp, vcmp_t, kk, vt)


def _sgu_tile(u_ref, v_ref, z_ref, lng_ref, lnb_ref, w_ref, b_ref):
    u = jax.nn.gelu(u_ref[...].astype(F32))
    v = jax.nn.gelu(v_ref[...].astype(F32))
    mu = jnp.mean(v, axis=-1, keepdims=True)
    var = jnp.mean(jnp.square(v - mu), axis=-1, keepdims=True)
    vn = ((v - mu) * lax.rsqrt(var + LN_EPS) * lng_ref[...] + lnb_ref[...]).astype(BF16)
    gate = u * _silu(z_ref[...].astype(F32))
    ti = lax.broadcasted_iota(jnp.int32, (CHUNK_B, CHUNK_B), 0)
    si = lax.broadcasted_iota(jnp.int32, (CHUNK_B, CHUNK_B), 1)
    ws = [jnp.where(si <= ti, w_ref[g], 0.0).astype(BF16) for g in range(G_B)]
    rows = []
    for c in range(u_ref.shape[0] // CHUNK_B):
        rs = slice(c * CHUNK_B, (c + 1) * CHUNK_B)
        rows.append(jnp.concatenate(
            [gate[rs, g * LANE:(g + 1) * LANE] * (_dot(ws[g], vn[rs, g * LANE:(g + 1) * LANE]) + b_ref[:, g:g + 1])
             for g in range(G_B)], axis=1))
    return jnp.concatenate(rows, axis=0)


def _mlstm_kernel(q_ref, k_ref, vt_ref, o_ref, z_ref, g_ref, cw_ref, cb_ref, gb_ref, ng_ref, y_ref,
                  qc_sc, kc_sc, halo_sc, st_sc, m_sc):
    sb = pl.program_id(1)
    blk = q_ref.shape[0]

    @pl.when(sb == 0)
    def _():
        halo_sc[...] = jnp.zeros(halo_sc.shape, F32)
        st_sc[...] = jnp.zeros(st_sc.shape, F32)
        m_sc[...] = jnp.zeros(m_sc.shape, F32)

    r8 = lax.broadcasted_iota(jnp.int32, (SUBLANE, 1), 0)

    def conv_silu(x_ref, which, dst_ref, scale):
        x = x_ref[...].astype(F32)
        prev = halo_sc[which]
        w = cw_ref[:, which * D_C:(which + 1) * D_C]
        acc = x * w[CONV_W - 1:CONV_W, :] + cb_ref[:, which * D_C:(which + 1) * D_C]
        for j in range(1, CONV_W):
            rolled = pltpu.roll(x, j, 0)
            head = jnp.where(r8 < j, pltpu.roll(prev, j, 0), rolled[0:SUBLANE])
            shifted = jnp.concatenate([head, rolled[SUBLANE:]], axis=0)
            acc = acc + shifted * w[CONV_W - 1 - j:CONV_W - j, :]
        halo_sc[which] = x[blk - SUBLANE:blk]
        dst_ref[...] = (_silu(acc) * scale).astype(BF16)

    conv_silu(q_ref, 0, qc_sc, 1.0)
    conv_silu(k_ref, 1, kc_sc, DH_C ** -0.5)

    si = lax.broadcasted_iota(jnp.int32, (CHUNK_C, CHUNK_C), 0)
    ti = lax.broadcasted_iota(jnp.int32, (CHUNK_C, CHUNK_C), 1)
    causal = si <= ti
    tri_u = jnp.where(causal, 1.0, 0.0).astype(BF16)
    ones_rows = jnp.where(lax.broadcasted_iota(jnp.int32, (DVX - DH_C, CHUNK_C), 0) == 0, 1.0, 0.0).astype(BF16)
    ng = ng_ref[...]
    gbias = gb_ref[...]

    for c in range(blk // CHUNK_C):
        rs = slice(c * CHUNK_C, (c + 1) * CHUNK_C)
        gate_rows = jnp.transpose(g_ref[rs, :] + gbias)[GI_LANE:GI_LANE + 2 * NH_C]
        f1, f2, f3 = _split3(_log_sigmoid(gate_rows))
        bc_rows = _dot(f1, tri_u) + _dot(f2, tri_u) + _dot(f3, tri_u)
        g_rows = gate_rows[0:NH_C] - bc_rows[NH_C:2 * NH_C]
        g_cols = jnp.transpose(jnp.concatenate([g_rows, jnp.zeros((CHUNK_C - NH_C, CHUNK_C), F32)], axis=0))
        for h in range(NH_C):
            ls = slice(h * DH_C, (h + 1) * DH_C)
            q = qc_sc[rs, ls]
            k = kc_sc[rs, ls]
            vext = jnp.concatenate([vt_ref[ls, rs], ones_rows], axis=0)
            g_col = g_cols[:, h:h + 1]
            g_row = g_rows[h:h + 1, :]
            bc_row = bc_rows[NH_C + h:NH_C + h + 1, :]
            b_last = bc_row[:, CHUNK_C - 1:CHUNK_C]
            m_prev = m_sc[h][:, 0:1]
            st = st_sc[h]
            m_loc = b_last + jnp.max(g_row, axis=1, keepdims=True)
            e_end = jnp.exp(b_last + g_row - m_loc)

            gmat = jnp.where(causal, g_col, -jnp.inf)
            mg = jnp.maximum(m_prev, jnp.max(gmat, axis=0, keepdims=True))
            s_t = _nt_dot(k, q) * jnp.exp(gmat - mg)
            e_int = jnp.exp(m_prev - mg)
            both = _dot(vext, s_t.astype(BF16)) + _nt_dot(st.astype(BF16), q) * e_int
            den = both[DH_C:DH_C + 1, :]
            hval = both[:DH_C] * (1.0 / jnp.maximum(jnp.abs(den), jnp.exp(-(bc_row + mg))))
            mu = jnp.mean(hval, axis=0, keepdims=True)
            var = jnp.mean(jnp.square(hval - mu), axis=0, keepdims=True)
            hn = jnp.transpose((hval - mu) * lax.rsqrt(var + LN_EPS))
            o_gate = o_ref[rs, ls].astype(F32)
            z_gate = z_ref[rs, ls].astype(F32)
            gated = hn * ng[:, ls] * _sigmoid(o_gate) * _silu(z_gate)
            y_ref[rs, ls] = gated.astype(BF16)

            loc = _dot((vext.astype(F32) * e_end).astype(BF16), k)
            m_new = jnp.maximum(b_last + m_prev, m_loc)
            st_sc[h] = jnp.exp(b_last + m_prev - m_new) * st + jnp.exp(m_loc - m_new) * loc
            m_sc[h] = jnp.broadcast_to(m_new, (1, LANE))


def _mlstm(main, vt, small, conv_w, conv_b, gbias, norm_g, layer, bsz, t_len):
    nsb = t_len // SEQ_BLK_C
    blk = lambda col: pl.BlockSpec((SEQ_BLK_C, D_C), lambda b, s: (b * nsb + s, col))
    full = lambda a: _layer_spec(a, layer, 2)
    return pl.pallas_call(
        _mlstm_kernel,
        grid=(bsz, nsb),
        in_specs=[blk(COL_CQ), blk(COL_CK),
                  pl.BlockSpec((D_C, SEQ_BLK_C), lambda b, s: (NSA_VT_W // D_C, b * nsb + s)),
                  blk(COL_CO), blk(COL_CZ),
                  pl.BlockSpec((SEQ_BLK_C, LANE), lambda b, s: (b * nsb + s, 0)),
                  full(conv_w), full(conv_b), full(gbias), full(norm_g)],
        out_specs=pl.BlockSpec((SEQ_BLK_C, D_C), lambda b, s: (b * nsb + s, 0)),
        out_shape=jax.ShapeDtypeStruct((bsz * t_len, D_C), BF16),
        scratch_shapes=[pltpu.VMEM((SEQ_BLK_C, D_C), BF16), pltpu.VMEM((SEQ_BLK_C, D_C), BF16),
                        pltpu.VMEM((2, SUBLANE, D_C), F32), pltpu.VMEM((NH_C, DVX, DH_C), F32),
                        pltpu.VMEM((NH_C, 1, LANE), F32)],
        compiler_params=_params("parallel", "arbitrary"),
        name="mlstm",
    )(main, main, vt, main, main, small, conv_w, conv_b, gbias, norm_g)


def _merge_kernel(x_ref, ya_ref, yc_ref, u_ref, v_ref, z_ref, g0_ref, g1_ref, g2_ref, p_ref,
                  sgg_ref, sgb_ref, sgw_ref, sgbt_ref,
                  wa_ref, wb_ref, wc_ref, wo_ref, wp_ref, wg_ref, lng_ref, lnb_ref, o_ref, *ob_ref):
    yb = _sgu_tile(u_ref, v_ref, z_ref, sgg_ref, sgb_ref, sgw_ref, sgbt_ref).astype(BF16)
    merged = (_sigmoid(g0_ref[...].astype(F32)) * _dot(ya_ref[...], wa_ref[...])
              + _sigmoid(g1_ref[...].astype(F32)) * _dot(yb, wb_ref[...])
              + _sigmoid(g2_ref[...].astype(F32)) * _dot(yc_ref[...], wc_ref[...]))
    r = ALPHA * x_ref[...] + _dot(merged.astype(BF16), wo_ref[...])
    r = r + _sigmoid(_dot(r.astype(BF16), wg_ref[...])) * _dot(p_ref[...].astype(BF16), wp_ref[...])
    mu = jnp.mean(r, axis=-1, keepdims=True)
    var = jnp.mean(jnp.square(r - mu), axis=-1, keepdims=True)
    y = (r - mu) * lax.rsqrt(var + LN_EPS) * lng_ref[...] + lnb_ref[...]
    o_ref[...] = y
    for ref in ob_ref:
        ref[...] = y.astype(BF16)


def _merge(x, ya, yc, main, p, sgg, sgb, sgw, sgbt, wa, wb, wc, wo, wp, wg, lng, lnb, layer, tm):
    n = x.shape[0]
    n_out = 1 if layer == DEPTH - 1 else 2
    row = lambda w: pl.BlockSpec((tm, w), lambda i: (i, 0))
    blk = lambda col: pl.BlockSpec((tm, D_B), lambda i: (i, col))
    gate = lambda j: pl.BlockSpec((tm, D_MODEL), lambda i: (i, j))
    full = lambda a: _layer_spec(a, layer, 1)
    return pl.pallas_call(
        _merge_kernel,
        grid=(n // tm,),
        in_specs=[row(D_MODEL), row(D_A), row(D_C), blk(COL_BU), blk(COL_BV), blk(COL_BZ),
                  gate(0), gate(1), gate(2), pl.BlockSpec((None, tm, PLE_DIM), lambda i: (layer, i, 0)),
                  full(sgg), full(sgb), full(sgw), full(sgbt),
                  full(wa), full(wb), full(wc), full(wo), full(wp), full(wg), full(lng), full(lnb)],
        out_specs=[row(D_MODEL)] * n_out,
        out_shape=[jax.ShapeDtypeStruct((n, D_MODEL), F32), jax.ShapeDtypeStruct((n, D_MODEL), BF16)][:n_out],
        compiler_params=_params("parallel"),
        name="merge",
    )(x, ya, yc, main, main, main, main, main, main, p, sgg, sgb, sgw, sgbt, wa, wb, wc, wo, wp, wg, lng, lnb)


def _regroup_kernel(wt_ref, main_ref, k_ref, v_ref, aux_ref):
    wt = wt_ref[...]
    cols = wt.shape[1]
    run = lambda first, last: wt[SEG[first][0]:SEG[last][1]]
    zeros = lambda n: jnp.zeros((n, cols), F32)
    main_ref[...] = jnp.concatenate([run('m_g', 'm_g'), run('a_q', 'a_q'), run('a_z', 'c_k'), run('c_o', 'c_z')],
                                    axis=0).astype(BF16)
    k_ref[...] = jnp.concatenate([run('a_ks', 'a_ks'), run('a_kw', 'a_kw')], axis=0).astype(BF16)
    v_rows = []
    for name in ('a_vs', 'a_vw'):
        for g in range(G_A):
            lo = SEG[name][0] + g * DK_A
            v_rows += [wt[lo:lo + DK_A], zeros(LANE - DK_A)]
    v_ref[...] = jnp.concatenate(v_rows + [run('c_v', 'c_v')], axis=0).astype(BF16)
    n_ag = SEG['a_g'][1] - SEG['a_g'][0]
    aux_ref[...] = jnp.concatenate([run('a_kc', 'a_vc'), run('a_g', 'a_g'), zeros(GI_LANE - n_ag), run('c_if', 'c_if'),
                                    zeros(LANE - GI_LANE - 2 * NH_C)], axis=0).astype(BF16)


def _regroup(w_in_t, cols):
    nl, n_in, d = w_in_t.shape
    blk = lambda rows: pl.BlockSpec((None, rows, cols), lambda l, c: (l, 0, c))
    return pl.pallas_call(
        _regroup_kernel,
        grid=(nl, d // cols),
        in_specs=[blk(n_in)],
        out_specs=[blk(MAIN_W), blk(K_W), blk(VT_W), blk(3 * LANE)],
        out_shape=[jax.ShapeDtypeStruct((nl, MAIN_W, d), BF16), jax.ShapeDtypeStruct((nl, K_W, d), BF16),
                   jax.ShapeDtypeStruct((nl, VT_W, d), BF16), jax.ShapeDtypeStruct((nl, 3 * LANE, d), BF16)],
        compiler_params=_params("parallel", "parallel"),
        name="regroup_w_in",
    )(w_in_t)


def _prepare(w_in, cmp_wv2, sg_ln_g, sg_ln_b, sg_b,
             ml_conv_b, ml_b_i, ml_b_f, ml_norm_g, w_br_a, w_br_b, w_br_c, w_out, ple_w, ple_gate, ln_g, ln_b):
    nl = w_in.shape[0]
    prm = {}
    prm['w_main'], prm['w_k'], prm['w_vt'], prm['w_aux'] = _regroup(jnp.swapaxes(w_in, 1, 2), REGROUP_COLS)
    prm['wv2t'] = jnp.swapaxes(cmp_wv2, 1, 2)
    prm['sg_ln_g'], prm['sg_ln_b'] = sg_ln_g[:, None, :], sg_ln_b[:, None, :]
    prm['sg_b_t'] = jnp.swapaxes(sg_b, 1, 2)
    gbias = jnp.zeros((nl, 1, LANE), F32).at[:, 0, GI_LANE:GI_LANE + NH_C].set(ml_b_i)
    prm['gbias'] = gbias.at[:, 0, GF_LANE:GF_LANE + NH_C].set(ml_b_f)
    prm['conv_b'], prm['norm_g'] = ml_conv_b[:, None, :], ml_norm_g[:, None, :]
    for name, a in (('wa', w_br_a), ('wb', w_br_b), ('wc', w_br_c), ('wo', w_out), ('wp', ple_w), ('wg', ple_gate)):
        prm[name] = a.astype(BF16)
    prm['ln_g'], prm['ln_b'] = ln_g[:, None, :], ln_b[:, None, :]
    return prm


def _layer(i, x, xin, p, prm, cmp, sg_w, ml_conv_w, bsz, t_len):
    main = _matmul(xin, prm['w_main'], i, PROJ_TM, PROJ_TN, "proj_main")
    kk, vt, kc, vc, small = _tail_matmul(xin, prm['w_k'], prm['w_vt'], prm['w_aux'], i, TAIL_TM)

    pos_k, pos_v, wk1, wk2, wv1 = cmp
    kcmp, vcmp_t = _compress(kc, vc, pos_k, pos_v, wk1, wk2, wv1, prm['wv2t'], i, bsz, t_len)
    ya = _nsa(main, kk, vt, small, kcmp, vcmp_t, bsz, t_len)
    yc = _mlstm(main, vt, small, ml_conv_w, prm['conv_b'], prm['gbias'], prm['norm_g'], i, bsz, t_len)
    return _merge(x, ya, yc, main, p, prm['sg_ln_g'], prm['sg_ln_b'], sg_w, prm['sg_b_t'],
                  prm['wa'], prm['wb'], prm['wc'], prm['wo'], prm['wp'], prm['wg'], prm['ln_g'], prm['ln_b'], i, MERGE_TM)


def kernel(x, p, w_in, cmp_pos_k, cmp_pos_v, cmp_wk1, cmp_wk2, cmp_wv1, cmp_wv2, sg_ln_g, sg_ln_b, sg_w, sg_b,
           ml_conv_w, ml_conv_b, ml_b_i, ml_b_f, ml_norm_g, w_br_a, w_br_b, w_br_c, w_out, ple_w, ple_gate,
           ln_g, ln_b):
    bsz, t_len, d = x.shape
    assert d == D_MODEL and t_len == N_CMP_PAD * STRIDE_CMP and t_len // L_SEL == N_BLK
    assert t_len % SEQ_BLK_C == 0 and t_len % CK == 0 and w_in.shape[0] == DEPTH
    prm = _prepare(w_in, cmp_wv2, sg_ln_g, sg_ln_b, sg_b,
                   ml_conv_b, ml_b_i, ml_b_f, ml_norm_g, w_br_a, w_br_b, w_br_c, w_out, ple_w, ple_gate, ln_g, ln_b)
    xf = x.reshape(bsz * t_len, d)
    xin = xf
    pf = p.reshape(DEPTH, bsz * t_len, PLE_DIM)
    for i in range(DEPTH):
        outs = _layer(i, xf, xin, pf, prm, (cmp_pos_k, cmp_pos_v, cmp_wk1, cmp_wk2, cmp_wv1), sg_w, ml_conv_w,
                      bsz, t_len)
        xf, xin = outs[0], outs[-1]
    return xf.reshape(bsz, t_len, d)
```

```python
import jax
import jax.numpy as jnp
from jax import lax
from jax.experimental import pallas as pl
from jax.experimental.pallas import tpu as pltpu

F32 = jnp.float32
BF16 = jnp.bfloat16

D_MODEL = 1024
PLE_DIM = 256
D_A = 512
NH_A = 8
G_A = 2
HPG_A = 4
DK_A = 64
L_CMP = 32
STRIDE_CMP = 16
CMP_HID = 128
L_SEL = 64
N_SEL = 8
N_BLK = 32
WINDOW = 256
QB = 256
CK = 256
N_CMP_PAD = 128
BIG = 1e9
D_B = 512
CHUNK_B = 128
G_B = 4
D_C = 512
NH_C = 4
DH_C = 128
CONV_W = 4
CHUNK_C = 128
SEQ_BLK_C = 1024
LN_EPS = 1e-5
DEPTH = 2
ALPHA = (2.0 * DEPTH) ** 0.25

LANE = 128
SUBLANE = 8
_SEG_SIZES = (('a_q', D_A), ('a_kc', G_A * DK_A), ('a_vc', G_A * DK_A), ('a_ks', G_A * DK_A), ('a_vs', G_A * DK_A),
              ('a_kw', G_A * DK_A), ('a_vw', G_A * DK_A), ('a_g', 3 * NH_A), ('a_z', D_A),
              ('b_u', D_B), ('b_v', D_B), ('b_z', D_B),
              ('c_q', D_C), ('c_k', D_C), ('c_v', D_C), ('c_if', 2 * NH_C), ('c_o', D_C), ('c_z', D_C),
              ('m_g', 3 * D_MODEL))
SEG = {}
for _name, _size in _SEG_SIZES:
    _lo = sum(s for _, s in _SEG_SIZES[:len(SEG)])
    SEG[_name] = (_lo, _lo + _size)
MAIN_W = 7680
COL_AQ, COL_AZ, COL_BU, COL_BV, COL_BZ, COL_CQ, COL_CK, COL_CO, COL_CZ = 6, 7, 8, 9, 10, 11, 12, 13, 14
K_W = 256
NSA_VT_W = 512
VT_W = NSA_VT_W + 512
DVX = 128 + 16
GI_LANE = 32
GF_LANE = 36
V7X_VMEM_BYTES = 64 * 1024 * 1024
VMEM_LIMIT = V7X_VMEM_BYTES * 14 // 16
PROJ_TM, PROJ_TN = 2048, 2560
TAIL_TM = 2048
MERGE_TM = 512
MERGE_SLAB = 256
REGROUP_COLS = 256
NEG = -1e30
LOG2E = 1.4426950408889634


def _params(*sem):
    return pltpu.CompilerParams(dimension_semantics=sem, vmem_limit_bytes=VMEM_LIMIT)


def _nt_dot(a, b):
    return lax.dot_general(a, b, (((1,), (1,)), ((), ())), preferred_element_type=F32)


def _dot(a, b):
    return jnp.dot(a, b, preferred_element_type=F32)


def _split3(a):
    a1 = a.astype(BF16)
    r1 = a - a1.astype(F32)
    a2 = r1.astype(BF16)
    a3 = (r1 - a2.astype(F32)).astype(BF16)
    return a1, a2, a3


def _sigmoid(x):
    return 0.5 * jnp.tanh(0.5 * x) + 0.5


def _silu(x):
    return x * _sigmoid(x)


def _log_sigmoid(x):
    return jnp.minimum(x, 0.0) - jnp.log1p(jnp.exp(-jnp.abs(x)))


def _layer_spec(a, layer, grid_rank):
    zeros = (0,) * (a.ndim - 1)
    if grid_rank == 1:
        return pl.BlockSpec((None,) + a.shape[1:], lambda i: (layer,) + zeros)
    return pl.BlockSpec((None,) + a.shape[1:], lambda i, j: (layer,) + zeros)


def _mm_kernel(x_ref, wt_ref, o_ref):
    o_ref[...] = _nt_dot(x_ref[...].astype(BF16), wt_ref[...]).astype(o_ref.dtype)


def _matmul(x, wt, layer, tm, tn, name):
    m, k = x.shape
    n = wt.shape[1]
    return pl.pallas_call(
        _mm_kernel,
        grid=(m // tm, n // tn),
        in_specs=[pl.BlockSpec((tm, k), lambda i, j: (i, 0)),
                  pl.BlockSpec((None, tn, k), lambda i, j: (layer, j, 0))],
        out_specs=pl.BlockSpec((tm, tn), lambda i, j: (i, j)),
        out_shape=jax.ShapeDtypeStruct((m, n), BF16),
        compiler_params=_params("parallel", "arbitrary"),
        name=name,
    )(x, wt)


def _tail_kernel(x_ref, wkt_ref, wvt_ref, wauxt_ref, ok_ref, ovt_ref, okc_ref, ovc_ref, osm_ref):
    x = x_ref[...].astype(BF16)
    ok_ref[...] = _nt_dot(x, wkt_ref[...]).astype(BF16)
    ovt_ref[...] = _nt_dot(wvt_ref[...], x).astype(BF16)
    aux = _nt_dot(x, wauxt_ref[...])
    okc_ref[...] = aux[:, 0:LANE]
    ovc_ref[...] = aux[:, LANE:2 * LANE]
    osm_ref[...] = aux[:, 2 * LANE:3 * LANE]


def _tail_matmul(x, wk, wvt, waux, layer, tm):
    m, k = x.shape
    tok = lambda w: pl.BlockSpec((tm, w), lambda i: (i, 0))
    f32_out = jax.ShapeDtypeStruct((m, LANE), F32)
    return pl.pallas_call(
        _tail_kernel,
        grid=(m // tm,),
        in_specs=[tok(k), _layer_spec(wk, layer, 1), _layer_spec(wvt, layer, 1), _layer_spec(waux, layer, 1)],
        out_specs=[tok(K_W), pl.BlockSpec((VT_W, tm), lambda i: (0, i)), tok(LANE), tok(LANE), tok(LANE)],
        out_shape=[jax.ShapeDtypeStruct((m, K_W), BF16), jax.ShapeDtypeStruct((VT_W, m), BF16),
                   f32_out, f32_out, f32_out],
        compiler_params=_params("parallel"),
        name="proj_tail",
    )(x, wk, wvt, waux)


def _compress_kernel(zk_ref, zv_ref, pk_ref, pv_ref, wk1_ref, wk2_ref, wv1_ref, wv2t_ref, ko_ref, vot_ref):
    n = zk_ref.shape[0] // STRIDE_CMP

    def hidden(z_ref, p_ref, w1_ref):
        r0 = [jnp.zeros((n, CMP_HID), F32) for _ in range(G_A)]
        r1 = [jnp.zeros((n, CMP_HID), F32) for _ in range(G_A)]
        for l in range(STRIDE_CMP):
            zl = z_ref[pl.ds(l, n, stride=STRIDE_CMP), :]
            w_a = w1_ref[l * DK_A:(l + 1) * DK_A, :].astype(BF16)
            w_b = w1_ref[(STRIDE_CMP + l) * DK_A:(STRIDE_CMP + l + 1) * DK_A, :].astype(BF16)
            for g in range(G_A):
                zg = zl[:, g * DK_A:(g + 1) * DK_A]
                r0[g] = r0[g] + _dot((zg + p_ref[l:l + 1, :]).astype(BF16), w_a)
                r1[g] = r1[g] + _dot((zg + p_ref[STRIDE_CMP + l:STRIDE_CMP + l + 1, :]).astype(BF16), w_b)
        return [jax.nn.gelu(r0[g] + pltpu.roll(r1[g], n - 1, 0)).astype(BF16) for g in range(G_A)]

    hk = hidden(zk_ref, pk_ref, wk1_ref)
    wk2 = wk2_ref[...].astype(BF16)
    for g in range(G_A):
        ko_ref[0, g] = _dot(hk[g], wk2).astype(BF16)
    hv = hidden(zv_ref, pv_ref, wv1_ref)
    wv2t = wv2t_ref[...].astype(BF16)
    vot_ref[0] = jnp.concatenate([_nt_dot(wv2t, hv[g]) for g in range(G_A)], axis=0).astype(BF16)


def _compress(kc, vc, pk, pv, wk1, wk2, wv1, wv2t, layer, bsz, t_len):
    n = t_len // STRIDE_CMP
    zspec = pl.BlockSpec((t_len, LANE), lambda i: (i, 0))
    full = lambda a: _layer_spec(a, layer, 1)
    return pl.pallas_call(
        _compress_kernel,
        grid=(bsz,),
        in_specs=[zspec, zspec, full(pk), full(pv), full(wk1), full(wk2), full(wv1), full(wv2t)],
        out_specs=[pl.BlockSpec((1, G_A, n, DK_A), lambda i: (i, 0, 0, 0)),
                   pl.BlockSpec((1, G_A * DK_A, n), lambda i: (i, 0, 0))],
        out_shape=[jax.ShapeDtypeStruct((bsz, G_A, n, DK_A), BF16), jax.ShapeDtypeStruct((bsz, G_A * DK_A, n), BF16)],
        compiler_params=_params("parallel"),
        name="nsa_compress",
    )(kc, vc, pk, pv, wk1, wk2, wv1, wv2t)


def _nsa_kernel(q_ref, z_ref, g_ref, kc_ref, vct_ref, k_ref, vt_ref, o_ref,
                s_sc, acc_sc, ow_sc, bw_sc, sel_sc, idx_sc):
    qb = pl.program_id(1)
    t0 = qb * QB
    cols = HPG_A * QB
    n_qtile = QB // LANE
    n_wchunk = (WINDOW + QB) // LANE
    n_wslot = n_wchunk + 1
    k_i = lax.broadcasted_iota(jnp.int32, (LANE, 1), 0)
    t_i = lax.broadcasted_iota(jnp.int32, (1, QB), 1)
    tk = (t_i - k_i).astype(F32)
    ones_row = jnp.where(lax.broadcasted_iota(jnp.int32, (LANE, CK), 0) == DK_A, 1.0, 0.0).astype(BF16)
    slopes = [[LOG2E * 2.0 ** -(g * HPG_A + h + 1) for h in range(HPG_A)] for g in range(G_A)]
    hs = [slice(h * QB, (h + 1) * QB) for h in range(HPG_A)]

    def biased(s, bias_of_head):
        return jnp.concatenate([s[:, hs[h]] + bias_of_head(h) for h in range(HPG_A)], axis=1)

    @pl.when((pl.program_id(0) == 0) & (qb == 0))
    def _():
        for g in range(G_A):
            for h in range(HPG_A):
                for j in range(n_wchunk):
                    dist = tk + float(WINDOW - j * LANE)
                    ok = (dist >= 0.0) & (dist < WINDOW)
                    bw_sc[g, h * n_wslot + j] = jnp.where(ok, -slopes[g][h] * dist, NEG)
                bw_sc[g, h * n_wslot + n_wchunk] = jnp.full((LANE, QB), NEG, F32)

    gsig = _sigmoid(g_ref[...])
    gates_t = jnp.concatenate([jnp.transpose(gsig[r * LANE:(r + 1) * LANE]) for r in range(n_qtile)], axis=1)

    jb = lax.broadcasted_iota(jnp.int32, (N_BLK, QB), 0)
    cur = (t0 + lax.broadcasted_iota(jnp.int32, (N_BLK, QB), 1)) >> 6
    forced = (jb == 0) | (jb == cur) | (jb == cur - 1)
    allowed = jb <= cur
    ov_j = lax.broadcasted_iota(jnp.int32, (N_BLK, N_CMP_PAD), 0) * L_SEL
    ov_n = lax.broadcasted_iota(jnp.int32, (N_BLK, N_CMP_PAD), 1) * STRIDE_CMP
    ov_t = jnp.where((ov_n < ov_j + L_SEL) & (ov_n + L_CMP > ov_j), 1.0, 0.0).astype(BF16)
    jb8 = lax.broadcasted_iota(jnp.int32, (SUBLANE, QB), 0)

    d_cmp = (t0 - (L_CMP - 1)).astype(F32) + (t_i - STRIDE_CMP * k_i).astype(F32)
    valid_cmp = (d_cmp >= 0.0) & (k_i < N_CMP_PAD - 1)

    gls = [slice(g * DK_A, (g + 1) * DK_A) for g in range(G_A)]
    q4s, o_cmps = [], []
    for g in range(G_A):
        q4 = jnp.concatenate(
            [q_ref[:, (g * HPG_A + h) * DK_A:(g * HPG_A + h + 1) * DK_A] for h in range(HPG_A)], axis=0)
        q4 = (q4.astype(F32) * (LOG2E * DK_A ** -0.5)).astype(BF16)
        q4s.append(q4)

        s = biased(_nt_dot(kc_ref[0, g], q4),
                   lambda h: jnp.where(valid_cmp, -slopes[g][h] * d_cmp, -jnp.inf))
        mx = jnp.max(s, axis=0, keepdims=True)
        mx = jnp.where(mx > -jnp.inf, mx, 0.0)
        e = jnp.exp2(s - mx)
        p = e * (1.0 / jnp.maximum(jnp.sum(e, axis=0, keepdims=True), 1e-30))
        o_cmps.append(_dot(vct_ref[0][gls[g], :], p.astype(BF16)))

        psum = p[:, hs[0]] + p[:, hs[1]] + p[:, hs[2]] + p[:, hs[3]]
        p1, p2, p3 = _split3(psum)
        imp = _dot(ov_t, p1) + _dot(ov_t, p2) + _dot(ov_t, p3)
        score = jnp.where(allowed, jnp.where(forced, BIG, imp), -BIG)
        tiles = [score[r * SUBLANE:(r + 1) * SUBLANE] for r in range(N_BLK // SUBLANE)]
        cnt = [jnp.zeros((SUBLANE, QB), F32) for _ in tiles]
        for j in range(N_BLK):
            sj = score[j:j + 1, :]
            for r, tile in enumerate(tiles):
                lo = r * SUBLANE
                if j >= lo + SUBLANE:
                    beats = sj > tile
                elif j < lo:
                    beats = sj >= tile
                else:
                    beats = (sj > tile) | ((sj >= tile) & (jb8 > j - lo))
                cnt[r] = cnt[r] + jnp.where(beats, 1.0, 0.0)
        sel_sc[g] = jnp.where((jnp.concatenate(cnt, axis=0) < N_SEL) & allowed, 1.0, 0.0)

        for r in range(n_qtile):
            q4_r = jnp.concatenate([q4[h * QB + r * LANE:h * QB + (r + 1) * LANE] for h in range(HPG_A)], axis=0)
            s_w, v_w = [], []
            for j in range(r, r + WINDOW // LANE + 1):
                c = qb * n_qtile - WINDOW // LANE + j
                k0 = pl.multiple_of(jnp.maximum(c, 0) * LANE, LANE)
                slot = jnp.where(c >= 0, j, n_wchunk)
                sj = _nt_dot(k_ref[pl.ds(k0, LANE), LANE + g * DK_A:LANE + (g + 1) * DK_A], q4_r)
                s_w.append(jnp.concatenate(
                    [sj[:, h * LANE:(h + 1) * LANE] + bw_sc[g, h * n_wslot + slot, :, r * LANE:(r + 1) * LANE]
                     for h in range(HPG_A)], axis=1))
                v_w.append(vt_ref[G_A * LANE + g * LANE:G_A * LANE + (g + 1) * LANE, pl.ds(k0, LANE)]
                           + ones_row[:, :LANE])
            mx = s_w[0]
            for sj in s_w[1:]:
                mx = jnp.maximum(mx, sj)
            mx = jnp.max(mx, axis=0, keepdims=True)
            o_win = _dot(v_w[0], jnp.exp2((s_w[0] - mx).astype(BF16)))
            for sj, vj in zip(s_w[1:], v_w[1:]):
                o_win = o_win + _dot(vj, jnp.exp2((sj - mx).astype(BF16)))
            ow_sc[g, r] = o_win

    chosen_any = jnp.maximum(sel_sc[0], sel_sc[1])
    blk_per_chunk = CK // L_SEL
    n_all = s_sc.shape[1]
    per_chunk = jnp.concatenate(
        [jnp.max(chosen_any[c * blk_per_chunk:(c + 1) * blk_per_chunk, :], axis=0, keepdims=True)
         for c in range(n_all)], axis=0)
    bit = jnp.exp2(lax.broadcasted_iota(jnp.int32, (n_all, 1), 0).astype(F32))
    mask = jnp.sum(jnp.max(per_chunk, axis=1, keepdims=True) * bit).astype(jnp.int32)
    n_causal = ((qb + 1) * QB - 1) // CK + 1
    n_chunk = jnp.int32(0)
    for c in range(n_all):
        need = (((mask >> c) & 1) == 1) & (c < n_causal)
        idx_sc[n_chunk] = c
        n_chunk = n_chunk + jnp.where(need, 1, 0)

    def score_chunk(i, m_runs):
        c = idx_sc[i]
        k0 = pl.multiple_of(c * CK, CK)
        out = []
        for g in range(G_A):
            sc = _nt_dot(k_ref[pl.ds(k0, CK), gls[g]], q4s[g])
            halves = []
            for u in range(CK // LANE):
                dist = tk + (t0 - k0 - u * LANE).astype(F32)
                blk0 = (CK // L_SEL) * c + (LANE // L_SEL) * u
                chosen = jnp.concatenate(
                    [jnp.broadcast_to(sel_sc[g, pl.ds(blk0 + b, 1), :], (L_SEL, QB))
                     for b in range(LANE // L_SEL)], axis=0)
                ok = (chosen > 0.5) & (dist >= 0.0)
                far = jnp.where(ok, dist, -NEG)
                halves.append(biased(sc[u * LANE:(u + 1) * LANE], lambda h: far * -slopes[g][h]))
            sc = jnp.concatenate(halves, axis=0)
            s_sc[g, i] = sc
            out.append(jnp.maximum(m_runs[g], jnp.max(sc.reshape(CK // SUBLANE, SUBLANE, cols), axis=0)))
        return tuple(out)

    m_runs = lax.fori_loop(0, n_chunk, score_chunk, (jnp.full((SUBLANE, cols), NEG, F32),) * G_A)
    m_rows = [jnp.max(m_runs[g], axis=0, keepdims=True) for g in range(G_A)]
    acc_sc[...] = jnp.zeros(acc_sc.shape, F32)

    def value_chunk(i, carry):
        k0 = pl.multiple_of(idx_sc[i] * CK, CK)
        for g in range(G_A):
            pc = jnp.exp2((s_sc[g, i] - m_rows[g]).astype(BF16))
            acc_sc[g] += _dot(vt_ref[g * LANE:(g + 1) * LANE, pl.ds(k0, CK)] + ones_row, pc)
        return carry

    lax.fori_loop(0, n_chunk, value_chunk, 0)

    pieces = []
    for g in range(G_A):
        o_sel = acc_sc[g]
        for h in range(HPG_A):
            col = g * HPG_A + h
            os_h = o_sel[:, hs[h]]
            ow_h = jnp.concatenate([ow_sc[g, r, :, h * LANE:(h + 1) * LANE] for r in range(n_qtile)], axis=1)
            w_sel = gates_t[NH_A + col:NH_A + col + 1, :] / jnp.maximum(os_h[DK_A:DK_A + 1, :], 1e-30)
            w_win = gates_t[2 * NH_A + col:2 * NH_A + col + 1, :] / jnp.maximum(ow_h[DK_A:DK_A + 1, :], 1e-30)
            pieces.append(gates_t[col:col + 1, :] * o_cmps[g][:, hs[h]] + w_sel * os_h[:DK_A] + w_win * ow_h[:DK_A])

    out_t = jnp.concatenate(pieces, axis=0)
    out = jnp.concatenate(
        [jnp.concatenate([jnp.transpose(out_t[i * LANE:(i + 1) * LANE, r * LANE:(r + 1) * LANE])
                          for i in range(D_A // LANE)], axis=1) for r in range(n_qtile)], axis=0)
    o_ref[...] = (out * _silu(z_ref[...].astype(F32))).astype(BF16)


def _nsa(main, kk, vt, small, kcmp, vcmp_t, bsz, t_len):
    nqb = t_len // QB
    cols = HPG_A * QB
    n_wslot = (WINDOW + QB) // LANE + 1
    return pl.pallas_call(
        _nsa_kernel,
        grid=(bsz, nqb),
        in_specs=[pl.BlockSpec((QB, D_A), lambda b, i: (b * nqb + i, COL_AQ)),
                  pl.BlockSpec((QB, D_A), lambda b, i: (b * nqb + i, COL_AZ)),
                  pl.BlockSpec((QB, LANE), lambda b, i: (b * nqb + i, 0)),
                  pl.BlockSpec((1, G_A, N_CMP_PAD, DK_A), lambda b, i: (b, 0, 0, 0)),
                  pl.BlockSpec((1, G_A * DK_A, N_CMP_PAD), lambda b, i: (b, 0, 0)),
                  pl.BlockSpec((t_len, K_W), lambda b, i: (b, 0)),
                  pl.BlockSpec((NSA_VT_W, t_len), lambda b, i: (0, b))],
        out_specs=pl.BlockSpec((QB, D_A), lambda b, i: (b * nqb + i, 0)),
        out_shape=jax.ShapeDtypeStruct((bsz * t_len, D_A), BF16),
        scratch_shapes=[pltpu.VMEM((G_A, t_len // CK, CK, cols), F32), pltpu.VMEM((G_A, LANE, cols), F32),
                        pltpu.VMEM((G_A, QB // LANE, LANE, HPG_A * LANE), F32),
                        pltpu.VMEM((G_A, HPG_A * n_wslot, LANE, QB), F32), pltpu.VMEM((G_A, N_BLK, QB), F32),
                        pltpu.SMEM((t_len // CK + 1,), jnp.int32)],
        compiler_params=_params("arbitrary", "arbitrary"),
        name="nsa_attention",
    )(main, main, small, kcmp, vcmp_t, kk, vt)


def _sgu_tile(u_ref, v_ref, z_ref, lng_ref, lnb_ref, w_ref, b_ref):
    u = jax.nn.gelu(u_ref[...].astype(F32))
    v = jax.nn.gelu(v_ref[...].astype(F32))
    mu = jnp.mean(v, axis=-1, keepdims=True)
    var = jnp.mean(jnp.square(v - mu), axis=-1, keepdims=True)
    vn = ((v - mu) * lax.rsqrt(var + LN_EPS) * lng_ref[...] + lnb_ref[...]).astype(BF16)
    gate = u * _silu(z_ref[...].astype(F32))
    ti = lax.broadcasted_iota(jnp.int32, (CHUNK_B, CHUNK_B), 0)
    si = lax.broadcasted_iota(jnp.int32, (CHUNK_B, CHUNK_B), 1)
    ws = [jnp.where(si <= ti, w_ref[g], 0.0).astype(BF16) for g in range(G_B)]
    rows = []
    for c in range(u_ref.shape[0] // CHUNK_B):
        rs = slice(c * CHUNK_B, (c + 1) * CHUNK_B)
        rows.append(jnp.concatenate(
            [gate[rs, g * LANE:(g + 1) * LANE] * (_dot(ws[g], vn[rs, g * LANE:(g + 1) * LANE]) + b_ref[:, g:g + 1])
             for g in range(G_B)], axis=1))
    return jnp.concatenate(rows, axis=0)


def _mlstm_kernel(q_ref, k_ref, vt_ref, o_ref, z_ref, g_ref, cw_ref, cb_ref, gb_ref, ng_ref, y_ref,
                  qc_sc, kc_sc, halo_sc, st_sc, m_sc):
    sb = pl.program_id(1)
    blk = q_ref.shape[0]

    @pl.when(sb == 0)
    def _():
        halo_sc[...] = jnp.zeros(halo_sc.shape, F32)
        st_sc[...] = jnp.zeros(st_sc.shape, F32)
        m_sc[...] = jnp.zeros(m_sc.shape, F32)

    r8 = lax.broadcasted_iota(jnp.int32, (SUBLANE, 1), 0)

    def conv_silu(x_ref, which, dst_ref, scale):
        x = x_ref[...].astype(F32)
        prev = halo_sc[which]
        w = cw_ref[:, which * D_C:(which + 1) * D_C]
        acc = x * w[CONV_W - 1:CONV_W, :] + cb_ref[:, which * D_C:(which + 1) * D_C]
        for j in range(1, CONV_W):
            rolled = pltpu.roll(x, j, 0)
            head = jnp.where(r8 < j, pltpu.roll(prev, j, 0), rolled[0:SUBLANE])
            shifted = jnp.concatenate([head, rolled[SUBLANE:]], axis=0)
            acc = acc + shifted * w[CONV_W - 1 - j:CONV_W - j, :]
        halo_sc[which] = x[blk - SUBLANE:blk]
        dst_ref[...] = (_silu(acc) * scale).astype(BF16)

    conv_silu(q_ref, 0, qc_sc, 1.0)
    conv_silu(k_ref, 1, kc_sc, DH_C ** -0.5)

    si = lax.broadcasted_iota(jnp.int32, (CHUNK_C, CHUNK_C), 0)
    ti = lax.broadcasted_iota(jnp.int32, (CHUNK_C, CHUNK_C), 1)
    causal = si <= ti
    tri_u = jnp.where(causal, 1.0, 0.0).astype(BF16)
    ones_rows = jnp.where(lax.broadcasted_iota(jnp.int32, (DVX - DH_C, CHUNK_C), 0) == 0, 1.0, 0.0).astype(BF16)
    ng = ng_ref[...]
    gbias = gb_ref[...]

    for c in range(blk // CHUNK_C):
        rs = slice(c * CHUNK_C, (c + 1) * CHUNK_C)
        gate_rows = jnp.transpose(g_ref[rs, :] + gbias)[GI_LANE:GI_LANE + 2 * NH_C]
        f1, f2, f3 = _split3(_log_sigmoid(gate_rows))
        bc_rows = _dot(f1, tri_u) + _dot(f2, tri_u) + _dot(f3, tri_u)
        g_rows = gate_rows[0:NH_C] - bc_rows[NH_C:2 * NH_C]
        g_cols = jnp.transpose(jnp.concatenate([g_rows, jnp.zeros((CHUNK_C - NH_C, CHUNK_C), F32)], axis=0))
        for h in range(NH_C):
            ls = slice(h * DH_C, (h + 1) * DH_C)
            q = qc_sc[rs, ls]
            k = kc_sc[rs, ls]
            vext = jnp.concatenate([vt_ref[ls, rs], ones_rows], axis=0)
            g_col = g_cols[:, h:h + 1]
            g_row = g_rows[h:h + 1, :]
            bc_row = bc_rows[NH_C + h:NH_C + h + 1, :]
            b_last = bc_row[:, CHUNK_C - 1:CHUNK_C]
            m_prev = m_sc[h][:, 0:1]
            st = st_sc[h]
            m_loc = b_last + jnp.max(g_row, axis=1, keepdims=True)
            e_end = jnp.exp(b_last + g_row - m_loc)

            gmat = jnp.where(causal, g_col, -jnp.inf)
            mg = jnp.maximum(m_prev, jnp.max(gmat, axis=0, keepdims=True))
            s_t = _nt_dot(k, q) * jnp.exp(gmat - mg)
            e_int = jnp.exp(m_prev - mg)
            both = _dot(vext, s_t.astype(BF16)) + _nt_dot(st.astype(BF16), q) * e_int
            den = both[DH_C:DH_C + 1, :]
            hval = both[:DH_C] * (1.0 / jnp.maximum(jnp.abs(den), jnp.exp(-(bc_row + mg))))
            mu = jnp.mean(hval, axis=0, keepdims=True)
            var = jnp.mean(jnp.square(hval - mu), axis=0, keepdims=True)
            hn = jnp.transpose((hval - mu) * lax.rsqrt(var + LN_EPS))
            o_gate = o_ref[rs, ls].astype(F32)
            z_gate = z_ref[rs, ls].astype(F32)
            gated = hn * ng[:, ls] * _sigmoid(o_gate) * _silu(z_gate)
            y_ref[rs, ls] = gated.astype(BF16)

            loc = _dot((vext.astype(F32) * e_end).astype(BF16), k)
            m_new = jnp.maximum(b_last + m_prev, m_loc)
            st_sc[h] = jnp.exp(b_last + m_prev - m_new) * st + jnp.exp(m_loc - m_new) * loc
            m_sc[h] = jnp.broadcast_to(m_new, (1, LANE))


def _mlstm(main, vt, small, conv_w, conv_b, gbias, norm_g, layer, bsz, t_len):
    nsb = t_len // SEQ_BLK_C
    blk = lambda col: pl.BlockSpec((SEQ_BLK_C, D_C), lambda b, s: (b * nsb + s, col))
    full = lambda a: _layer_spec(a, layer, 2)
    return pl.pallas_call(
        _mlstm_kernel,
        grid=(bsz, nsb),
        in_specs=[blk(COL_CQ), blk(COL_CK),
                  pl.BlockSpec((D_C, SEQ_BLK_C), lambda b, s: (NSA_VT_W // D_C, b * nsb + s)),
                  blk(COL_CO), blk(COL_CZ),
                  pl.BlockSpec((SEQ_BLK_C, LANE), lambda b, s: (b * nsb + s, 0)),
                  full(conv_w), full(conv_b), full(gbias), full(norm_g)],
        out_specs=pl.BlockSpec((SEQ_BLK_C, D_C), lambda b, s: (b * nsb + s, 0)),
        out_shape=jax.ShapeDtypeStruct((bsz * t_len, D_C), BF16),
        scratch_shapes=[pltpu.VMEM((SEQ_BLK_C, D_C), BF16), pltpu.VMEM((SEQ_BLK_C, D_C), BF16),
                        pltpu.VMEM((2, SUBLANE, D_C), F32), pltpu.VMEM((NH_C, DVX, DH_C), F32),
                        pltpu.VMEM((NH_C, 1, LANE), F32)],
        compiler_params=_params("parallel", "arbitrary"),
        name="mlstm",
    )(main, main, vt, main, main, small, conv_w, conv_b, gbias, norm_g)


def _merge_kernel(x_ref, ya_ref, yc_ref, u_ref, v_ref, z_ref, g0_ref, g1_ref, g2_ref, p_ref,
                  sgg_ref, sgb_ref, sgw_ref, sgbt_ref,
                  wa_ref, wb_ref, wc_ref, wo_ref, wp_ref, wg_ref, lng_ref, lnb_ref, o_ref, *ob_ref):
    yb = _sgu_tile(u_ref, v_ref, z_ref, sgg_ref, sgb_ref, sgw_ref, sgbt_ref).astype(BF16)
    ya, yc = ya_ref[...], yc_ref[...]
    slabs = []
    for c in range(D_MODEL // MERGE_SLAB):
        cs = slice(c * MERGE_SLAB, (c + 1) * MERGE_SLAB)
        slabs.append((_sigmoid(g0_ref[:, cs].astype(F32)) * _dot(ya, wa_ref[:, cs])
                      + _sigmoid(g1_ref[:, cs].astype(F32)) * _dot(yb, wb_ref[:, cs])
                      + _sigmoid(g2_ref[:, cs].astype(F32)) * _dot(yc, wc_ref[:, cs])).astype(BF16))
    merged = jnp.concatenate(slabs, axis=1)
    r = ALPHA * x_ref[...] + _dot(merged, wo_ref[...])
    r = r + _sigmoid(_dot(r.astype(BF16), wg_ref[...])) * _dot(p_ref[...].astype(BF16), wp_ref[...])
    mu = jnp.mean(r, axis=-1, keepdims=True)
    var = jnp.mean(jnp.square(r - mu), axis=-1, keepdims=True)
    y = (r - mu) * lax.rsqrt(var + LN_EPS) * lng_ref[...] + lnb_ref[...]
    o_ref[...] = y
    for ref in ob_ref:
        ref[...] = y.astype(BF16)


def _merge(x, ya, yc, main, p, sgg, sgb, sgw, sgbt, wa, wb, wc, wo, wp, wg, lng, lnb, layer, tm):
    n = x.shape[0]
    n_out = 1 if layer == DEPTH - 1 else 2
    row = lambda w: pl.BlockSpec((tm, w), lambda i: (i, 0))
    blk = lambda col: pl.BlockSpec((tm, D_B), lambda i: (i, col))
    gate = lambda j: pl.BlockSpec((tm, D_MODEL), lambda i: (i, j))
    full = lambda a: _layer_spec(a, layer, 1)
    return pl.pallas_call(
        _merge_kernel,
        grid=(n // tm,),
        in_specs=[row(D_MODEL), row(D_A), row(D_C), blk(COL_BU), blk(COL_BV), blk(COL_BZ),
                  gate(0), gate(1), gate(2), pl.BlockSpec((None, tm, PLE_DIM), lambda i: (layer, i, 0)),
                  full(sgg), full(sgb), full(sgw), full(sgbt),
                  full(wa), full(wb), full(wc), full(wo), full(wp), full(wg), full(lng), full(lnb)],
        out_specs=[row(D_MODEL)] * n_out,
        out_shape=[jax.ShapeDtypeStruct((n, D_MODEL), F32), jax.ShapeDtypeStruct((n, D_MODEL), BF16)][:n_out],
        compiler_params=_params("parallel"),
        name="merge",
    )(x, ya, yc, main, main, main, main, main, main, p, sgg, sgb, sgw, sgbt, wa, wb, wc, wo, wp, wg, lng, lnb)


def _regroup_kernel(wt_ref, main_ref, k_ref, v_ref, aux_ref):
    wt = wt_ref[...]
    cols = wt.shape[1]
    run = lambda first, last: wt[SEG[first][0]:SEG[last][1]]
    zeros = lambda n: jnp.zeros((n, cols), F32)
    main_ref[...] = jnp.concatenate([run('m_g', 'm_g'), run('a_q', 'a_q'), run('a_z', 'c_k'), run('c_o', 'c_z')],
                                    axis=0).astype(BF16)
    k_ref[...] = jnp.concatenate([run('a_ks', 'a_ks'), run('a_kw', 'a_kw')], axis=0).astype(BF16)
    v_rows = []
    for name in ('a_vs', 'a_vw'):
        for g in range(G_A):
            lo = SEG[name][0] + g * DK_A
            v_rows += [wt[lo:lo + DK_A], zeros(LANE - DK_A)]
    v_ref[...] = jnp.concatenate(v_rows + [run('c_v', 'c_v')], axis=0).astype(BF16)
    n_ag = SEG['a_g'][1] - SEG['a_g'][0]
    aux_ref[...] = jnp.concatenate([run('a_kc', 'a_vc'), run('a_g', 'a_g'), zeros(GI_LANE - n_ag), run('c_if', 'c_if'),
                                    zeros(LANE - GI_LANE - 2 * NH_C)], axis=0).astype(BF16)


def _regroup(w_in_t, cols):
    nl, n_in, d = w_in_t.shape
    blk = lambda rows: pl.BlockSpec((None, rows, cols), lambda l, c: (l, 0, c))
    return pl.pallas_call(
        _regroup_kernel,
        grid=(nl, d // cols),
        in_specs=[blk(n_in)],
        out_specs=[blk(MAIN_W), blk(K_W), blk(VT_W), blk(3 * LANE)],
        out_shape=[jax.ShapeDtypeStruct((nl, MAIN_W, d), BF16), jax.ShapeDtypeStruct((nl, K_W, d), BF16),
                   jax.ShapeDtypeStruct((nl, VT_W, d), BF16), jax.ShapeDtypeStruct((nl, 3 * LANE, d), BF16)],
        compiler_params=_params("parallel", "parallel"),
        name="regroup_w_in",
    )(w_in_t)


def _prepare(w_in, cmp_wv2, sg_ln_g, sg_ln_b, sg_b,
             ml_conv_b, ml_b_i, ml_b_f, ml_norm_g, w_br_a, w_br_b, w_br_c, w_out, ple_w, ple_gate, ln_g, ln_b):
    nl = w_in.shape[0]
    prm = {}
    prm['w_main'], prm['w_k'], prm['w_vt'], prm['w_aux'] = _regroup(jnp.swapaxes(w_in, 1, 2), REGROUP_COLS)
    prm['wv2t'] = jnp.swapaxes(cmp_wv2, 1, 2)
    prm['sg_ln_g'], prm['sg_ln_b'] = sg_ln_g[:, None, :], sg_ln_b[:, None, :]
    prm['sg_b_t'] = jnp.swapaxes(sg_b, 1, 2)
    gbias = jnp.zeros((nl, 1, LANE), F32).at[:, 0, GI_LANE:GI_LANE + NH_C].set(ml_b_i)
    prm['gbias'] = gbias.at[:, 0, GF_LANE:GF_LANE + NH_C].set(ml_b_f)
    prm['conv_b'], prm['norm_g'] = ml_conv_b[:, None, :], ml_norm_g[:, None, :]
    for name, a in (('wa', w_br_a), ('wb', w_br_b), ('wc', w_br_c), ('wo', w_out), ('wp', ple_w), ('wg', ple_gate)):
        prm[name] = a.astype(BF16)
    prm['ln_g'], prm['ln_b'] = ln_g[:, None, :], ln_b[:, None, :]
    return prm


def _layer(i, x, xin, p, prm, cmp, sg_w, ml_conv_w, bsz, t_len):
    main = _matmul(xin, prm['w_main'], i, PROJ_TM, PROJ_TN, "proj_main")
    kk, vt, kc, vc, small = _tail_matmul(xin, prm['w_k'], prm['w_vt'], prm['w_aux'], i, TAIL_TM)

    pos_k, pos_v, wk1, wk2, wv1 = cmp
    kcmp, vcmp_t = _compress(kc, vc, pos_k, pos_v, wk1, wk2, wv1, prm['wv2t'], i, bsz, t_len)
    ya = _nsa(main, kk, vt, small, kcmp, vcmp_t, bsz, t_len)
    yc = _mlstm(main, vt, small, ml_conv_w, prm['conv_b'], prm['gbias'], prm['norm_g'], i, bsz, t_len)
    return _merge(x, ya, yc, main, p, prm['sg_ln_g'], prm['sg_ln_b'], sg_w, prm['sg_b_t'],
                  prm['wa'], prm['wb'], prm['wc'], prm['wo'], prm['wp'], prm['wg'], prm['ln_g'], prm['ln_b'], i, MERGE_TM)


def kernel(x, p, w_in, cmp_pos_k, cmp_pos_v, cmp_wk1, cmp_wk2, cmp_wv1, cmp_wv2, sg_ln_g, sg_ln_b, sg_w, sg_b,
           ml_conv_w, ml_conv_b, ml_b_i, ml_b_f, ml_norm_g, w_br_a, w_br_b, w_br_c, w_out, ple_w, ple_gate,
           ln_g, ln_b):
    bsz, t_len, d = x.shape
    assert d == D_MODEL and t_len == N_CMP_PAD * STRIDE_CMP and t_len // L_SEL == N_BLK
    assert t_len % SEQ_BLK_C == 0 and t_len % CK == 0 and w_in.shape[0] == DEPTH
    prm = _prepare(w_in, cmp_wv2, sg_ln_g, sg_ln_b, sg_b,
                   ml_conv_b, ml_b_i, ml_b_f, ml_norm_g, w_br_a, w_br_b, w_br_c, w_out, ple_w, ple_gate, ln_g, ln_b)
    xf = x.reshape(bsz * t_len, d)
    xin = xf
    pf = p.reshape(DEPTH, bsz * t_len, PLE_DIM)
    for i in range(DEPTH):
        outs = _layer(i, xf, xin, pf, prm, (cmp_pos_k, cmp_pos_v, cmp_wk1, cmp_wk2, cmp_wv1), sg_w, ml_conv_w,
                      bsz, t_len)
        xf, xin = outs[0], outs[-1]
    return xf.reshape(bsz, t_len, d)
```
